```python
import math
import jax, jax.numpy as jnp
from jax import lax
import numpy as np

D_MODEL = 1024
BATCH = 8
SEQ = 2048
DEPTH = 4
DEC_BATCH = 128
DEC_SEQ = 1
PAST_LEN = 16384
PAGE_SIZE = 128

N_EVEN = (DEPTH + 1) // 2
N_ODD = DEPTH // 2
EPS = 1e-6

GLA_HEADS = 4
GLA_DK = D_MODEL // 8
GLA_DV = D_MODEL // 4
GLA_K_WIDTH = GLA_HEADS * GLA_DK
GLA_V_WIDTH = GLA_HEADS * GLA_DV
GLA_RANK = 16
GLA_GATE_NORM = 16.0
GLA_CHUNK = 64

S5_GROUP_CH = 16
S5_P = 64
S5_WIDTH = D_MODEL // 2
S5_GROUPS = S5_WIDTH // S5_GROUP_CH
S5_DT_MIN = 0.001
S5_DT_MAX = 0.1

DN_HEADS = 8
DN_DK = D_MODEL // 8
DN_DV = D_MODEL // 8
DN_K_WIDTH = DN_HEADS * DN_DK
DN_V_WIDTH = DN_HEADS * DN_DV
DN_CONV_DIM = 2 * DN_K_WIDTH + DN_V_WIDTH
CONV_W = 4
DN_CHUNK = 64

EVEN_SIZES = (GLA_K_WIDTH, GLA_K_WIDTH, GLA_V_WIDTH, GLA_V_WIDTH, GLA_RANK, S5_WIDTH, S5_WIDTH)
ODD_SIZES = (DN_CONV_DIM, DN_V_WIDTH, DN_HEADS, DN_HEADS)
EVEN_IN = 2 * GLA_K_WIDTH + 2 * GLA_V_WIDTH + GLA_RANK + 2 * S5_WIDTH
ODD_IN = DN_CONV_DIM + DN_V_WIDTH + 2 * DN_HEADS

kernel_name = 'hybrid_gla_s5_gdn_step'


def _offsets(sizes):
    out, acc = [], 0
    for s in sizes[:-1]:
        acc += s
        out.append(acc)
    return out


def rmsnorm(x, w):
    xf = x.astype(jnp.float32)
    y = xf * lax.rsqrt(jnp.mean(xf * xf, axis=-1, keepdims=True) + EPS)
    return (y * w.astype(jnp.float32)).astype(x.dtype)


def l2norm(x):
    return x * lax.rsqrt(jnp.sum(x * x, axis=-1, keepdims=True) + EPS)


def _to_chunks(x, c, n):
    pad = n * c - x.shape[1]
    x = jnp.pad(x, [(0, 0), (0, pad)] + [(0, 0)] * (x.ndim - 2))
    x = x.reshape(x.shape[0], n, c, *x.shape[2:])
    return jnp.swapaxes(jnp.moveaxis(x, 1, 0), 2, 3)


def _from_chunks(o, L):
    n, b, h, c, v = o.shape
    return o.transpose(1, 0, 3, 2, 4).reshape(b, n * c, h, v)[:, :L]


def gla_chunked(q, k, v, log_a, s0):
    L = q.shape[1]
    c = min(GLA_CHUNK, L)
    n = -(-L // c)
    xs = tuple(_to_chunks(t, c, n) for t in (q, k, v, log_a))
    tril = jnp.tril(jnp.ones((c, c), dtype=bool))

    def step(S, inp):
        qi, ki, vi, gi = inp
        b = jnp.cumsum(gi, axis=2)
        diff = b[:, :, :, None, :] - b[:, :, None, :, :]
        decay = jnp.exp(jnp.where(tril[:, :, None], diff, -jnp.inf))
        scores = jnp.einsum('bhik,bhjk,bhijk->bhij', qi, ki, decay)
        o = (jnp.einsum('bhik,bhkv->bhiv', qi * jnp.exp(b), S)
             + jnp.einsum('bhij,bhjv->bhiv', scores, vi))
        b_last = b[:, :, -1:, :]
        S = (S * jnp.exp(b_last)[:, :, 0, :, None]
             + jnp.einsum('bhjk,bhjv->bhkv', ki * jnp.exp(b_last - b), vi))
        return S, o

    S, o = lax.scan(step, s0, xs)
    return _from_chunks(o, L), S


def gated_delta_chunked(q, k, v, g, beta, s0):
    L = q.shape[1]
    c = min(DN_CHUNK, L)
    n = -(-L // c)
    xs = tuple(_to_chunks(t, c, n) for t in (q, k, v, g, beta))
    idx = jnp.arange(c)
    tril = idx[:, None] >= idx[None, :]
    strict = idx[:, None] > idx[None, :]
    eye = jnp.eye(c, dtype=jnp.float32)

    def step(S, inp):
        qi, ki, vi, gi, bi = inp
        gcum = jnp.cumsum(gi, axis=-1)
        decay = jnp.exp(jnp.where(tril, gcum[..., :, None] - gcum[..., None, :], -jnp.inf))
        kk = jnp.einsum('bhik,bhjk->bhij', ki, ki)
        lower = jnp.where(strict, bi[..., :, None] * kk * decay, 0.0)
        rhs = jnp.concatenate([vi * bi[..., None], ki * (bi * jnp.exp(gcum))[..., None]], axis=-1)
        sol = lax.linalg.triangular_solve(lower + eye, rhs, left_side=True, lower=True,
                                          unit_diagonal=True)
        u_v, w_k = sol[..., :DN_DV], sol[..., DN_DV:]
        v_new = u_v - jnp.einsum('bhik,bhkv->bhiv', w_k, S)
        scores = jnp.einsum('bhik,bhjk->bhij', qi, ki) * decay
        o = (jnp.einsum('bhik,bhkv->bhiv', qi * jnp.exp(gcum)[..., None], S)
             + jnp.einsum('bhij,bhjv->bhiv', scores, v_new))
        g_last = gcum[..., -1:]
        S = (S * jnp.exp(g_last)[..., None]
             + jnp.einsum('bhjk,bhjv->bhkv', ki * jnp.exp(g_last - gcum)[..., None], v_new))
        return S, o

    S, o = lax.scan(step, s0, xs)
    return _from_chunks(o, L), S


def s5_scan(u, h0_re, h0_im, lam_re, lam_im, log_dt, b_re, b_im, c_re, c_im, d):
    bsz, L, _ = u.shape
    ug = u.reshape(bsz, L, S5_GROUPS, S5_GROUP_CH)
    dt = jnp.exp(log_dt)[:, None]
    mag = jnp.exp(lam_re * dt)
    ar, ai = mag * jnp.cos(lam_im * dt), mag * jnp.sin(lam_im * dt)
    den = lam_re * lam_re + lam_im * lam_im
    wr = ((ar - 1.0) * lam_re + ai * lam_im) / den
    wi = (ai * lam_re - (ar - 1.0) * lam_im) / den
    bb_re = wr[..., None] * b_re - wi[..., None] * b_im
    bb_im = wr[..., None] * b_im + wi[..., None] * b_re
    bu_re = jnp.einsum('gph,blgh->blgp', bb_re, ug)
    bu_im = jnp.einsum('gph,blgh->blgp', bb_im, ug)
    bu_re = bu_re.at[:, 0].add(ar * h0_re - ai * h0_im)
    bu_im = bu_im.at[:, 0].add(ar * h0_im + ai * h0_re)
    a_re = jnp.broadcast_to(ar, bu_re.shape)
    a_im = jnp.broadcast_to(ai, bu_im.shape)

    def combine(e1, e2):
        ar1, ai1, br1, bi1 = e1
        ar2, ai2, br2, bi2 = e2
        return (ar1 * ar2 - ai1 * ai2, ar1 * ai2 + ai1 * ar2,
                ar2 * br1 - ai2 * bi1 + br2, ar2 * bi1 + ai2 * br1 + bi2)

    _, _, h_re, h_im = lax.associative_scan(combine, (a_re, a_im, bu_re, bu_im), axis=1)
    y = jnp.einsum('ghp,blgp->blgh', c_re, h_re) - jnp.einsum('ghp,blgp->blgh', c_im, h_im)
    y = y.reshape(bsz, L, S5_WIDTH) + d * u
    return y, h_re[:, -1], h_im[:, -1]


def causal_conv(x, buf, w):
    L = x.shape[1]
    xp = jnp.concatenate([buf, x], axis=1)
    y = xp[:, 0:L] * w[0]
    for i in range(1, CONV_W):
        y = y + xp[:, i:i + L] * w[i]
    return y, xp[:, L:]


def even_mixer(h, s_gla, s_re, s_im, w_in, w_gate_up, b_gate, gla_norm_w, lam_re, lam_im,
               log_dt, b_re, b_im, c_re, c_im, d, w_glu, b_glu, w_out):
    f32 = jnp.float32
    bsz, L, _ = h.shape
    q, k, v, r, lr, u, sg = jnp.split((h @ w_in).astype(f32), _offsets(EVEN_SIZES), axis=-1)
    q = q.reshape(bsz, L, GLA_HEADS, GLA_DK) * (GLA_DK ** -0.5)
    k = k.reshape(bsz, L, GLA_HEADS, GLA_DK)
    v = v.reshape(bsz, L, GLA_HEADS, GLA_DV)
    log_a = jax.nn.log_sigmoid(lr @ w_gate_up.astype(f32) + b_gate.astype(f32))
    log_a = log_a.reshape(bsz, L, GLA_HEADS, GLA_DK) / GLA_GATE_NORM
    o, s_gla_new = gla_chunked(q, k, v, log_a, s_gla.astype(f32))
    o = rmsnorm(o, gla_norm_w).reshape(bsz, L, GLA_V_WIDTH) * jax.nn.silu(r)
    y5, h_re, h_im = s5_scan(u, s_re.astype(f32), s_im.astype(f32), lam_re.astype(f32),
                             lam_im.astype(f32), log_dt.astype(f32), b_re.astype(f32),
                             b_im.astype(f32), c_re.astype(f32), c_im.astype(f32), d.astype(f32))
    y5 = jax.nn.gelu(y5)
    y5 = y5 * jax.nn.sigmoid(y5 @ w_glu.astype(f32) + b_glu.astype(f32)) * jax.nn.silu(sg)
    out = jnp.concatenate([o, y5], axis=-1).astype(h.dtype) @ w_out
    return out, s_gla_new, h_re, h_im


def odd_mixer(h, s_delta, s_conv, w_in, conv_w, a_log, dt_bias, norm_w, w_out):
    f32 = jnp.float32
    bsz, L, _ = h.shape
    qkv, z, b, a = jnp.split((h @ w_in).astype(f32), _offsets(ODD_SIZES), axis=-1)
    qkv, conv_new = causal_conv(qkv, s_conv.astype(f32), conv_w.astype(f32))
    qkv = jax.nn.silu(qkv)
    q, k, v = jnp.split(qkv, [DN_K_WIDTH, 2 * DN_K_WIDTH], axis=-1)
    q = l2norm(q.reshape(bsz, L, DN_HEADS, DN_DK)) * (DN_DK ** -0.5)
    k = l2norm(k.reshape(bsz, L, DN_HEADS, DN_DK))
    v = v.reshape(bsz, L, DN_HEADS, DN_DV)
    beta = jax.nn.sigmoid(b)
    g = -jnp.exp(a_log.astype(f32)) * jax.nn.softplus(a + dt_bias.astype(f32))
    o, s_new = gated_delta_chunked(q, k, v, g, beta, s_delta.astype(f32))
    o = rmsnorm(o, norm_w).reshape(bsz, L, DN_V_WIDTH) * jax.nn.silu(z)
    return o.astype(h.dtype) @ w_out, s_new, conv_new


def setup_inputs(seed: int = 0) -> dict:
    key = jax.random.key(seed)
    ks = iter(jax.random.split(key, 48))
    f32 = jnp.float32

    def nrm(shape, scale):
        return jax.random.normal(next(ks), shape, f32) * scale

    def unif(shape, lo, hi):
        return jax.random.uniform(next(ks), shape, f32, lo, hi)

    x_prompt = nrm((BATCH, SEQ, D_MODEL), 1.0)
    x_sample = nrm((DEC_BATCH, DEC_SEQ, D_MODEL), 1.0)
    state_gla = nrm((N_EVEN, DEC_BATCH, GLA_HEADS, GLA_DK, GLA_DV), 0.5)
    state_s5_re = nrm((N_EVEN, DEC_BATCH, S5_GROUPS, S5_P), 0.1)
    state_s5_im = nrm((N_EVEN, DEC_BATCH, S5_GROUPS, S5_P), 0.1)
    state_delta = nrm((N_ODD, DEC_BATCH, DN_HEADS, DN_DK, DN_DV), 0.5)
    state_conv = nrm((N_ODD, DEC_BATCH, CONV_W - 1, DN_CONV_DIM), 1.0)
    norm_w = 1.0 + nrm((DEPTH, D_MODEL), 0.02)
    final_norm_w = 1.0 + nrm((D_MODEL,), 0.02)
    w_in_even = nrm((N_EVEN, D_MODEL, EVEN_IN), D_MODEL ** -0.5)
    gla_w_gate_up = nrm((N_EVEN, GLA_RANK, GLA_K_WIDTH), GLA_RANK ** -0.5)
    gla_b_gate = nrm((N_EVEN, GLA_K_WIDTH), 0.1)
    gla_norm_w = 1.0 + nrm((N_EVEN, GLA_DV), 0.02)
    n_idx = jnp.arange(S5_P, dtype=f32)
    s5_lambda_re = -0.5 + nrm((N_EVEN, S5_GROUPS, S5_P), 0.01)
    s5_lambda_im = math.pi * n_idx + nrm((N_EVEN, S5_GROUPS, S5_P), 0.01)
    s5_log_dt = unif((N_EVEN, S5_GROUPS), math.log(S5_DT_MIN), math.log(S5_DT_MAX))
    s5_b_re = nrm((N_EVEN, S5_GROUPS, S5_P, S5_GROUP_CH), (2 * S5_GROUP_CH) ** -0.5)
    s5_b_im = nrm((N_EVEN, S5_GROUPS, S5_P, S5_GROUP_CH), (2 * S5_GROUP_CH) ** -0.5)
    s5_c_re = nrm((N_EVEN, S5_GROUPS, S5_GROUP_CH, S5_P), S5_P ** -0.5)
    s5_c_im = nrm((N_EVEN, S5_GROUPS, S5_GROUP_CH, S5_P), S5_P ** -0.5)
    s5_d = nrm((N_EVEN, S5_WIDTH), 1.0)
    s5_w_glu = nrm((N_EVEN, S5_WIDTH, S5_WIDTH), S5_WIDTH ** -0.5)
    s5_b_glu = nrm((N_EVEN, S5_WIDTH), 0.01)
    w_out_even = nrm((N_EVEN, GLA_V_WIDTH + S5_WIDTH, D_MODEL), (GLA_V_WIDTH + S5_WIDTH) ** -0.5)
    w_in_odd = nrm((N_ODD, D_MODEL, ODD_IN), D_MODEL ** -0.5)
    dn_conv_w = nrm((N_ODD, CONV_W, DN_CONV_DIM), CONV_W ** -0.5)
    dn_a_log = jnp.log(unif((N_ODD, DN_HEADS), 1.0, 16.0))
    dt = jnp.exp(unif((N_ODD, DN_HEADS), math.log(0.001), math.log(0.1)))
    dn_dt_bias = dt + jnp.log(-jnp.expm1(-dt))
    dn_norm_w = 1.0 + nrm((N_ODD, DN_DV), 0.02)
    w_out_odd = nrm((N_ODD, DN_V_WIDTH, D_MODEL), DN_V_WIDTH ** -0.5)
    return {
        'x_prompt': x_prompt, 'x_sample': x_sample,
        'state_gla': state_gla, 'state_s5_re': state_s5_re, 'state_s5_im': state_s5_im,
        'state_delta': state_delta, 'state_conv': state_conv,
        'norm_w': norm_w, 'final_norm_w': final_norm_w,
        'w_in_even': w_in_even, 'gla_w_gate_up': gla_w_gate_up, 'gla_b_gate': gla_b_gate,
        'gla_norm_w': gla_norm_w, 's5_lambda_re': s5_lambda_re, 's5_lambda_im': s5_lambda_im,
        's5_log_dt': s5_log_dt, 's5_b_re': s5_b_re, 's5_b_im': s5_b_im,
        's5_c_re': s5_c_re, 's5_c_im': s5_c_im, 's5_d': s5_d,
        's5_w_glu': s5_w_glu, 's5_b_glu': s5_b_glu, 'w_out_even': w_out_even,
        'w_in_odd': w_in_odd, 'dn_conv_w': dn_conv_w, 'dn_a_log': dn_a_log,
        'dn_dt_bias': dn_dt_bias, 'dn_norm_w': dn_norm_w, 'w_out_odd': w_out_odd,
    }


def reference(x_prompt, x_sample, state_gla, state_s5_re, state_s5_im, state_delta, state_conv,
              norm_w, final_norm_w, w_in_even, gla_w_gate_up, gla_b_gate, gla_norm_w,
              s5_lambda_re, s5_lambda_im, s5_log_dt, s5_b_re, s5_b_im, s5_c_re, s5_c_im, s5_d,
              s5_w_glu, s5_b_glu, w_out_even, w_in_odd, dn_conv_w, dn_a_log, dn_dt_bias,
              dn_norm_w, w_out_odd):
    f32 = jnp.float32
    nb = x_prompt.shape[0]
    xp, xs = x_prompt, x_sample
    gla_p, gla_s, s5r_p, s5i_p, s5r_s, s5i_s = [], [], [], [], [], []
    dn_p, dn_s, cv_p, cv_s = [], [], [], []
    for layer in range(DEPTH):
        i = layer // 2
        if layer % 2 == 0:
            prm = (w_in_even[i], gla_w_gate_up[i], gla_b_gate[i], gla_norm_w[i], s5_lambda_re[i],
                   s5_lambda_im[i], s5_log_dt[i], s5_b_re[i], s5_b_im[i], s5_c_re[i], s5_c_im[i],
                   s5_d[i], s5_w_glu[i], s5_b_glu[i], w_out_even[i])
            zg = jnp.zeros((nb, GLA_HEADS, GLA_DK, GLA_DV), f32)
            zs = jnp.zeros((nb, S5_GROUPS, S5_P), f32)
            dp, g_p, r_p, m_p = even_mixer(rmsnorm(xp, norm_w[layer]), zg, zs, zs, *prm)
            ds, g_s, r_s, m_s = even_mixer(rmsnorm(xs, norm_w[layer]), state_gla[i],
                                           state_s5_re[i], state_s5_im[i], *prm)
            xp = xp + dp
            xs = xs + ds
            gla_p.append(g_p)
            gla_s.append(g_s)
            s5r_p.append(r_p)
            s5i_p.append(m_p)
            s5r_s.append(r_s)
            s5i_s.append(m_s)
        else:
            prm = (w_in_odd[i], dn_conv_w[i], dn_a_log[i], dn_dt_bias[i], dn_norm_w[i], w_out_odd[i])
            zd = jnp.zeros((nb, DN_HEADS, DN_DK, DN_DV), f32)
            zc = jnp.zeros((nb, CONV_W - 1, DN_CONV_DIM), f32)
            dp, d_p, c_p = odd_mixer(rmsnorm(xp, norm_w[layer]), zd, zc, *prm)
            ds, d_s, c_s = odd_mixer(rmsnorm(xs, norm_w[layer]), state_delta[i], state_conv[i], *prm)
            xp = xp + dp
            xs = xs + ds
            dn_p.append(d_p)
            dn_s.append(d_s)
            cv_p.append(c_p)
            cv_s.append(c_s)
    y_prompt = rmsnorm(xp, final_norm_w)
    y_sample = rmsnorm(xs, final_norm_w)
    return (y_prompt, y_sample, jnp.stack(gla_p), jnp.stack(gla_s), jnp.stack(s5r_p),
            jnp.stack(s5i_p), jnp.stack(s5r_s), jnp.stack(s5i_s), jnp.stack(dn_p),
            jnp.stack(dn_s), jnp.stack(cv_p), jnp.stack(cv_s))
```

```python
import functools
import math

import jax
import jax.numpy as jnp
from jax import lax
from jax.experimental import pallas as pl
from jax.experimental.pallas import tpu as pltpu

F32 = jnp.float32
BF16 = jnp.bfloat16
EPS = 1e-6

D_MODEL = 1024
GLA_HEADS, GLA_DK, GLA_DV, GLA_RANK = 4, 128, 256, 16
GLA_GATE_NORM = 16.0
S5_GROUPS, S5_GROUP_CH, S5_P, S5_WIDTH = 32, 16, 64, 512
S5_STATE = S5_GROUPS * S5_P
DN_HEADS, DN_DK, DN_DV = 8, 128, 128
DN_CONV_DIM, CONV_W = 3072, 4
CHUNK = 64
SUB = 16
LANES = 128
VMEM_LIMIT = 56 * 1024 * 1024


def _mm(a, b):
    return jnp.dot(a.astype(BF16), b.astype(BF16), preferred_element_type=F32)


def _mm_nt(a, b):
    return lax.dot_general(a.astype(BF16), b.astype(BF16), (((1,), (1,)), ((), ())),
                           preferred_element_type=F32)


def _mm_tn(a, b):
    return lax.dot_general(a.astype(BF16), b.astype(BF16), (((0,), (0,)), ((), ())),
                           preferred_element_type=F32)


def _split3(a):
    a1 = a.astype(BF16)
    r1 = a - a1.astype(F32)
    a2 = r1.astype(BF16)
    a3 = (r1 - a2.astype(F32)).astype(BF16)
    return a1, a2, a3


def _mm_exact_rhs(m_bf16, a):
    a1, a2, a3 = _split3(a)
    d = lambda x: jnp.dot(m_bf16, x, preferred_element_type=F32)
    return (d(a1) + d(a2)) + d(a3)


def _mm_tn_exact(a, m_bf16):
    a1, a2, a3 = _split3(a)
    d = lambda x: lax.dot_general(x, m_bf16, (((0,), (0,)), ((), ())), preferred_element_type=F32)
    return (d(a1) + d(a2)) + d(a3)


def _mm_hi(a, b):
    a1 = a.astype(BF16)
    a2 = (a - a1.astype(F32)).astype(BF16)
    b1 = b.astype(BF16)
    b2 = (b - b1.astype(F32)).astype(BF16)
    d = lambda x, y: jnp.dot(x, y, preferred_element_type=F32)
    return (d(a1, b1) + d(a1, b2)) + d(a2, b1)


def _sigmoid(x):
    return 1.0 / (1.0 + jnp.exp(-x))


def _silu(x):
    return x * _sigmoid(x)


def _softplus(x):
    return jnp.maximum(x, 0.0) + jnp.log1p(jnp.exp(-jnp.abs(x)))


def _rms_rows(x, w):
    ms = jnp.mean(x * x, axis=-1, keepdims=True)
    return x * lax.rsqrt(ms + EPS) * w


def _const_spec(shape):
    nd = len(shape)
    return pl.BlockSpec(shape, lambda i, _nd=nd: (0,) * _nd)


def _params():
    return pltpu.CompilerParams(dimension_semantics=("arbitrary",), vmem_limit_bytes=VMEM_LIMIT)


def _row_tile(t):
    return min(256, t)


def _proj_even_kernel(x_ref, nw_ref, wq_ref, wk_ref, wv_ref, wr_ref, wlr_ref, wu_ref, wsg_ref,
                      wg_ref, bg_ref, q_ref, k_ref, g_ref, v_ref, r_ref, u_ref, sg_ref):
    hb = _rms_rows(x_ref[...], nw_ref[...]).astype(BF16)
    d = lambda w_ref: jnp.dot(hb, w_ref[...], preferred_element_type=F32)
    q = d(wq_ref) * (GLA_DK ** -0.5)
    k = d(wk_ref)
    lr = d(wlr_ref)
    logit = jnp.dot(lr.astype(BF16), wg_ref[...], preferred_element_type=F32) + bg_ref[...]
    g = -_softplus(-logit) / GLA_GATE_NORM
    for h in range(GLA_HEADS):
        sl = slice(h * LANES, (h + 1) * LANES)
        q_ref[h] = q[:, sl]
        k_ref[h] = k[:, sl]
        g_ref[h] = g[:, sl]
    v = d(wv_ref)
    for s in range(2 * GLA_HEADS):
        v_ref[s] = v[:, s * LANES:(s + 1) * LANES]
    r_ref[...] = d(wr_ref)
    u_ref[...] = d(wu_ref)
    sg_ref[...] = d(wsg_ref)


def _proj_even(x, nw, w):
    t = x.shape[0]
    tm = _row_tile(t)
    row = lambda c: pl.BlockSpec((tm, c), lambda i: (i, 0))
    slab = lambda n: pl.BlockSpec((n, tm, LANES), lambda i: (0, i, 0))
    out_shape = (
        jax.ShapeDtypeStruct((GLA_HEADS, t, LANES), F32),
        jax.ShapeDtypeStruct((GLA_HEADS, t, LANES), F32),
        jax.ShapeDtypeStruct((GLA_HEADS, t, LANES), F32),
        jax.ShapeDtypeStruct((2 * GLA_HEADS, t, LANES), F32),
        jax.ShapeDtypeStruct((t, 1024), F32),
        jax.ShapeDtypeStruct((t, S5_WIDTH), F32),
        jax.ShapeDtypeStruct((t, S5_WIDTH), F32),
    )
    weights = (w["wq"], w["wk"], w["wv"], w["wr"], w["wlr"], w["wu"], w["wsg"], w["wg"], w["bg"])
    return pl.pallas_call(
        _proj_even_kernel,
        grid=(t // tm,),
        in_specs=[row(D_MODEL), _const_spec(nw.shape)] + [_const_spec(a.shape) for a in weights],
        out_specs=(slab(4), slab(4), slab(4), slab(8), row(1024), row(S5_WIDTH), row(S5_WIDTH)),
        out_shape=out_shape,
        compiler_params=_params(),
        name="proj_even",
    )(x, nw, *weights)


def _proj_odd_kernel(x_ref, nw_ref, wqkv_ref, wz_ref, wb_ref, wa_ref, alog_ref, dtb_ref,
                     xqkv_ref, z_ref, beta_ref, gd_ref):
    hb = _rms_rows(x_ref[...], nw_ref[...]).astype(BF16)
    d = lambda w_ref: jnp.dot(hb, w_ref[...], preferred_element_type=F32)
    xqkv_ref[...] = d(wqkv_ref)
    z_ref[...] = d(wz_ref)
    beta_ref[...] = _sigmoid(d(wb_ref))
    gd_ref[...] = -jnp.exp(alog_ref[...]) * _softplus(d(wa_ref) + dtb_ref[...])


def _proj_odd(x, nw, w):
    t = x.shape[0]
    tm = _row_tile(t)
    row = lambda c: pl.BlockSpec((tm, c), lambda i: (i, 0))
    weights = (w["wqkv"], w["wz"], w["wb"], w["wa"], w["alog"], w["dtb"])
    return pl.pallas_call(
        _proj_odd_kernel,
        grid=(t // tm,),
        in_specs=[row(D_MODEL), _const_spec(nw.shape)] + [_const_spec(a.shape) for a in weights],
        out_specs=(row(DN_CONV_DIM), row(1024), row(LANES), row(LANES)),
        out_shape=(
            jax.ShapeDtypeStruct((t, DN_CONV_DIM), F32),
            jax.ShapeDtypeStruct((t, 1024), F32),
            jax.ShapeDtypeStruct((t, LANES), F32),
            jax.ShapeDtypeStruct((t, LANES), F32),
        ),
        compiler_params=_params(),
        name="proj_odd",
    )(x, nw, *weights)


def _out_even_kernel(x_ref, og_ref, y5_ref, wa_ref, wb_ref, o_ref):
    o_ref[...] = x_ref[...] + (_mm(og_ref[...], wa_ref[...]) + _mm(y5_ref[...], wb_ref[...]))


def _out_even(x, og, y5, wa, wb):
    t = x.shape[0]
    tm = _row_tile(t)
    row = lambda c: pl.BlockSpec((tm, c), lambda i: (i, 0))
    return pl.pallas_call(
        _out_even_kernel,
        grid=(t // tm,),
        in_specs=[row(D_MODEL), row(1024), row(S5_WIDTH), _const_spec(wa.shape), _const_spec(wb.shape)],
        out_specs=row(D_MODEL),
        out_shape=jax.ShapeDtypeStruct((t, D_MODEL), F32),
        compiler_params=_params(),
        name="out_even",
    )(x, og, y5, wa, wb)


def _out_odd_kernel(x_ref, og_ref, w_ref, fw_ref, o_ref, *, final):
    y = x_ref[...] + _mm(og_ref[...], w_ref[...])
    if final:
        y = _rms_rows(y, fw_ref[...])
    o_ref[...] = y


def _out_odd(x, og, w, fw, final):
    t = x.shape[0]
    tm = _row_tile(t)
    row = lambda c: pl.BlockSpec((tm, c), lambda i: (i, 0))
    return pl.pallas_call(
        functools.partial(_out_odd_kernel, final=final),
        grid=(t // tm,),
        in_specs=[row(D_MODEL), row(1024), _const_spec(w.shape), _const_spec(fw.shape)],
        out_specs=row(D_MODEL),
        out_shape=jax.ShapeDtypeStruct((t, D_MODEL), F32),
        compiler_params=_params(),
        name="out_odd",
    )(x, og, w, fw)


def _gla_prompt_kernel(q_ref, k_ref, g_ref, v_ref, r_ref, tri_ref, nw_ref, og_ref, sout_ref,
                       kp, bp, vp, st, oscr, *, nb):
    rows = CHUNK * nb
    pad = SUB * nb
    step = pl.program_id(0)

    @pl.when(step == 0)
    def _():
        st[...] = jnp.zeros_like(st)
        kp[:, 0:pad, :] = jnp.zeros((GLA_HEADS, pad, LANES), F32)
        bp[:, 0:pad, :] = jnp.zeros((GLA_HEADS, pad, LANES), F32)
        vp[:, 0:pad, :] = jnp.zeros((2 * GLA_HEADS, pad, LANES), F32)

    tri = tri_ref[...]
    for h in range(GLA_HEADS):
        bp[h, pad:pad + rows, :] = _mm_exact_rhs(tri, g_ref[h])
        kp[h, pad:pad + rows, :] = k_ref[h]
    for s in range(2 * GLA_HEADS):
        vp[s, pad:pad + rows, :] = v_ref[s]

    def band_tile(ti, carry):
        r0 = pl.multiple_of(ti * 64, 64)
        for h in range(GLA_HEADS):
            qt = q_ref[h, pl.ds(r0, 64), :]
            bt = bp[h, pl.ds(pad + r0, 64), :]
            acc0 = jnp.zeros((64, LANES), F32)
            acc1 = jnp.zeros((64, LANES), F32)
            for d in range(SUB):
                off = pl.multiple_of(pad + r0 - d * nb, 8)
                ks = kp[h, pl.ds(off, 64), :]
                bs = bp[h, pl.ds(off, 64), :]
                w = jnp.sum(qt * ks * jnp.exp(bt - bs), axis=-1, keepdims=True)
                acc0 = acc0 + w * vp[2 * h, pl.ds(off, 64), :]
                acc1 = acc1 + w * vp[2 * h + 1, pl.ds(off, 64), :]
            oscr[2 * h, pl.ds(r0, 64), :] = acc0
            oscr[2 * h + 1, pl.ds(r0, 64), :] = acc1
        return carry

    lax.fori_loop(0, rows // 64, band_tile, 0)

    ii = lax.broadcasted_iota(jnp.int32, (CHUNK, CHUNK), 0)
    jj = lax.broadcasted_iota(jnp.int32, (CHUNK, CHUNK), 1)

    def per_batch(b, carry):
        for h in range(GLA_HEADS):
            sel = pl.ds(b, CHUNK, stride=nb)
            selp = pl.ds(pad + b, CHUNK, stride=nb)
            qb = q_ref[h, sel, :]
            kb = kp[h, selp, :]
            bb = bp[h, selp, :]
            vb = jnp.concatenate([vp[2 * h, selp, :], vp[2 * h + 1, selp, :]], axis=1)
            stt = st[b, h]
            o = _mm_nt(qb * jnp.exp(bb), stt)
            amat = jnp.zeros((CHUNK, CHUNK), F32)
            for blk in range(1, CHUNK // SUB):
                ref_b = bb[SUB * blk - 1:SUB * blk, :]
                qs = qb * jnp.exp(jnp.minimum(bb - ref_b, 0.0))
                ks = kb * jnp.exp(jnp.minimum(ref_b - bb, 0.0))
                p = _mm_nt(qs, ks)
                keep = (ii >= SUB * blk) & (ii < SUB * (blk + 1)) & (jj <= ii - SUB)
                amat = amat + jnp.where(keep, p, 0.0)
            o = o + _mm(amat, vb)
            oscr[2 * h, sel, :] = oscr[2 * h, sel, :] + o[:, :LANES]
            oscr[2 * h + 1, sel, :] = oscr[2 * h + 1, sel, :] + o[:, LANES:]
            blast = bb[CHUNK - 1:CHUNK, :]
            kd = kb * jnp.exp(blast - bb)
            st[b, h] = stt * jnp.exp(blast) + _mm_tn(vb, kd)
        return carry

    lax.fori_loop(0, nb, per_batch, 0)

    nw = nw_ref[...]

    def epi_tile(ti, carry):
        r0 = pl.multiple_of(ti * 64, 64)
        for h in range(GLA_HEADS):
            o0 = oscr[2 * h, pl.ds(r0, 64), :]
            o1 = oscr[2 * h + 1, pl.ds(r0, 64), :]
            ms = (jnp.sum(o0 * o0, axis=-1, keepdims=True) + jnp.sum(o1 * o1, axis=-1, keepdims=True)) / GLA_DV
            inv = lax.rsqrt(ms + EPS)
            c0 = h * GLA_DV
            og_ref[pl.ds(r0, 64), c0:c0 + LANES] = o0 * inv * nw[:, :LANES] * _silu(r_ref[pl.ds(r0, 64), c0:c0 + LANES])
            og_ref[pl.ds(r0, 64), c0 + LANES:c0 + 2 * LANES] = (
                o1 * inv * nw[:, LANES:] * _silu(r_ref[pl.ds(r0, 64), c0 + LANES:c0 + 2 * LANES]))
        return carry

    lax.fori_loop(0, rows // 64, epi_tile, 0)

    @pl.when(step == pl.num_programs(0) - 1)
    def _():
        def wr(b, carry):
            for h in range(GLA_HEADS):
                sout_ref[b, h] = st[b, h].T
            return carry
        lax.fori_loop(0, nb, wr, 0)


def _tri_rows(nb):
    n = CHUNK * nb
    r = jnp.arange(n)
    same = (r[:, None] % nb) == (r[None, :] % nb)
    lower = (r[None, :] // nb) <= (r[:, None] // nb)
    return (same & lower).astype(BF16)


def _gla_prompt(q, k, g, v, r, nw, nb):
    t = r.shape[0]
    rows = CHUNK * nb
    pad = SUB * nb
    slab = lambda n: pl.BlockSpec((n, rows, LANES), lambda i: (0, i, 0))
    row = lambda c: pl.BlockSpec((rows, c), lambda i: (i, 0))
    tri = _tri_rows(nb)
    return pl.pallas_call(
        functools.partial(_gla_prompt_kernel, nb=nb),
        grid=(t // rows,),
        in_specs=[slab(4), slab(4), slab(4), slab(8), row(1024), _const_spec(tri.shape), _const_spec(nw.shape)],
        out_specs=(row(1024), _const_spec((nb, GLA_HEADS, GLA_DK, GLA_DV))),
        out_shape=(jax.ShapeDtypeStruct((t, 1024), F32),
                   jax.ShapeDtypeStruct((nb, GLA_HEADS, GLA_DK, GLA_DV), F32)),
        scratch_shapes=[
            pltpu.VMEM((GLA_HEADS, pad + rows, LANES), F32),
            pltpu.VMEM((GLA_HEADS, pad + rows, LANES), F32),
            pltpu.VMEM((2 * GLA_HEADS, pad + rows, LANES), F32),
            pltpu.VMEM((nb, GLA_HEADS, GLA_DV, GLA_DK), F32),
            pltpu.VMEM((2 * GLA_HEADS, rows, LANES), F32),
        ],
        compiler_params=_params(),
        name="gla_prompt",
    )(q, k, g, v, r, tri, nw)


def _gla_step_kernel(q_ref, k_ref, g_ref, v_ref, r_ref, s_ref, nw_ref, og_ref, sout_ref):
    nw = nw_ref[...]
    zeros = jnp.zeros((LANES - 24, LANES), F32)
    for h in range(GLA_HEADS):
        stack = jnp.concatenate([jnp.exp(g_ref[h]), k_ref[h], q_ref[h], zeros], axis=0)
        cols = stack.T
        outs = []
        for n in range(8):
            s_old = s_ref[n, h]
            vrow = jnp.concatenate([v_ref[2 * h, n:n + 1, :], v_ref[2 * h + 1, n:n + 1, :]], axis=1)
            s_new = s_old * cols[:, n:n + 1] + cols[:, 8 + n:9 + n] * vrow
            sout_ref[n, h] = s_new
            outs.append(jnp.sum(cols[:, 16 + n:17 + n] * s_new, axis=0, keepdims=True))
        o = jnp.concatenate(outs, axis=0)
        ms = jnp.mean(o * o, axis=-1, keepdims=True)
        c0 = h * GLA_DV
        og_ref[:, c0:c0 + GLA_DV] = o * lax.rsqrt(ms + EPS) * nw * _silu(r_ref[:, c0:c0 + GLA_DV])


def _gla_step(q, k, g, v, r, s, nw):
    n = r.shape[0]
    slab = lambda c: pl.BlockSpec((c, 8, LANES), lambda i: (0, i, 0))
    sspec = pl.BlockSpec((8, GLA_HEADS, GLA_DK, GLA_DV), lambda i: (i, 0, 0, 0))
    return pl.pallas_call(
        _gla_step_kernel,
        grid=(n // 8,),
        in_specs=[slab(4), slab(4), slab(4), slab(8), pl.BlockSpec((8, 1024), lambda i: (i, 0)), sspec,
                  _const_spec(nw.shape)],
        out_specs=(pl.BlockSpec((8, 1024), lambda i: (i, 0)), sspec),
        out_shape=(jax.ShapeDtypeStruct((n, 1024), F32), jax.ShapeDtypeStruct(s.shape, F32)),
        compiler_params=_params(),
        name="gla_step",
    )(q, k, g, v, r, s, nw)


def _s5_prep_kernel(lre_ref, lim_ref, ldt_ref, lre_r_ref, lim_r_ref, ldt_r_ref, bre_ref, bim_ref,
                    ar_ref, ai_ref, bbre_ref, bbim_ref):
    def disc(lre, lim, ldt):
        dt = jnp.exp(ldt)
        mag = jnp.exp(lre * dt)
        ar = mag * jnp.cos(lim * dt)
        ai = mag * jnp.sin(lim * dt)
        return ar, ai

    ar, ai = disc(lre_ref[...], lim_ref[...], ldt_ref[...])
    ar_ref[...] = ar
    ai_ref[...] = ai
    lre, lim = lre_r_ref[...], lim_r_ref[...]
    ar, ai = disc(lre, lim, ldt_r_ref[...])
    den = lre * lre + lim * lim
    wr = ((ar - 1.0) * lre + ai * lim) / den
    wi = (ai * lre - (ar - 1.0) * lim) / den
    bre, bim = bre_ref[...], bim_ref[...]
    bbre_ref[...] = wr * bre - wi * bim
    bbim_ref[...] = wr * bim + wi * bre


def _s5_prep(lam_re, lam_im, log_dt, b_re, b_im):
    n = lam_re.shape[0]
    rows = n * S5_GROUPS
    lre = lam_re.reshape(rows, S5_P)
    lim = lam_im.reshape(rows, S5_P)
    ldt = jnp.broadcast_to(log_dt.reshape(rows, 1), (rows, S5_P))
    rep = lambda a: jnp.repeat(a, S5_GROUP_CH, axis=1)
    args = (lre, lim, ldt, rep(lre), rep(lim), rep(ldt),
            b_re.reshape(rows, S5_P * S5_GROUP_CH), b_im.reshape(rows, S5_P * S5_GROUP_CH))
    wide = jax.ShapeDtypeStruct((rows, S5_P * S5_GROUP_CH), F32)
    narrow = jax.ShapeDtypeStruct((rows, S5_P), F32)
    ar, ai, bbre, bbim = pl.pallas_call(
        _s5_prep_kernel, out_shape=(narrow, narrow, wide, wide), name="s5_prep")(*args)
    shp = (n, S5_GROUPS, S5_P, S5_GROUP_CH)
    return ar.reshape(n, S5_GROUPS, S5_P), ai.reshape(n, S5_GROUPS, S5_P), bbre.reshape(shp), bbim.reshape(shp)


def _s5_kernel(u_ref, sg_ref, h0re_ref, h0im_ref, wbre_ref, wbim_ref, wc_ref, are_ref, aim_ref, d_ref,
               wglu_ref, bglu_ref, y_ref, hre_out, him_out, hre, him, cre, cim, ys, *, nb, tokens):
    step = pl.program_id(0)
    rows = nb * tokens

    @pl.when(step == 0)
    def _():
        cre[...] = h0re_ref[...]
        cim[...] = h0im_ref[...]

    for m in range(4):
        um = u_ref[:, m * LANES:(m + 1) * LANES].astype(BF16)
        hre[:, m * 512:(m + 1) * 512] = jnp.dot(um, wbre_ref[m], preferred_element_type=F32)
        him[:, m * 512:(m + 1) * 512] = jnp.dot(um, wbim_ref[m], preferred_element_type=F32)

    for qd in range(4):
        ql = slice(qd * 512, (qd + 1) * 512)
        ar = jnp.broadcast_to(are_ref[:, ql], (nb, 512))
        ai = jnp.broadcast_to(aim_ref[:, ql], (nb, 512))

        def tok(t, carry, ql=ql, ar=ar, ai=ai):
            hr, hi = carry
            sel = pl.ds(pl.multiple_of(t * nb, nb), nb)
            nr = ar * hr - ai * hi + hre[sel, ql]
            ni = ar * hi + ai * hr + him[sel, ql]
            hre[sel, ql] = nr
            him[sel, ql] = ni
            return nr, ni

        hr, hi = lax.fori_loop(0, tokens, tok, (cre[:, ql], cim[:, ql]))
        cre[:, ql] = hr
        cim[:, ql] = hi

    for m in range(4):
        hc = jnp.concatenate([hre[:, m * 512:(m + 1) * 512], him[:, m * 512:(m + 1) * 512]], axis=1)
        sl = slice(m * LANES, (m + 1) * LANES)
        y = _mm(hc, wc_ref[m]) + d_ref[:, sl] * u_ref[:, sl]
        ys[:, sl] = jax.nn.gelu(y)
    y = ys[...]
    gate = _sigmoid(_mm(y, wglu_ref[...]) + bglu_ref[...])
    y_ref[...] = y * gate * _silu(sg_ref[...])

    @pl.when(step == pl.num_programs(0) - 1)
    def _():
        hre_out[...] = cre[...]
        him_out[...] = cim[...]


def _s5(u, sg, h0re, h0im, w, nb, tokens):
    t = u.shape[0]
    rows = nb * tokens
    row = lambda c: pl.BlockSpec((rows, c), lambda i: (i, 0))
    weights = (w["wbre"], w["wbim"], w["wc"], w["are"], w["aim"], w["d"], w["wglu"], w["bglu"])
    state = jax.ShapeDtypeStruct((nb, S5_STATE), F32)
    return pl.pallas_call(
        functools.partial(_s5_kernel, nb=nb, tokens=tokens),
        grid=(t // rows,),
        in_specs=[row(S5_WIDTH), row(S5_WIDTH), _const_spec(h0re.shape), _const_spec(h0im.shape)]
        + [_const_spec(a.shape) for a in weights],
        out_specs=(row(S5_WIDTH), _const_spec((nb, S5_STATE)), _const_spec((nb, S5_STATE))),
        out_shape=(jax.ShapeDtypeStruct((t, S5_WIDTH), F32), state, state),
        scratch_shapes=[
            pltpu.VMEM((rows, S5_STATE), F32),
            pltpu.VMEM((rows, S5_STATE), F32),
            pltpu.VMEM((nb, S5_STATE), F32),
            pltpu.VMEM((nb, S5_STATE), F32),
            pltpu.VMEM((rows, S5_WIDTH), F32),
        ],
        compiler_params=_params(),
        name="s5",
    )(u, sg, h0re, h0im, *weights)


def _dn_solve(lm, rhs):
    xs = []
    for blk in range(CHUNK // SUB):
        lo, hi = SUB * blk, SUB * (blk + 1)
        cur = rhs[lo:hi]
        if blk:
            cur = cur - _mm_hi(lm[lo:hi, 0:lo], jnp.concatenate(xs, axis=0))
        ldiag = lm[lo:hi, lo:hi]
        for j in range(SUB - 1):
            cur = cur - ldiag[:, j:j + 1] * cur[j:j + 1, :]
        xs.append(cur)
    return jnp.concatenate(xs, axis=0)


def _dn_prompt_kernel(x_ref, z_ref, beta_ref, gd_ref, cw_ref, tri_ref, nw_ref, c0_ref,
                      og_ref, sout_ref, cout_ref,
                      xs, qs, ks, vs, bcol, gcol, gc, gt, oscr, *, nb):
    rows = CHUNK * nb
    hist = (CONV_W - 1) * nb
    step = pl.program_id(0)
    st = sout_ref

    @pl.when(step == 0)
    def _():
        st[...] = jnp.zeros_like(st)
        xs[0:hist, :] = c0_ref[...]

    xs[hist:hist + rows, :] = x_ref[...]

    def conv_tile(ti, carry):
        r0 = pl.multiple_of(ti * 64, 64)
        for c in range(DN_CONV_DIM // LANES):
            cl = slice(c * LANES, (c + 1) * LANES)
            y = xs[pl.ds(r0, 64), cl] * cw_ref[0:1, cl]
            for i in range(1, CONV_W):
                y = y + xs[pl.ds(r0 + i * nb, 64), cl] * cw_ref[i:i + 1, cl]
            y = _silu(y)
            if c < 2 * DN_HEADS:
                y = y * lax.rsqrt(jnp.sum(y * y, axis=-1, keepdims=True) + EPS)
            if c < DN_HEADS:
                qs[c, pl.ds(r0, 64), :] = y * (DN_DK ** -0.5)
            elif c < 2 * DN_HEADS:
                ks[c - DN_HEADS, pl.ds(r0, 64), :] = y
            else:
                vs[c - 2 * DN_HEADS, pl.ds(r0, 64), :] = y
        return carry

    lax.fori_loop(0, rows // 64, conv_tile, 0)

    xs[0:hist, :] = xs[rows:rows + hist, :]

    gc[...] = _mm_exact_rhs(tri_ref[...], gd_ref[...])

    def bc_tile(ti, carry):
        r0 = pl.multiple_of(ti * 64, 64)
        gtile = gc[pl.ds(r0, 64), :]
        btile = beta_ref[pl.ds(r0, 64), :]
        for h in range(DN_HEADS):
            gcol[h, pl.ds(r0, 64), :] = jnp.broadcast_to(gtile[:, h:h + 1], (64, LANES))
            bcol[h, pl.ds(r0, 64), :] = jnp.broadcast_to(btile[:, h:h + 1], (64, LANES))
        return carry

    lax.fori_loop(0, rows // 64, bc_tile, 0)

    ii = lax.broadcasted_iota(jnp.int32, (CHUNK, CHUNK), 0)
    jj = lax.broadcasted_iota(jnp.int32, (CHUNK, CHUNK), 1)
    eye = (ii == jj).astype(BF16)

    def per_batch(b, carry):
        sel = pl.ds(b, CHUNK, stride=nb)
        gt[...] = _mm_tn_exact(gc[sel, :], eye)

        def per_head(h, carry2):
            qb = qs[h, sel, :]
            kb = ks[h, sel, :]
            vb = vs[h, sel, :]
            bc = bcol[h, sel, :]
            gcl = gcol[h, sel, :]
            grow = gt[pl.ds(h, 1), :]
            diff = gcl[:, :CHUNK] - grow
            dec = jnp.where(jj <= ii, jnp.exp(jnp.where(jj <= ii, diff, 0.0)), 0.0)
            kk = _mm_nt(kb, kb)
            qk = _mm_nt(qb, kb)
            lm = jnp.where(jj < ii, bc[:, :CHUNK] * kk * dec, 0.0)
            eg = jnp.exp(gcl)
            rhs = jnp.concatenate([vb * bc, kb * (bc * eg)], axis=1)
            sol = _dn_solve(lm, rhs)
            s_old = st[b, h]
            v_new = sol[:, :DN_DV] - _mm(sol[:, DN_DV:], s_old)
            o = _mm(qb * eg, s_old) + _mm(qk * dec, v_new)
            glast = gcl[CHUNK - 1:CHUNK, :]
            st[b, h] = s_old * jnp.exp(glast) + _mm_tn(kb * jnp.exp(glast - gcl), v_new)
            oscr[h, sel, :] = o
            return carry2

        lax.fori_loop(0, DN_HEADS, per_head, 0)
        return carry

    lax.fori_loop(0, nb, per_batch, 0)

    nw = nw_ref[...]

    def epi_tile(ti, carry):
        r0 = pl.multiple_of(ti * 64, 64)
        for h in range(DN_HEADS):
            o = oscr[h, pl.ds(r0, 64), :]
            inv = lax.rsqrt(jnp.mean(o * o, axis=-1, keepdims=True) + EPS)
            cl = slice(h * LANES, (h + 1) * LANES)
            og_ref[pl.ds(r0, 64), cl] = o * inv * nw * _silu(z_ref[pl.ds(r0, 64), cl])
        return carry

    lax.fori_loop(0, rows // 64, epi_tile, 0)

    @pl.when(step == pl.num_programs(0) - 1)
    def _():
        cout_ref[...] = xs[0:hist, :]


def _dn_prompt(xqkv, z, beta, gd, cw, nw, c0, nb):
    t = z.shape[0]
    rows = CHUNK * nb
    hist = (CONV_W - 1) * nb
    row = lambda c: pl.BlockSpec((rows, c), lambda i: (i, 0))
    tri = _tri_rows(nb)
    slab = pltpu.VMEM((DN_HEADS, rows, LANES), F32)
    return pl.pallas_call(
        functools.partial(_dn_prompt_kernel, nb=nb),
        grid=(t // rows,),
        in_specs=[row(DN_CONV_DIM), row(1024), row(LANES), row(LANES), _const_spec(cw.shape),
                  _const_spec(tri.shape), _const_spec(nw.shape), _const_spec(c0.shape)],
        out_specs=(row(1024), _const_spec((nb, DN_HEADS, DN_DK, DN_DV)), _const_spec((hist, DN_CONV_DIM))),
        out_shape=(jax.ShapeDtypeStruct((t, 1024), F32),
                   jax.ShapeDtypeStruct((nb, DN_HEADS, DN_DK, DN_DV), F32),
                   jax.ShapeDtypeStruct((hist, DN_CONV_DIM), F32)),
        scratch_shapes=[
            pltpu.VMEM((hist + rows, DN_CONV_DIM), F32),
            slab, slab, slab, slab, slab,
            pltpu.VMEM((rows, LANES), F32),
            pltpu.VMEM((LANES, CHUNK), F32),
            slab,
        ],
        compiler_params=_params(),
        name="dn_prompt",
    )(xqkv, z, beta, gd, cw, tri, nw, c0)


def _dn_step_kernel(x_ref, cb_ref, z_ref, beta_ref, gd_ref, s_ref, cw_ref, nw_ref, og_ref, sout_ref, cn_ref):
    nw = nw_ref[...]
    cn_ref[0] = cb_ref[1]
    cn_ref[1] = cb_ref[2]
    cn_ref[2] = x_ref[...]
    zeros = jnp.zeros((LANES - 16, LANES), F32)
    beta = beta_ref[...]
    eg_all = jnp.exp(gd_ref[...])

    def conv(c):
        cl = slice(c * LANES, (c + 1) * LANES)
        y = cb_ref[0, :, cl] * cw_ref[0:1, cl]
        y = y + cb_ref[1, :, cl] * cw_ref[1:2, cl]
        y = y + cb_ref[2, :, cl] * cw_ref[2:3, cl]
        y = y + x_ref[:, cl] * cw_ref[3:4, cl]
        return _silu(y)

    def l2(y):
        return y * lax.rsqrt(jnp.sum(y * y, axis=-1, keepdims=True) + EPS)

    for h in range(DN_HEADS):
        qh = l2(conv(h)) * (DN_DK ** -0.5)
        kh = l2(conv(DN_HEADS + h))
        vh = conv(2 * DN_HEADS + h)
        cols = jnp.concatenate([kh, qh, zeros], axis=0).T
        outs = []
        for n in range(8):
            s_old = s_ref[n, h]
            kcol = cols[:, n:n + 1]
            eg = eg_all[n:n + 1, h:h + 1]
            ks_row = jnp.sum(kcol * s_old, axis=0, keepdims=True)
            v_new = beta[n:n + 1, h:h + 1] * (vh[n:n + 1, :] - eg * ks_row)
            s_new = s_old * eg + kcol * v_new
            sout_ref[n, h] = s_new
            outs.append(jnp.sum(cols[:, 8 + n:9 + n] * s_new, axis=0, keepdims=True))
        o = jnp.concatenate(outs, axis=0)
        cl = slice(h * LANES, (h + 1) * LANES)
        og_ref[:, cl] = o * lax.rsqrt(jnp.mean(o * o, axis=-1, keepdims=True) + EPS) * nw * _silu(z_ref[:, cl])


def _dn_step(xqkv, cbuf, z, beta, gd, s, cw, nw):
    n = z.shape[0]
    row = lambda c: pl.BlockSpec((8, c), lambda i: (i, 0))
    cspec = pl.BlockSpec((CONV_W - 1, 8, DN_CONV_DIM), lambda i: (0, i, 0))
    sspec = pl.BlockSpec((8, DN_HEADS, DN_DK, DN_DV), lambda i: (i, 0, 0, 0))
    return pl.pallas_call(
        _dn_step_kernel,
        grid=(n // 8,),
        in_specs=[row(DN_CONV_DIM), cspec, row(1024), row(LANES), row(LANES), sspec,
                  _const_spec(cw.shape), _const_spec(nw.shape)],
        out_specs=(row(1024), sspec, cspec),
        out_shape=(jax.ShapeDtypeStruct((n, 1024), F32), jax.ShapeDtypeStruct(s.shape, F32),
                   jax.ShapeDtypeStruct(cbuf.shape, F32)),
        compiler_params=_params(),
        name="dn_step",
    )(xqkv, cbuf, z, beta, gd, s, cw, nw)


def _pad_cols(a, n):
    return jnp.pad(a, ((0, 0), (0, n - a.shape[1])))


def _even_weights(w_in, w_gate_up, b_gate, w_out):
    o = [0, 512, 1024, 2048, 3072, 3088, 3600, 4112]
    cut = lambda i: w_in[:, o[i]:o[i + 1]]
    return {
        "wq": cut(0).astype(BF16), "wk": cut(1).astype(BF16), "wv": cut(2).astype(BF16),
        "wr": cut(3).astype(BF16), "wlr": _pad_cols(cut(4), LANES).astype(BF16),
        "wu": cut(5).astype(BF16), "wsg": cut(6).astype(BF16),
        "wg": jnp.pad(w_gate_up, ((0, LANES - GLA_RANK), (0, 0))).astype(BF16),
        "bg": b_gate.reshape(1, -1),
        "wo_a": w_out[:1024].astype(BF16), "wo_b": w_out[1024:].astype(BF16),
    }


def _s5_weights(ar, ai, bbre, bbim, c_re, c_im, d, w_glu, b_glu):
    eye = jnp.eye(8, dtype=F32)

    def pack_b(bb):
        t = bb.reshape(4, 8, S5_P, S5_GROUP_CH).transpose(0, 1, 3, 2)
        return (t[:, :, :, None, :] * eye[None, :, None, :, None]).reshape(4, 128, 512).astype(BF16)

    def pack_c(c):
        t = c.reshape(4, 8, S5_GROUP_CH, S5_P).transpose(0, 1, 3, 2)
        return (t[:, :, :, None, :] * eye[None, :, None, :, None]).reshape(4, 512, 128)

    return {
        "wbre": pack_b(bbre), "wbim": pack_b(bbim),
        "wc": jnp.concatenate([pack_c(c_re), -pack_c(c_im)], axis=1).astype(BF16),
        "are": ar.reshape(1, S5_STATE), "aim": ai.reshape(1, S5_STATE),
        "d": d.reshape(1, -1), "wglu": w_glu.astype(BF16), "bglu": b_glu.reshape(1, -1),
    }


def _odd_weights(w_in, conv_w, a_log, dt_bias, w_out):
    return {
        "wqkv": w_in[:, :3072].astype(BF16), "wz": w_in[:, 3072:4096].astype(BF16),
        "wb": _pad_cols(w_in[:, 4096:4104], LANES).astype(BF16),
        "wa": _pad_cols(w_in[:, 4104:4112], LANES).astype(BF16),
        "alog": _pad_cols(a_log.reshape(1, -1), LANES), "dtb": _pad_cols(dt_bias.reshape(1, -1), LANES),
        "cw": jnp.pad(conv_w, ((0, 8 - CONV_W), (0, 0))),
        "wo": w_out.astype(BF16),
    }


def kernel(x_prompt, x_sample, state_gla, state_s5_re, state_s5_im, state_delta, state_conv, norm_w, final_norm_w, w_in_even, gla_w_gate_up, gla_b_gate, gla_norm_w, s5_lambda_re, s5_lambda_im, s5_log_dt, s5_b_re, s5_b_im, s5_c_re, s5_c_im, s5_d, s5_w_glu, s5_b_glu, w_out_even, w_in_odd, dn_conv_w, dn_a_log, dn_dt_bias, dn_norm_w, w_out_odd):
    nb, seq, _ = x_prompt.shape
    ns = x_sample.shape[0]
    assert seq % CHUNK == 0 and nb % 8 == 0 and ns % 8 == 0 and x_sample.shape[1] == 1
    depth = norm_w.shape[0]

    xp = x_prompt.transpose(1, 0, 2).reshape(seq * nb, D_MODEL)
    xs = x_sample.reshape(ns, D_MODEL)
    fw = final_norm_w.reshape(1, -1)

    ar, ai, bbre, bbim = _s5_prep(s5_lambda_re, s5_lambda_im, s5_log_dt, s5_b_re, s5_b_im)

    gla_p, gla_s, s5r_p, s5i_p, s5r_s, s5i_s, dn_p, dn_s, cv_p, cv_s = ([] for _ in range(10))
    for layer in range(depth):
        i = layer // 2
        nw = norm_w[layer].reshape(1, -1)
        if layer % 2 == 0:
            w = _even_weights(w_in_even[i], gla_w_gate_up[i], gla_b_gate[i], w_out_even[i])
            w5 = _s5_weights(ar[i], ai[i], bbre[i], bbim[i], s5_c_re[i], s5_c_im[i], s5_d[i],
                             s5_w_glu[i], s5_b_glu[i])
            gnw = gla_norm_w[i].reshape(1, -1)
            q, k, g, v, r, u, sg = _proj_even(xp, nw, w)
            og, s_new = _gla_prompt(q, k, g, v, r, gnw, nb)
            zero = jnp.zeros((nb, S5_STATE), F32)
            y5, hre, him = _s5(u, sg, zero, zero, w5, nb, CHUNK)
            xp = _out_even(xp, og, y5, w["wo_a"], w["wo_b"])
            gla_p.append(s_new)
            s5r_p.append(hre.reshape(nb, S5_GROUPS, S5_P))
            s5i_p.append(him.reshape(nb, S5_GROUPS, S5_P))
            q, k, g, v, r, u, sg = _proj_even(xs, nw, w)
            og, s_new = _gla_step(q, k, g, v, r, state_gla[i], gnw)
            y5, hre, him = _s5(u, sg, state_s5_re[i].reshape(ns, S5_STATE), state_s5_im[i].reshape(ns, S5_STATE),
                               w5, ns, 1)
            xs = _out_even(xs, og, y5, w["wo_a"], w["wo_b"])
            gla_s.append(s_new)
            s5r_s.append(hre.reshape(ns, S5_GROUPS, S5_P))
            s5i_s.append(him.reshape(ns, S5_GROUPS, S5_P))
        else:
            w = _odd_weights(w_in_odd[i], dn_conv_w[i], dn_a_log[i], dn_dt_bias[i], w_out_odd[i])
            dnw = dn_norm_w[i].reshape(1, -1)
            final = layer == depth - 1
            xqkv, z, beta, gd = _proj_odd(xp, nw, w)
            c0 = jnp.zeros(((CONV_W - 1) * nb, DN_CONV_DIM), F32)
            og, s_new, c_new = _dn_prompt(xqkv, z, beta, gd, w["cw"], dnw, c0, nb)
            xp = _out_odd(xp, og, w["wo"], fw, final)
            dn_p.append(s_new)
            cv_p.append(c_new.reshape(CONV_W - 1, nb, DN_CONV_DIM).transpose(1, 0, 2))
            xqkv, z, beta, gd = _proj_odd(xs, nw, w)
            og, s_new, c_new = _dn_step(xqkv, state_conv[i].transpose(1, 0, 2), z, beta, gd, state_delta[i],
                                        w["cw"], dnw)
            xs = _out_odd(xs, og, w["wo"], fw, final)
            dn_s.append(s_new)
            cv_s.append(c_new.transpose(1, 0, 2))

    y_prompt = xp.reshape(seq, nb, D_MODEL).transpose(1, 0, 2)
    y_sample = xs.reshape(ns, 1, D_MODEL)
    st = jnp.stack
    return (y_prompt, y_sample, st(gla_p), st(gla_s), st(s5r_p), st(s5i_p), st(s5r_s), st(s5i_s),
            st(dn_p), st(dn_s), st(cv_p), st(cv_s))
```

```python
import functools
import math

import jax
import jax.numpy as jnp
from jax import lax
from jax.experimental import pallas as pl
from jax.experimental.pallas import tpu as pltpu

F32 = jnp.float32
BF16 = jnp.bfloat16
EPS = 1e-6

D_MODEL = 1024
GLA_HEADS, GLA_DK, GLA_DV, GLA_RANK = 4, 128, 256, 16
GLA_GATE_NORM = 16.0
S5_GROUPS, S5_GROUP_CH, S5_P, S5_WIDTH = 32, 16, 64, 512
S5_STATE = S5_GROUPS * S5_P
DN_HEADS, DN_DK, DN_DV = 8, 128, 128
DN_CONV_DIM, CONV_W = 3072, 4
CHUNK = 64
SUB = 16
GLA_BAND = 8
PAD_TOKENS = 16
DN_GROUP = 8
LANES = 128
VMEM_LIMIT = 56 * 1024 * 1024


def _mm(a, b):
    return jnp.dot(a.astype(BF16), b.astype(BF16), preferred_element_type=F32)


def _mm_nt(a, b):
    return lax.dot_general(a.astype(BF16), b.astype(BF16), (((1,), (1,)), ((), ())),
                           preferred_element_type=F32)


def _mm_tn(a, b):
    return lax.dot_general(a.astype(BF16), b.astype(BF16), (((0,), (0,)), ((), ())),
                           preferred_element_type=F32)


def _split3(a):
    a1 = a.astype(BF16)
    r1 = a - a1.astype(F32)
    a2 = r1.astype(BF16)
    a3 = (r1 - a2.astype(F32)).astype(BF16)
    return a1, a2, a3


def _mm_tn_exact(a, m_bf16):
    a1, a2, a3 = _split3(a)
    d = lambda x: lax.dot_general(x, m_bf16, (((0,), (0,)), ((), ())), preferred_element_type=F32)
    return (d(a1) + d(a2)) + d(a3)


def _mm_hi(a, b):
    a1 = a.astype(BF16)
    a2 = (a - a1.astype(F32)).astype(BF16)
    b1 = b.astype(BF16)
    b2 = (b - b1.astype(F32)).astype(BF16)
    d = lambda x, y: jnp.dot(x, y, preferred_element_type=F32)
    return (d(a1, b1) + d(a1, b2)) + d(a2, b1)


def _sigmoid(x):
    return 1.0 / (1.0 + jnp.exp(-x))


def _silu(x):
    return x * _sigmoid(x)


def _softplus(x):
    return jnp.maximum(x, 0.0) + jnp.log1p(jnp.exp(-jnp.abs(x)))


def _rms_rows(x, w):
    ms = jnp.mean(x * x, axis=-1, keepdims=True)
    return x * lax.rsqrt(ms + EPS) * w


def _const_spec(shape):
    nd = len(shape)
    return pl.BlockSpec(shape, lambda i, _nd=nd: (0,) * _nd)


def _params(**flags):
    return pltpu.CompilerParams(dimension_semantics=("arbitrary",), vmem_limit_bytes=VMEM_LIMIT,
                                flags=flags or None)


def _row_tile(t):
    return min(256, t)


def _proj_even_kernel(x_ref, nw_ref, wq_ref, wk_ref, wv_ref, wr_ref, wlr_ref, wu_ref, wsg_ref,
                      wg_ref, bg_ref, q_ref, k_ref, g_ref, v_ref, r_ref, u_ref, sg_ref):
    hb = _rms_rows(x_ref[...], nw_ref[...]).astype(BF16)
    d = lambda w_ref: jnp.dot(hb, w_ref[...], preferred_element_type=F32)
    q = d(wq_ref) * (GLA_DK ** -0.5)
    k = d(wk_ref)
    lr = d(wlr_ref)
    logit = jnp.dot(lr.astype(BF16), wg_ref[...], preferred_element_type=F32) + bg_ref[...]
    g = -_softplus(-logit) / GLA_GATE_NORM
    for h in range(GLA_HEADS):
        sl = slice(h * LANES, (h + 1) * LANES)
        q_ref[h] = q[:, sl]
        k_ref[h] = k[:, sl]
        g_ref[h] = g[:, sl]
    v = d(wv_ref)
    for s in range(2 * GLA_HEADS):
        v_ref[s] = v[:, s * LANES:(s + 1) * LANES]
    r_ref[...] = d(wr_ref)
    u_ref[...] = d(wu_ref)
    sg_ref[...] = d(wsg_ref)


def _proj_even(x, nw, w):
    t = x.shape[0]
    tm = _row_tile(t)
    row = lambda c: pl.BlockSpec((tm, c), lambda i: (i, 0))
    slab = lambda n: pl.BlockSpec((n, tm, LANES), lambda i: (0, i, 0))
    out_shape = (
        jax.ShapeDtypeStruct((GLA_HEADS, t, LANES), F32),
        jax.ShapeDtypeStruct((GLA_HEADS, t, LANES), F32),
        jax.ShapeDtypeStruct((GLA_HEADS, t, LANES), F32),
        jax.ShapeDtypeStruct((2 * GLA_HEADS, t, LANES), F32),
        jax.ShapeDtypeStruct((t, 1024), F32),
        jax.ShapeDtypeStruct((t, S5_WIDTH), F32),
        jax.ShapeDtypeStruct((t, S5_WIDTH), F32),
    )
    weights = (w["wq"], w["wk"], w["wv"], w["wr"], w["wlr"], w["wu"], w["wsg"], w["wg"], w["bg"])
    return pl.pallas_call(
        _proj_even_kernel,
        grid=(t // tm,),
        in_specs=[row(D_MODEL), _const_spec(nw.shape)] + [_const_spec(a.shape) for a in weights],
        out_specs=(slab(4), slab(4), slab(4), slab(8), row(1024), row(S5_WIDTH), row(S5_WIDTH)),
        out_shape=out_shape,
        compiler_params=_params(),
        name="proj_even",
    )(x, nw, *weights)


def _proj_odd_kernel(x_ref, nw_ref, wqkv_ref, wz_ref, wb_ref, wa_ref, alog_ref, dtb_ref,
                     xqkv_ref, z_ref, beta_ref, gd_ref):
    hb = _rms_rows(x_ref[...], nw_ref[...]).astype(BF16)
    d = lambda w_ref: jnp.dot(hb, w_ref[...], preferred_element_type=F32)
    xqkv_ref[...] = d(wqkv_ref)
    z_ref[...] = d(wz_ref)
    beta_ref[...] = _sigmoid(d(wb_ref))
    gd_ref[...] = -jnp.exp(alog_ref[...]) * _softplus(d(wa_ref) + dtb_ref[...])


def _proj_odd(x, nw, w):
    t = x.shape[0]
    tm = _row_tile(t)
    row = lambda c: pl.BlockSpec((tm, c), lambda i: (i, 0))
    weights = (w["wqkv"], w["wz"], w["wb"], w["wa"], w["alog"], w["dtb"])
    return pl.pallas_call(
        _proj_odd_kernel,
        grid=(t // tm,),
        in_specs=[row(D_MODEL), _const_spec(nw.shape)] + [_const_spec(a.shape) for a in weights],
        out_specs=(row(DN_CONV_DIM), row(1024), row(LANES), row(LANES)),
        out_shape=(
            jax.ShapeDtypeStruct((t, DN_CONV_DIM), F32),
            jax.ShapeDtypeStruct((t, 1024), F32),
            jax.ShapeDtypeStruct((t, LANES), F32),
            jax.ShapeDtypeStruct((t, LANES), F32),
        ),
        compiler_params=_params(),
        name="proj_odd",
    )(x, nw, *weights)


def _out_even_kernel(x_ref, og_ref, y5_ref, wa_ref, wb_ref, o_ref):
    o_ref[...] = x_ref[...] + (_mm(og_ref[...], wa_ref[...]) + _mm(y5_ref[...], wb_ref[...]))


def _out_even(x, og, y5, wa, wb):
    t = x.shape[0]
    tm = _row_tile(t)
    row = lambda c: pl.BlockSpec((tm, c), lambda i: (i, 0))
    return pl.pallas_call(
        _out_even_kernel,
        grid=(t // tm,),
        in_specs=[row(D_MODEL), row(1024), row(S5_WIDTH), _const_spec(wa.shape), _const_spec(wb.shape)],
        out_specs=row(D_MODEL),
        out_shape=jax.ShapeDtypeStruct((t, D_MODEL), F32),
        compiler_params=_params(),
        name="out_even",
    )(x, og, y5, wa, wb)


def _out_odd_kernel(x_ref, og_ref, w_ref, fw_ref, o_ref, *, final):
    y = x_ref[...] + _mm(og_ref[...], w_ref[...])
    if final:
        y = _rms_rows(y, fw_ref[...])
    o_ref[...] = y


def _out_odd(x, og, w, fw, final):
    t = x.shape[0]
    tm = _row_tile(t)
    row = lambda c: pl.BlockSpec((tm, c), lambda i: (i, 0))
    return pl.pallas_call(
        functools.partial(_out_odd_kernel, final=final),
        grid=(t // tm,),
        in_specs=[row(D_MODEL), row(1024), _const_spec(w.shape), _const_spec(fw.shape)],
        out_specs=row(D_MODEL),
        out_shape=jax.ShapeDtypeStruct((t, D_MODEL), F32),
        compiler_params=_params(),
        name="out_odd",
    )(x, og, w, fw)


def _cumsum_tokens(ref, lead, pad, rows, nb):
    shift = nb
    while shift <= pad:
        ref[lead, pad:pad + rows, :] = ref[lead, pad:pad + rows, :] + ref[lead, pad - shift:pad + rows - shift, :]
        shift *= 2
    while shift < rows:
        ref[lead, pad + shift:pad + rows, :] = (ref[lead, pad + shift:pad + rows, :]
                                                + ref[lead, pad:pad + rows - shift, :])
        shift *= 2


def _gla_prompt_kernel(q_ref, k_ref, g_ref, v_ref, r_ref, far_ref, nw_ref, og_ref, sout_ref,
                       kp, bp, vp, st, oscr, *, nb):
    rows = CHUNK * nb
    pad = PAD_TOKENS * nb
    blk_rows = GLA_BAND * nb
    step = pl.program_id(0)

    @pl.when(step == 0)
    def _():
        st[...] = jnp.zeros_like(st)
        kp[:, 0:pad, :] = jnp.zeros((GLA_HEADS, pad, LANES), F32)
        bp[:, 0:pad, :] = jnp.zeros((GLA_HEADS, pad, LANES), F32)
        vp[:, 0:pad, :] = jnp.zeros((2 * GLA_HEADS, pad, LANES), F32)

    for h in range(GLA_HEADS):
        bp[h, pad:pad + rows, :] = g_ref[h]
        kp[h, pad:pad + rows, :] = k_ref[h]
        _cumsum_tokens(bp, h, pad, rows, nb)
    for s in range(2 * GLA_HEADS):
        vp[s, pad:pad + rows, :] = v_ref[s]

    def band_tile(ti, carry):
        r0 = pl.multiple_of(ti * 64, 64)
        for h in range(GLA_HEADS):
            qt = q_ref[h, pl.ds(r0, 64), :]
            bt = bp[h, pl.ds(pad + r0, 64), :]
            acc0 = jnp.zeros((64, LANES), F32)
            acc1 = jnp.zeros((64, LANES), F32)
            for d in range(GLA_BAND):
                off = pl.multiple_of(pad + r0 - d * nb, 8)
                ks = kp[h, pl.ds(off, 64), :]
                bs = bp[h, pl.ds(off, 64), :]
                w = jnp.sum(qt * ks * jnp.exp(bt - bs), axis=-1, keepdims=True)
                acc0 = acc0 + w * vp[2 * h, pl.ds(off, 64), :]
                acc1 = acc1 + w * vp[2 * h + 1, pl.ds(off, 64), :]
            oscr[2 * h, pl.ds(r0, 64), :] = acc0
            oscr[2 * h + 1, pl.ds(r0, 64), :] = acc1
        return carry

    lax.fori_loop(0, rows // 64, band_tile, 0)

    nblk = CHUNK // GLA_BAND
    for h in range(GLA_HEADS):
        probs = []
        for blk in range(1, nblk):
            lo = blk * blk_rows
            ref_b = bp[h, pad + lo - nb:pad + lo, :]
            qs = q_ref[h, lo:lo + blk_rows, :] * jnp.exp(
                bp[h, pad + lo:pad + lo + blk_rows, :] - jnp.concatenate([ref_b] * GLA_BAND, axis=0))
            ks = kp[h, pad:pad + lo, :] * jnp.exp(
                jnp.concatenate([ref_b] * (GLA_BAND * blk), axis=0) - bp[h, pad:pad + lo, :])
            probs.append(_mm_nt(qs, ks) * far_ref[lo:lo + blk_rows, 0:lo])
        for blk in range(1, nblk):
            lo = blk * blk_rows
            p = probs[blk - 1].astype(BF16)
            for half in range(2):
                sl = 2 * h + half
                oscr[sl, lo:lo + blk_rows, :] = oscr[sl, lo:lo + blk_rows, :] + jnp.dot(
                    p, vp[sl, pad:pad + lo, :].astype(BF16), preferred_element_type=F32)

    def per_batch(b, carry):
        sel = pl.ds(b, CHUNK, stride=nb)
        selp = pl.ds(pad + b, CHUNK, stride=nb)
        loaded = [(q_ref[h, sel, :], kp[h, selp, :], bp[h, selp, :], vp[2 * h, selp, :], vp[2 * h + 1, selp, :],
                   st[b, h]) for h in range(GLA_HEADS)]
        outs = [_mm_nt(qb * jnp.exp(bb), stt) for qb, _, bb, _, _, stt in loaded]
        news = []
        for _, kb, bb, v0, v1, stt in loaded:
            blast = bb[CHUNK - 1:CHUNK, :]
            kd = kb * jnp.exp(blast - bb)
            news.append(stt * jnp.exp(blast) + _mm_tn(jnp.concatenate([v0, v1], axis=1), kd))
        for h in range(GLA_HEADS):
            oscr[2 * h, sel, :] = oscr[2 * h, sel, :] + outs[h][:, :LANES]
            oscr[2 * h + 1, sel, :] = oscr[2 * h + 1, sel, :] + outs[h][:, LANES:]
            st[b, h] = news[h]
        return carry

    lax.fori_loop(0, nb, per_batch, 0)

    nw = nw_ref[...]

    def epi_tile(ti, carry):
        r0 = pl.multiple_of(ti * 64, 64)
        for h in range(GLA_HEADS):
            o0 = oscr[2 * h, pl.ds(r0, 64), :]
            o1 = oscr[2 * h + 1, pl.ds(r0, 64), :]
            ms = (jnp.sum(o0 * o0, axis=-1, keepdims=True) + jnp.sum(o1 * o1, axis=-1, keepdims=True)) / GLA_DV
            inv = lax.rsqrt(ms + EPS)
            c0 = h * GLA_DV
            og_ref[pl.ds(r0, 64), c0:c0 + LANES] = o0 * inv * nw[:, :LANES] * _silu(r_ref[pl.ds(r0, 64), c0:c0 + LANES])
            og_ref[pl.ds(r0, 64), c0 + LANES:c0 + 2 * LANES] = (
                o1 * inv * nw[:, LANES:] * _silu(r_ref[pl.ds(r0, 64), c0 + LANES:c0 + 2 * LANES]))
        return carry

    lax.fori_loop(0, rows // 64, epi_tile, 0)

    @pl.when(step == pl.num_programs(0) - 1)
    def _():
        def wr(b, carry):
            for h in range(GLA_HEADS):
                sout_ref[b, h] = st[b, h].T
            return carry
        lax.fori_loop(0, nb, wr, 0)


def _pair_mask(nb, min_dist):
    n = CHUNK * nb
    r = jnp.arange(n)
    same = (r[:, None] % nb) == (r[None, :] % nb)
    far = (r[None, :] // nb) <= (r[:, None] // nb) - min_dist
    return (same & far).astype(F32)


def _gla_prompt(q, k, g, v, r, nw, nb):
    t = r.shape[0]
    rows = CHUNK * nb
    pad = PAD_TOKENS * nb
    slab = lambda n: pl.BlockSpec((n, rows, LANES), lambda i: (0, i, 0))
    row = lambda c: pl.BlockSpec((rows, c), lambda i: (i, 0))
    far = _pair_mask(nb, GLA_BAND)
    return pl.pallas_call(
        functools.partial(_gla_prompt_kernel, nb=nb),
        grid=(t // rows,),
        in_specs=[slab(4), slab(4), slab(4), slab(8), row(1024), _const_spec(far.shape), _const_spec(nw.shape)],
        out_specs=(row(1024), _const_spec((nb, GLA_HEADS, GLA_DK, GLA_DV))),
        out_shape=(jax.ShapeDtypeStruct((t, 1024), F32),
                   jax.ShapeDtypeStruct((nb, GLA_HEADS, GLA_DK, GLA_DV), F32)),
        scratch_shapes=[
            pltpu.VMEM((GLA_HEADS, pad + rows, LANES), F32),
            pltpu.VMEM((GLA_HEADS, pad + rows, LANES), F32),
            pltpu.VMEM((2 * GLA_HEADS, pad + rows, LANES), F32),
            pltpu.VMEM((nb, GLA_HEADS, GLA_DV, GLA_DK), F32),
            pltpu.VMEM((2 * GLA_HEADS, rows, LANES), F32),
        ],
        compiler_params=_params(),
        name="gla_prompt",
    )(q, k, g, v, r, far, nw)


def _gla_step_kernel(q_ref, k_ref, g_ref, v_ref, r_ref, s_ref, nw_ref, og_ref, sout_ref):
    nw = nw_ref[...]
    zeros = jnp.zeros((LANES - 24, LANES), F32)
    for h in range(GLA_HEADS):
        stack = jnp.concatenate([jnp.exp(g_ref[h]), k_ref[h], q_ref[h], zeros], axis=0)
        cols = stack.T
        outs = []
        for n in range(8):
            s_old = s_ref[n, h]
            vrow = jnp.concatenate([v_ref[2 * h, n:n + 1, :], v_ref[2 * h + 1, n:n + 1, :]], axis=1)
            s_new = s_old * cols[:, n:n + 1] + cols[:, 8 + n:9 + n] * vrow
            sout_ref[n, h] = s_new
            outs.append(jnp.sum(cols[:, 16 + n:17 + n] * s_new, axis=0, keepdims=True))
        o = jnp.concatenate(outs, axis=0)
        ms = jnp.mean(o * o, axis=-1, keepdims=True)
        c0 = h * GLA_DV
        og_ref[:, c0:c0 + GLA_DV] = o * lax.rsqrt(ms + EPS) * nw * _silu(r_ref[:, c0:c0 + GLA_DV])


def _gla_step(q, k, g, v, r, s, nw):
    n = r.shape[0]
    slab = lambda c: pl.BlockSpec((c, 8, LANES), lambda i: (0, i, 0))
    sspec = pl.BlockSpec((8, GLA_HEADS, GLA_DK, GLA_DV), lambda i: (i, 0, 0, 0))
    return pl.pallas_call(
        _gla_step_kernel,
        grid=(n // 8,),
        in_specs=[slab(4), slab(4), slab(4), slab(8), pl.BlockSpec((8, 1024), lambda i: (i, 0)), sspec,
                  _const_spec(nw.shape)],
        out_specs=(pl.BlockSpec((8, 1024), lambda i: (i, 0)), sspec),
        out_shape=(jax.ShapeDtypeStruct((n, 1024), F32), jax.ShapeDtypeStruct(s.shape, F32)),
        compiler_params=_params(),
        name="gla_step",
    )(q, k, g, v, r, s, nw)


def _s5_prep_kernel(lre_ref, lim_ref, ldt_ref, lre_r_ref, lim_r_ref, ldt_r_ref, bre_ref, bim_ref,
                    ar_ref, ai_ref, bbre_ref, bbim_ref):
    def disc(lre, lim, ldt):
        dt = jnp.exp(ldt)
        mag = jnp.exp(lre * dt)
        ar = mag * jnp.cos(lim * dt)
        ai = mag * jnp.sin(lim * dt)
        return ar, ai

    ar, ai = disc(lre_ref[...], lim_ref[...], ldt_ref[...])
    ar_ref[...] = ar
    ai_ref[...] = ai
    lre, lim = lre_r_ref[...], lim_r_ref[...]
    ar, ai = disc(lre, lim, ldt_r_ref[...])
    den = lre * lre + lim * lim
    wr = ((ar - 1.0) * lre + ai * lim) / den
    wi = (ai * lre - (ar - 1.0) * lim) / den
    bre, bim = bre_ref[...], bim_ref[...]
    bbre_ref[...] = wr * bre - wi * bim
    bbim_ref[...] = wr * bim + wi * bre


def _s5_prep(lam_re, lam_im, log_dt, b_re, b_im):
    n = lam_re.shape[0]
    rows = n * S5_GROUPS
    lre = lam_re.reshape(rows, S5_P)
    lim = lam_im.reshape(rows, S5_P)
    ldt = jnp.broadcast_to(log_dt.reshape(rows, 1), (rows, S5_P))
    rep = lambda a: jnp.repeat(a, S5_GROUP_CH, axis=1)
    args = (lre, lim, ldt, rep(lre), rep(lim), rep(ldt),
            b_re.reshape(rows, S5_P * S5_GROUP_CH), b_im.reshape(rows, S5_P * S5_GROUP_CH))
    wide = jax.ShapeDtypeStruct((rows, S5_P * S5_GROUP_CH), F32)
    narrow = jax.ShapeDtypeStruct((rows, S5_P), F32)
    ar, ai, bbre, bbim = pl.pallas_call(
        _s5_prep_kernel, out_shape=(narrow, narrow, wide, wide), name="s5_prep")(*args)
    shp = (n, S5_GROUPS, S5_P, S5_GROUP_CH)
    return ar.reshape(n, S5_GROUPS, S5_P), ai.reshape(n, S5_GROUPS, S5_P), bbre.reshape(shp), bbim.reshape(shp)


def _s5_kernel(u_ref, sg_ref, h0re_ref, h0im_ref, wbre_ref, wbim_ref, wc_ref, are_ref, aim_ref, d_ref,
               wglu_ref, bglu_ref, y_ref, hre_out, him_out, hre, him, cre, cim, ys, *, nb, tokens):
    step = pl.program_id(0)
    rows = nb * tokens

    @pl.when(step == 0)
    def _():
        cre[...] = h0re_ref[...]
        cim[...] = h0im_ref[...]

    for m in range(4):
        um = u_ref[:, m * LANES:(m + 1) * LANES].astype(BF16)
        hre[:, m * 512:(m + 1) * 512] = jnp.dot(um, wbre_ref[m], preferred_element_type=F32)
        him[:, m * 512:(m + 1) * 512] = jnp.dot(um, wbim_ref[m], preferred_element_type=F32)

    for qd in range(4):
        ql = slice(qd * 512, (qd + 1) * 512)
        ar = jnp.broadcast_to(are_ref[:, ql], (nb, 512))
        ai = jnp.broadcast_to(aim_ref[:, ql], (nb, 512))

        def tok(t, carry, ql=ql, ar=ar, ai=ai):
            hr, hi = carry
            sel = pl.ds(pl.multiple_of(t * nb, nb), nb)
            nr = ar * hr - ai * hi + hre[sel, ql]
            ni = ar * hi + ai * hr + him[sel, ql]
            hre[sel, ql] = nr
            him[sel, ql] = ni
            return nr, ni

        hr, hi = lax.fori_loop(0, tokens, tok, (cre[:, ql], cim[:, ql]))
        cre[:, ql] = hr
        cim[:, ql] = hi

    for m in range(4):
        hc = jnp.concatenate([hre[:, m * 512:(m + 1) * 512], him[:, m * 512:(m + 1) * 512]], axis=1)
        sl = slice(m * LANES, (m + 1) * LANES)
        y = _mm(hc, wc_ref[m]) + d_ref[:, sl] * u_ref[:, sl]
        ys[:, sl] = jax.nn.gelu(y)
    y = ys[...]
    gate = _sigmoid(_mm(y, wglu_ref[...]) + bglu_ref[...])
    y_ref[...] = y * gate * _silu(sg_ref[...])

    @pl.when(step == pl.num_programs(0) - 1)
    def _():
        hre_out[...] = cre[...]
        him_out[...] = cim[...]


def _s5(u, sg, h0re, h0im, w, nb, tokens):
    t = u.shape[0]
    rows = nb * tokens
    row = lambda c: pl.BlockSpec((rows, c), lambda i: (i, 0))
    weights = (w["wbre"], w["wbim"], w["wc"], w["are"], w["aim"], w["d"], w["wglu"], w["bglu"])
    state = jax.ShapeDtypeStruct((nb, S5_STATE), F32)
    return pl.pallas_call(
        functools.partial(_s5_kernel, nb=nb, tokens=tokens),
        grid=(t // rows,),
        in_specs=[row(S5_WIDTH), row(S5_WIDTH), _const_spec(h0re.shape), _const_spec(h0im.shape)]
        + [_const_spec(a.shape) for a in weights],
        out_specs=(row(S5_WIDTH), _const_spec((nb, S5_STATE)), _const_spec((nb, S5_STATE))),
        out_shape=(jax.ShapeDtypeStruct((t, S5_WIDTH), F32), state, state),
        scratch_shapes=[
            pltpu.VMEM((rows, S5_STATE), F32),
            pltpu.VMEM((rows, S5_STATE), F32),
            pltpu.VMEM((nb, S5_STATE), F32),
            pltpu.VMEM((nb, S5_STATE), F32),
            pltpu.VMEM((rows, S5_WIDTH), F32),
        ],
        compiler_params=_params(),
        name="s5",
    )(u, sg, h0re, h0im, *weights)


def _dn_solve(lms, rhss):
    n = len(lms)
    xs = [[] for _ in range(n)]
    for blk in range(CHUNK // SUB):
        lo, hi = SUB * blk, SUB * (blk + 1)
        curs = [rhs[lo:hi] for rhs in rhss]
        if blk:
            curs = [cur - _mm_hi(lm[lo:hi, 0:lo], jnp.concatenate(x, axis=0))
                    for cur, lm, x in zip(curs, lms, xs)]
        ldiags = [lm[lo:hi, lo:hi] for lm in lms]
        for j in range(SUB - 1):
            curs = [cur - ld[:, j:j + 1] * cur[j:j + 1, :] for cur, ld in zip(curs, ldiags)]
        for x, cur in zip(xs, curs):
            x.append(cur)
    return [jnp.concatenate(x, axis=0) for x in xs]


def _dn_prompt_kernel(x_ref, z_ref, beta_ref, gd_ref, cw_ref, nw_ref, c0_ref,
                      og_ref, sout_ref, cout_ref,
                      xs, qs, ks, vs, bcol, gcol, gc, gt, oscr, *, nb):
    rows = CHUNK * nb
    hist = (CONV_W - 1) * nb
    pad = PAD_TOKENS * nb
    step = pl.program_id(0)
    st = sout_ref

    @pl.when(step == 0)
    def _():
        st[...] = jnp.zeros_like(st)
        xs[0:hist, :] = c0_ref[...]
        gc[0, 0:pad, :] = jnp.zeros((pad, LANES), F32)

    xs[hist:hist + rows, :] = x_ref[...]

    def conv_tile(ti, carry):
        r0 = pl.multiple_of(ti * 64, 64)
        for c in range(DN_CONV_DIM // LANES):
            cl = slice(c * LANES, (c + 1) * LANES)
            y = xs[pl.ds(r0, 64), cl] * cw_ref[0:1, cl]
            for i in range(1, CONV_W):
                y = y + xs[pl.ds(r0 + i * nb, 64), cl] * cw_ref[i:i + 1, cl]
            y = _silu(y)
            if c < 2 * DN_HEADS:
                y = y * lax.rsqrt(jnp.sum(y * y, axis=-1, keepdims=True) + EPS)
            if c < DN_HEADS:
                qs[c, pl.ds(r0, 64), :] = y * (DN_DK ** -0.5)
            elif c < 2 * DN_HEADS:
                ks[c - DN_HEADS, pl.ds(r0, 64), :] = y
            else:
                vs[c - 2 * DN_HEADS, pl.ds(r0, 64), :] = y
        return carry

    lax.fori_loop(0, rows // 64, conv_tile, 0)

    xs[0:hist, :] = xs[rows:rows + hist, :]

    gc[0, pad:pad + rows, :] = gd_ref[...]
    _cumsum_tokens(gc, 0, pad, rows, nb)

    def bc_tile(ti, carry):
        r0 = pl.multiple_of(ti * 64, 64)
        gtile = gc[0, pl.ds(pad + r0, 64), :]
        btile = beta_ref[pl.ds(r0, 64), :]
        for h in range(DN_HEADS):
            gcol[h, pl.ds(r0, 64), :] = jnp.broadcast_to(gtile[:, h:h + 1], (64, LANES))
            bcol[h, pl.ds(r0, 64), :] = jnp.broadcast_to(btile[:, h:h + 1], (64, LANES))
        return carry

    lax.fori_loop(0, rows // 64, bc_tile, 0)

    ii = lax.broadcasted_iota(jnp.int32, (CHUNK, CHUNK), 0)
    jj = lax.broadcasted_iota(jnp.int32, (CHUNK, CHUNK), 1)
    eye = (ii == jj).astype(BF16)

    def per_batch(b, carry):
        sel = pl.ds(b, CHUNK, stride=nb)
        gt[...] = _mm_tn_exact(gc[0, pl.ds(pad + b, CHUNK, stride=nb), :], eye)

        def head_group(gi, carry2):
            heads = [gi * DN_GROUP + j for j in range(DN_GROUP)]
            loaded = [(qs[h, sel, :], ks[h, sel, :], vs[h, sel, :], bcol[h, sel, :], gcol[h, sel, :],
                       gt[pl.ds(h, 1), :], st[b, h]) for h in heads]
            qbs, kbs, vbs, bcs, gcls, grows, s_olds = zip(*loaded)
            decs = [jnp.where(jj <= ii, jnp.exp(jnp.where(jj <= ii, gcl[:, :CHUNK] - grow, 0.0)), 0.0)
                    for gcl, grow in zip(gcls, grows)]
            kks = [_mm_nt(kb, kb) for kb in kbs]
            qks = [_mm_nt(qb, kb) for qb, kb in zip(qbs, kbs)]
            lms = [jnp.where(jj < ii, bc[:, :CHUNK] * kk * dec, 0.0) for bc, kk, dec in zip(bcs, kks, decs)]
            egs = [jnp.exp(gcl) for gcl in gcls]
            rhss = [jnp.concatenate([vb * bc, kb * (bc * eg)], axis=1)
                    for vb, kb, bc, eg in zip(vbs, kbs, bcs, egs)]
            sols = _dn_solve(lms, rhss)
            v_news = [sol[:, :DN_DV] - _mm(sol[:, DN_DV:], s_old) for sol, s_old in zip(sols, s_olds)]
            os_ = [_mm(qb * eg, s_old) + _mm(qk * dec, v_new)
                   for qb, eg, s_old, qk, dec, v_new in zip(qbs, egs, s_olds, qks, decs, v_news)]
            glasts = [gcl[CHUNK - 1:CHUNK, :] for gcl in gcls]
            s_news = [s_old * jnp.exp(glast) + _mm_tn(kb * jnp.exp(glast - gcl), v_new)
                      for s_old, glast, kb, gcl, v_new in zip(s_olds, glasts, kbs, gcls, v_news)]
            for h, o, s_new in zip(heads, os_, s_news):
                st[b, h] = s_new
                oscr[h, sel, :] = o
            return carry2

        lax.fori_loop(0, DN_HEADS // DN_GROUP, head_group, 0)
        return carry

    lax.fori_loop(0, nb, per_batch, 0)

    nw = nw_ref[...]

    def epi_tile(ti, carry):
        r0 = pl.multiple_of(ti * 64, 64)
        for h in range(DN_HEADS):
            o = oscr[h, pl.ds(r0, 64), :]
            inv = lax.rsqrt(jnp.mean(o * o, axis=-1, keepdims=True) + EPS)
            cl = slice(h * LANES, (h + 1) * LANES)
            og_ref[pl.ds(r0, 64), cl] = o * inv * nw * _silu(z_ref[pl.ds(r0, 64), cl])
        return carry

    lax.fori_loop(0, rows // 64, epi_tile, 0)

    @pl.when(step == pl.num_programs(0) - 1)
    def _():
        cout_ref[...] = xs[0:hist, :]


def _dn_prompt(xqkv, z, beta, gd, cw, nw, c0, nb):
    t = z.shape[0]
    rows = CHUNK * nb
    hist = (CONV_W - 1) * nb
    row = lambda c: pl.BlockSpec((rows, c), lambda i: (i, 0))
    slab = pltpu.VMEM((DN_HEADS, rows, LANES), F32)
    return pl.pallas_call(
        functools.partial(_dn_prompt_kernel, nb=nb),
        grid=(t // rows,),
        in_specs=[row(DN_CONV_DIM), row(1024), row(LANES), row(LANES), _const_spec(cw.shape),
                  _const_spec(nw.shape), _const_spec(c0.shape)],
        out_specs=(row(1024), _const_spec((nb, DN_HEADS, DN_DK, DN_DV)), _const_spec((hist, DN_CONV_DIM))),
        out_shape=(jax.ShapeDtypeStruct((t, 1024), F32),
                   jax.ShapeDtypeStruct((nb, DN_HEADS, DN_DK, DN_DV), F32),
                   jax.ShapeDtypeStruct((hist, DN_CONV_DIM), F32)),
        scratch_shapes=[
            pltpu.VMEM((hist + rows, DN_CONV_DIM), F32),
            slab, slab, slab, slab, slab,
            pltpu.VMEM((1, PAD_TOKENS * nb + rows, LANES), F32),
            pltpu.VMEM((LANES, CHUNK), F32),
            slab,
        ],
        compiler_params=_params(),
        name="dn_prompt",
    )(xqkv, z, beta, gd, cw, nw, c0)


def _dn_step_kernel(x_ref, cb_ref, z_ref, beta_ref, gd_ref, s_ref, cw_ref, nw_ref, og_ref, sout_ref, cn_ref):
    nw = nw_ref[...]
    cn_ref[0] = cb_ref[1]
    cn_ref[1] = cb_ref[2]
    cn_ref[2] = x_ref[...]
    zeros = jnp.zeros((LANES - 16, LANES), F32)
    beta = beta_ref[...]
    eg_all = jnp.exp(gd_ref[...])

    def conv(c):
        cl = slice(c * LANES, (c + 1) * LANES)
        y = cb_ref[0, :, cl] * cw_ref[0:1, cl]
        y = y + cb_ref[1, :, cl] * cw_ref[1:2, cl]
        y = y + cb_ref[2, :, cl] * cw_ref[2:3, cl]
        y = y + x_ref[:, cl] * cw_ref[3:4, cl]
        return _silu(y)

    def l2(y):
        return y * lax.rsqrt(jnp.sum(y * y, axis=-1, keepdims=True) + EPS)

    for h in range(DN_HEADS):
        qh = l2(conv(h)) * (DN_DK ** -0.5)
        kh = l2(conv(DN_HEADS + h))
        vh = conv(2 * DN_HEADS + h)
        cols = jnp.concatenate([kh, qh, zeros], axis=0).T
        outs = []
        for n in range(8):
            s_old = s_ref[n, h]
            kcol = cols[:, n:n + 1]
            eg = eg_all[n:n + 1, h:h + 1]
            ks_row = jnp.sum(kcol * s_old, axis=0, keepdims=True)
            v_new = beta[n:n + 1, h:h + 1] * (vh[n:n + 1, :] - eg * ks_row)
            s_new = s_old * eg + kcol * v_new
            sout_ref[n, h] = s_new
            outs.append(jnp.sum(cols[:, 8 + n:9 + n] * s_new, axis=0, keepdims=True))
        o = jnp.concatenate(outs, axis=0)
        cl = slice(h * LANES, (h + 1) * LANES)
        og_ref[:, cl] = o * lax.rsqrt(jnp.mean(o * o, axis=-1, keepdims=True) + EPS) * nw * _silu(z_ref[:, cl])


def _dn_step(xqkv, cbuf, z, beta, gd, s, cw, nw):
    n = z.shape[0]
    row = lambda c: pl.BlockSpec((8, c), lambda i: (i, 0))
    cspec = pl.BlockSpec((CONV_W - 1, 8, DN_CONV_DIM), lambda i: (0, i, 0))
    sspec = pl.BlockSpec((8, DN_HEADS, DN_DK, DN_DV), lambda i: (i, 0, 0, 0))
    return pl.pallas_call(
        _dn_step_kernel,
        grid=(n // 8,),
        in_specs=[row(DN_CONV_DIM), cspec, row(1024), row(LANES), row(LANES), sspec,
                  _const_spec(cw.shape), _const_spec(nw.shape)],
        out_specs=(row(1024), sspec, cspec),
        out_shape=(jax.ShapeDtypeStruct((n, 1024), F32), jax.ShapeDtypeStruct(s.shape, F32),
                   jax.ShapeDtypeStruct(cbuf.shape, F32)),
        compiler_params=_params(),
        name="dn_step",
    )(xqkv, cbuf, z, beta, gd, s, cw, nw)


def _pad_cols(a, n):
    return jnp.pad(a, ((0, 0), (0, n - a.shape[1])))


def _even_weights(w_in, w_gate_up, b_gate, w_out):
    o = [0, 512, 1024, 2048, 3072, 3088, 3600, 4112]
    cut = lambda i: w_in[:, o[i]:o[i + 1]]
    return {
        "wq": cut(0).astype(BF16), "wk": cut(1).astype(BF16), "wv": cut(2).astype(BF16),
        "wr": cut(3).astype(BF16), "wlr": _pad_cols(cut(4), LANES).astype(BF16),
        "wu": cut(5).astype(BF16), "wsg": cut(6).astype(BF16),
        "wg": jnp.pad(w_gate_up, ((0, LANES - GLA_RANK), (0, 0))).astype(BF16),
        "bg": b_gate.reshape(1, -1),
        "wo_a": w_out[:1024].astype(BF16), "wo_b": w_out[1024:].astype(BF16),
    }


def _s5_weights(ar, ai, bbre, bbim, c_re, c_im, d, w_glu, b_glu):
    eye = jnp.eye(8, dtype=F32)

    def pack_b(bb):
        t = bb.reshape(4, 8, S5_P, S5_GROUP_CH).transpose(0, 1, 3, 2)
        return (t[:, :, :, None, :] * eye[None, :, None, :, None]).reshape(4, 128, 512).astype(BF16)

    def pack_c(c):
        t = c.reshape(4, 8, S5_GROUP_CH, S5_P).transpose(0, 1, 3, 2)
        return (t[:, :, :, None, :] * eye[None, :, None, :, None]).reshape(4, 512, 128)

    return {
        "wbre": pack_b(bbre), "wbim": pack_b(bbim),
        "wc": jnp.concatenate([pack_c(c_re), -pack_c(c_im)], axis=1).astype(BF16),
        "are": ar.reshape(1, S5_STATE), "aim": ai.reshape(1, S5_STATE),
        "d": d.reshape(1, -1), "wglu": w_glu.astype(BF16), "bglu": b_glu.reshape(1, -1),
    }


def _odd_weights(w_in, conv_w, a_log, dt_bias, w_out):
    return {
        "wqkv": w_in[:, :3072].astype(BF16), "wz": w_in[:, 3072:4096].astype(BF16),
        "wb": _pad_cols(w_in[:, 4096:4104], LANES).astype(BF16),
        "wa": _pad_cols(w_in[:, 4104:4112], LANES).astype(BF16),
        "alog": _pad_cols(a_log.reshape(1, -1), LANES), "dtb": _pad_cols(dt_bias.reshape(1, -1), LANES),
        "cw": jnp.pad(conv_w, ((0, 8 - CONV_W), (0, 0))),
        "wo": w_out.astype(BF16),
    }


def kernel(x_prompt, x_sample, state_gla, state_s5_re, state_s5_im, state_delta, state_conv, norm_w, final_norm_w, w_in_even, gla_w_gate_up, gla_b_gate, gla_norm_w, s5_lambda_re, s5_lambda_im, s5_log_dt, s5_b_re, s5_b_im, s5_c_re, s5_c_im, s5_d, s5_w_glu, s5_b_glu, w_out_even, w_in_odd, dn_conv_w, dn_a_log, dn_dt_bias, dn_norm_w, w_out_odd):
    nb, seq, _ = x_prompt.shape
    ns = x_sample.shape[0]
    assert seq % CHUNK == 0 and nb % 8 == 0 and ns % 8 == 0 and x_sample.shape[1] == 1
    depth = norm_w.shape[0]

    xp = x_prompt.transpose(1, 0, 2).reshape(seq * nb, D_MODEL)
    xs = x_sample.reshape(ns, D_MODEL)
    fw = final_norm_w.reshape(1, -1)

    ar, ai, bbre, bbim = _s5_prep(s5_lambda_re, s5_lambda_im, s5_log_dt, s5_b_re, s5_b_im)

    gla_p, gla_s, s5r_p, s5i_p, s5r_s, s5i_s, dn_p, dn_s, cv_p, cv_s = ([] for _ in range(10))
    for layer in range(depth):
        i = layer // 2
        nw = norm_w[layer].reshape(1, -1)
        if layer % 2 == 0:
            w = _even_weights(w_in_even[i], gla_w_gate_up[i], gla_b_gate[i], w_out_even[i])
            w5 = _s5_weights(ar[i], ai[i], bbre[i], bbim[i], s5_c_re[i], s5_c_im[i], s5_d[i],
                             s5_w_glu[i], s5_b_glu[i])
            gnw = gla_norm_w[i].reshape(1, -1)
            q, k, g, v, r, u, sg = _proj_even(xp, nw, w)
            og, s_new = _gla_prompt(q, k, g, v, r, gnw, nb)
            zero = jnp.zeros((nb, S5_STATE), F32)
            y5, hre, him = _s5(u, sg, zero, zero, w5, nb, CHUNK)
            xp = _out_even(xp, og, y5, w["wo_a"], w["wo_b"])
            gla_p.append(s_new)
            s5r_p.append(hre.reshape(nb, S5_GROUPS, S5_P))
            s5i_p.append(him.reshape(nb, S5_GROUPS, S5_P))
            q, k, g, v, r, u, sg = _proj_even(xs, nw, w)
            og, s_new = _gla_step(q, k, g, v, r, state_gla[i], gnw)
            y5, hre, him = _s5(u, sg, state_s5_re[i].reshape(ns, S5_STATE), state_s5_im[i].reshape(ns, S5_STATE),
                               w5, ns, 1)
            xs = _out_even(xs, og, y5, w["wo_a"], w["wo_b"])
            gla_s.append(s_new)
            s5r_s.append(hre.reshape(ns, S5_GROUPS, S5_P))
            s5i_s.append(him.reshape(ns, S5_GROUPS, S5_P))
        else:
            w = _odd_weights(w_in_odd[i], dn_conv_w[i], dn_a_log[i], dn_dt_bias[i], w_out_odd[i])
            dnw = dn_norm_w[i].reshape(1, -1)
            final = layer == depth - 1
            xqkv, z, beta, gd = _proj_odd(xp, nw, w)
            c0 = jnp.zeros(((CONV_W - 1) * nb, DN_CONV_DIM), F32)
            og, s_new, c_new = _dn_prompt(xqkv, z, beta, gd, w["cw"], dnw, c0, nb)
            xp = _out_odd(xp, og, w["wo"], fw, final)
            dn_p.append(s_new)
            cv_p.append(c_new.reshape(CONV_W - 1, nb, DN_CONV_DIM).transpose(1, 0, 2))
            xqkv, z, beta, gd = _proj_odd(xs, nw, w)
            og, s_new, c_new = _dn_step(xqkv, state_conv[i].transpose(1, 0, 2), z, beta, gd, state_delta[i],
                                        w["cw"], dnw)
            xs = _out_odd(xs, og, w["wo"], fw, final)
            dn_s.append(s_new)
            cv_s.append(c_new.transpose(1, 0, 2))

    y_prompt = xp.reshape(seq, nb, D_MODEL).transpose(1, 0, 2)
    y_sample = xs.reshape(ns, 1, D_MODEL)
    st = jnp.stack
    return (y_prompt, y_sample, st(gla_p), st(gla_s), st(s5r_p), st(s5i_p), st(s5r_s), st(s5i_s),
            st(dn_p), st(dn_s), st(cv_p), st(cv_s))
```

```python
import functools
import math

import jax
import jax.numpy as jnp
from jax import lax
from jax.experimental import pallas as pl
from jax.experimental.pallas import tpu as pltpu

F32 = jnp.float32
BF16 = jnp.bfloat16
EPS = 1e-6

D_MODEL = 1024
GLA_HEADS, GLA_DK, GLA_DV, GLA_RANK = 4, 128, 256, 16
GLA_GATE_NORM = 16.0
S5_GROUPS, S5_GROUP_CH, S5_P, S5_WIDTH = 32, 16, 64, 512
S5_STATE = S5_GROUPS * S5_P
DN_HEADS, DN_DK, DN_DV = 8, 128, 128
DN_CONV_DIM, CONV_W = 3072, 4
CHUNK = 64
SUB = 8
GLA_BAND = 8
PAD_TOKENS = 16
DN_BATCH = 2
DN_GROUP = 8
LANES = 128
W_IN_PAD = 4224
VMEM_LIMIT = 56 * 1024 * 1024


def _mm(a, b):
    return jnp.dot(a.astype(BF16), b.astype(BF16), preferred_element_type=F32)


def _mm_nt(a, b):
    return lax.dot_general(a.astype(BF16), b.astype(BF16), (((1,), (1,)), ((), ())),
                           preferred_element_type=F32)


def _mm_tn(a, b):
    return lax.dot_general(a.astype(BF16), b.astype(BF16), (((0,), (0,)), ((), ())),
                           preferred_element_type=F32)


def _split3(a):
    a1 = a.astype(BF16)
    r1 = a - a1.astype(F32)
    a2 = r1.astype(BF16)
    a3 = (r1 - a2.astype(F32)).astype(BF16)
    return a1, a2, a3


def _mm_tn_exact(a, m_bf16):
    a1, a2, a3 = _split3(a)
    d = lambda x: lax.dot_general(x, m_bf16, (((0,), (0,)), ((), ())), preferred_element_type=F32)
    return (d(a1) + d(a2)) + d(a3)


def _sigmoid(x):
    return 1.0 / (1.0 + jnp.exp(-x))


def _silu(x):
    return x * _sigmoid(x)


def _softplus(x):
    return jnp.maximum(x, 0.0) + jnp.log1p(jnp.exp(-jnp.abs(x)))


def _rms_rows(x, w):
    ms = jnp.mean(x * x, axis=-1, keepdims=True)
    return x * lax.rsqrt(ms + EPS) * w


def _const_spec(shape):
    nd = len(shape)
    return pl.BlockSpec(shape, lambda i, _nd=nd: (0,) * _nd)


def _params(**flags):
    return pltpu.CompilerParams(dimension_semantics=("arbitrary",), vmem_limit_bytes=VMEM_LIMIT,
                                flags=flags or None)


def _row_tile(t):
    return min(256, t)


def _proj_even_kernel(x_ref, nw_ref, w_ref, wg_ref, bg_ref, q_ref, k_ref, g_ref, v_ref, r_ref, u_ref, sg_ref):
    hb = _rms_rows(x_ref[...], nw_ref[...]).astype(BF16)
    d = lambda lo, hi: jnp.dot(hb, w_ref[:, lo:hi], preferred_element_type=F32)
    q = d(0, 512) * (GLA_DK ** -0.5)
    k = d(512, 1024)
    lr_u = d(3072, 3712)
    u_sg = d(3584, 4224)
    logit = jnp.dot(lr_u[:, :LANES].astype(BF16), wg_ref[...], preferred_element_type=F32) + bg_ref[...]
    g = -_softplus(-logit) / GLA_GATE_NORM
    for h in range(GLA_HEADS):
        sl = slice(h * LANES, (h + 1) * LANES)
        q_ref[h] = q[:, sl]
        k_ref[h] = k[:, sl]
        g_ref[h] = g[:, sl]
    v = d(1024, 2048)
    for s in range(2 * GLA_HEADS):
        v_ref[s] = v[:, s * LANES:(s + 1) * LANES]
    r_ref[...] = d(2048, 3072)
    u_ref[...] = lr_u[:, GLA_RANK:GLA_RANK + S5_WIDTH]
    sg_ref[...] = u_sg[:, GLA_RANK:GLA_RANK + S5_WIDTH]


def _proj_even(x, nw, w):
    t = x.shape[0]
    tm = _row_tile(t)
    row = lambda c: pl.BlockSpec((tm, c), lambda i: (i, 0))
    slab = lambda n: pl.BlockSpec((n, tm, LANES), lambda i: (0, i, 0))
    out_shape = (
        jax.ShapeDtypeStruct((GLA_HEADS, t, LANES), F32),
        jax.ShapeDtypeStruct((GLA_HEADS, t, LANES), F32),
        jax.ShapeDtypeStruct((GLA_HEADS, t, LANES), F32),
        jax.ShapeDtypeStruct((2 * GLA_HEADS, t, LANES), F32),
        jax.ShapeDtypeStruct((t, 1024), F32),
        jax.ShapeDtypeStruct((t, S5_WIDTH), F32),
        jax.ShapeDtypeStruct((t, S5_WIDTH), F32),
    )
    weights = (w["w"], w["wg"], w["bg"])
    return pl.pallas_call(
        _proj_even_kernel,
        grid=(t // tm,),
        in_specs=[row(D_MODEL), _const_spec(nw.shape)] + [_const_spec(a.shape) for a in weights],
        out_specs=(slab(4), slab(4), slab(4), slab(8), row(1024), row(S5_WIDTH), row(S5_WIDTH)),
        out_shape=out_shape,
        compiler_params=_params(),
        name="proj_even",
    )(x, nw, *weights)


def _proj_odd_kernel(x_ref, nw_ref, w_ref, alog_ref, dtb_ref, xqkv_ref, z_ref, beta_ref, gd_ref):
    hb = _rms_rows(x_ref[...], nw_ref[...]).astype(BF16)
    d = lambda lo, hi: jnp.dot(hb, w_ref[:, lo:hi], preferred_element_type=F32)
    xqkv_ref[...] = d(0, DN_CONV_DIM)
    z_ref[...] = d(DN_CONV_DIM, DN_CONV_DIM + 1024)
    tail = d(DN_CONV_DIM + 1024, DN_CONV_DIM + 1024 + LANES)
    beta_ref[...] = _sigmoid(tail)
    a = pltpu.roll(tail, LANES - DN_HEADS, 1)
    gd_ref[...] = -jnp.exp(alog_ref[...]) * _softplus(a + dtb_ref[...])


def _proj_odd(x, nw, w):
    t = x.shape[0]
    tm = _row_tile(t)
    row = lambda c: pl.BlockSpec((tm, c), lambda i: (i, 0))
    weights = (w["w"], w["alog"], w["dtb"])
    return pl.pallas_call(
        _proj_odd_kernel,
        grid=(t // tm,),
        in_specs=[row(D_MODEL), _const_spec(nw.shape)] + [_const_spec(a.shape) for a in weights],
        out_specs=(row(DN_CONV_DIM), row(1024), row(LANES), row(LANES)),
        out_shape=(
            jax.ShapeDtypeStruct((t, DN_CONV_DIM), F32),
            jax.ShapeDtypeStruct((t, 1024), F32),
            jax.ShapeDtypeStruct((t, LANES), F32),
            jax.ShapeDtypeStruct((t, LANES), F32),
        ),
        compiler_params=_params(),
        name="proj_odd",
    )(x, nw, *weights)


def _out_even_kernel(x_ref, og_ref, y5_ref, w_ref, o_ref):
    o_ref[...] = x_ref[...] + (_mm(og_ref[...], w_ref[0:1024, :]) + _mm(y5_ref[...], w_ref[1024:, :]))


def _out_even(x, og, y5, w):
    t = x.shape[0]
    tm = _row_tile(t)
    row = lambda c: pl.BlockSpec((tm, c), lambda i: (i, 0))
    return pl.pallas_call(
        _out_even_kernel,
        grid=(t // tm,),
        in_specs=[row(D_MODEL), row(1024), row(S5_WIDTH), _const_spec(w.shape)],
        out_specs=row(D_MODEL),
        out_shape=jax.ShapeDtypeStruct((t, D_MODEL), F32),
        compiler_params=_params(),
        name="out_even",
    )(x, og, y5, w)


def _out_odd_kernel(x_ref, og_ref, w_ref, fw_ref, o_ref, *, final):
    y = x_ref[...] + _mm(og_ref[...], w_ref[...])
    if final:
        y = _rms_rows(y, fw_ref[...])
    o_ref[...] = y


def _out_odd(x, og, w, fw, final):
    t = x.shape[0]
    tm = _row_tile(t)
    row = lambda c: pl.BlockSpec((tm, c), lambda i: (i, 0))
    return pl.pallas_call(
        functools.partial(_out_odd_kernel, final=final),
        grid=(t // tm,),
        in_specs=[row(D_MODEL), row(1024), _const_spec(w.shape), _const_spec(fw.shape)],
        out_specs=row(D_MODEL),
        out_shape=jax.ShapeDtypeStruct((t, D_MODEL), F32),
        compiler_params=_params(),
        name="out_odd",
    )(x, og, w, fw)


def _cumsum_tokens(ref, lead, pad, rows, nb):
    shift = nb
    while shift <= pad:
        ref[lead, pad:pad + rows, :] = ref[lead, pad:pad + rows, :] + ref[lead, pad - shift:pad + rows - shift, :]
        shift *= 2
    while shift < rows:
        ref[lead, pad + shift:pad + rows, :] = (ref[lead, pad + shift:pad + rows, :]
                                                + ref[lead, pad:pad + rows - shift, :])
        shift *= 2


def _gla_prompt_kernel(q_ref, k_ref, g_ref, v_ref, r_ref, far_ref, nw_ref, og_ref, sout_ref,
                       kp, bp, vp, st, oscr, *, nb):
    rows = CHUNK * nb
    pad = PAD_TOKENS * nb
    blk_rows = GLA_BAND * nb
    step = pl.program_id(0)

    @pl.when(step == 0)
    def _():
        st[...] = jnp.zeros_like(st)
        kp[:, 0:pad, :] = jnp.zeros((GLA_HEADS, pad, LANES), F32)
        bp[:, 0:pad, :] = jnp.zeros((GLA_HEADS, pad, LANES), F32)
        vp[:, 0:pad, :] = jnp.zeros((2 * GLA_HEADS, pad, LANES), F32)

    for h in range(GLA_HEADS):
        bp[h, pad:pad + rows, :] = g_ref[h]
        kp[h, pad:pad + rows, :] = k_ref[h]
        _cumsum_tokens(bp, h, pad, rows, nb)
    for s in range(2 * GLA_HEADS):
        vp[s, pad:pad + rows, :] = v_ref[s]

    def band_tile(ti, carry):
        r0 = pl.multiple_of(ti * 64, 64)
        for h in range(GLA_HEADS):
            qt = q_ref[h, pl.ds(r0, 64), :]
            bt = bp[h, pl.ds(pad + r0, 64), :]
            acc0 = jnp.zeros((64, LANES), F32)
            acc1 = jnp.zeros((64, LANES), F32)
            for d in range(GLA_BAND):
                off = pl.multiple_of(pad + r0 - d * nb, 8)
                ks = kp[h, pl.ds(off, 64), :]
                bs = bp[h, pl.ds(off, 64), :]
                w = jnp.sum(qt * ks * jnp.exp(bt - bs), axis=-1, keepdims=True)
                acc0 = acc0 + w * vp[2 * h, pl.ds(off, 64), :]
                acc1 = acc1 + w * vp[2 * h + 1, pl.ds(off, 64), :]
            oscr[2 * h, pl.ds(r0, 64), :] = acc0
            oscr[2 * h + 1, pl.ds(r0, 64), :] = acc1
        return carry

    lax.fori_loop(0, rows // 64, band_tile, 0)

    nblk = CHUNK // GLA_BAND
    for h in range(GLA_HEADS):
        probs = []
        for blk in range(1, nblk):
            lo = blk * blk_rows
            ref_b = bp[h, pad + lo - nb:pad + lo, :]
            qs = q_ref[h, lo:lo + blk_rows, :] * jnp.exp(
                bp[h, pad + lo:pad + lo + blk_rows, :] - jnp.concatenate([ref_b] * GLA_BAND, axis=0))
            ks = kp[h, pad:pad + lo, :] * jnp.exp(
                jnp.concatenate([ref_b] * (GLA_BAND * blk), axis=0) - bp[h, pad:pad + lo, :])
            probs.append(_mm_nt(qs, ks) * far_ref[lo:lo + blk_rows, 0:lo])
        for blk in range(1, nblk):
            lo = blk * blk_rows
            p = probs[blk - 1].astype(BF16)
            for half in range(2):
                sl = 2 * h + half
                oscr[sl, lo:lo + blk_rows, :] = oscr[sl, lo:lo + blk_rows, :] + jnp.dot(
                    p, vp[sl, pad:pad + lo, :].astype(BF16), preferred_element_type=F32)

    def per_batch(b, carry):
        sel = pl.ds(b, CHUNK, stride=nb)
        selp = pl.ds(pad + b, CHUNK, stride=nb)
        loaded = [(q_ref[h, sel, :], kp[h, selp, :], bp[h, selp, :], vp[2 * h, selp, :], vp[2 * h + 1, selp, :],
                   st[b, h]) for h in range(GLA_HEADS)]
        outs = [_mm_nt(qb * jnp.exp(bb), stt) for qb, _, bb, _, _, stt in loaded]
        news = []
        for _, kb, bb, v0, v1, stt in loaded:
            blast = bb[CHUNK - 1:CHUNK, :]
            kd = kb * jnp.exp(blast - bb)
            news.append(stt * jnp.exp(blast) + _mm_tn(jnp.concatenate([v0, v1], axis=1), kd))
        for h in range(GLA_HEADS):
            oscr[2 * h, sel, :] = oscr[2 * h, sel, :] + outs[h][:, :LANES]
            oscr[2 * h + 1, sel, :] = oscr[2 * h + 1, sel, :] + outs[h][:, LANES:]
            st[b, h] = news[h]
        return carry

    lax.fori_loop(0, nb, per_batch, 0)

    nw = nw_ref[...]

    def epi_tile(ti, carry):
        r0 = pl.multiple_of(ti * 64, 64)
        for h in range(GLA_HEADS):
            o0 = oscr[2 * h, pl.ds(r0, 64), :]
            o1 = oscr[2 * h + 1, pl.ds(r0, 64), :]
            ms = (jnp.sum(o0 * o0, axis=-1, keepdims=True) + jnp.sum(o1 * o1, axis=-1, keepdims=True)) / GLA_DV
            inv = lax.rsqrt(ms + EPS)
            c0 = h * GLA_DV
            og_ref[pl.ds(r0, 64), c0:c0 + LANES] = o0 * inv * nw[:, :LANES] * _silu(r_ref[pl.ds(r0, 64), c0:c0 + LANES])
            og_ref[pl.ds(r0, 64), c0 + LANES:c0 + 2 * LANES] = (
                o1 * inv * nw[:, LANES:] * _silu(r_ref[pl.ds(r0, 64), c0 + LANES:c0 + 2 * LANES]))
        return carry

    lax.fori_loop(0, rows // 64, epi_tile, 0)

    @pl.when(step == pl.num_programs(0) - 1)
    def _():
        def wr(b, carry):
            for h in range(GLA_HEADS):
                sout_ref[b, h] = st[b, h].T
            return carry
        lax.fori_loop(0, nb, wr, 0)


def _pair_mask(nb, min_dist):
    n = CHUNK * nb
    r = jnp.arange(n)
    same = (r[:, None] % nb) == (r[None, :] % nb)
    far = (r[None, :] // nb) <= (r[:, None] // nb) - min_dist
    return (same & far).astype(F32)


def _gla_prompt(q, k, g, v, r, nw, nb):
    t = r.shape[0]
    rows = CHUNK * nb
    pad = PAD_TOKENS * nb
    slab = lambda n: pl.BlockSpec((n, rows, LANES), lambda i: (0, i, 0))
    row = lambda c: pl.BlockSpec((rows, c), lambda i: (i, 0))
    far = _pair_mask(nb, GLA_BAND)
    return pl.pallas_call(
        functools.partial(_gla_prompt_kernel, nb=nb),
        grid=(t // rows,),
        in_specs=[slab(4), slab(4), slab(4), slab(8), row(1024), _const_spec(far.shape), _const_spec(nw.shape)],
        out_specs=(row(1024), _const_spec((nb, GLA_HEADS, GLA_DK, GLA_DV))),
        out_shape=(jax.ShapeDtypeStruct((t, 1024), F32),
                   jax.ShapeDtypeStruct((nb, GLA_HEADS, GLA_DK, GLA_DV), F32)),
        scratch_shapes=[
            pltpu.VMEM((GLA_HEADS, pad + rows, LANES), F32),
            pltpu.VMEM((GLA_HEADS, pad + rows, LANES), F32),
            pltpu.VMEM((2 * GLA_HEADS, pad + rows, LANES), F32),
            pltpu.VMEM((nb, GLA_HEADS, GLA_DV, GLA_DK), F32),
            pltpu.VMEM((2 * GLA_HEADS, rows, LANES), F32),
        ],
        compiler_params=_params(),
        name="gla_prompt",
    )(q, k, g, v, r, far, nw)


def _gla_step_kernel(q_ref, k_ref, g_ref, v_ref, r_ref, s_ref, nw_ref, og_ref, sout_ref):
    nw = nw_ref[...]
    zeros = jnp.zeros((LANES - 24, LANES), F32)
    for h in range(GLA_HEADS):
        stack = jnp.concatenate([jnp.exp(g_ref[h]), k_ref[h], q_ref[h], zeros], axis=0)
        cols = stack.T
        outs = []
        for n in range(8):
            s_old = s_ref[n, h]
            vrow = jnp.concatenate([v_ref[2 * h, n:n + 1, :], v_ref[2 * h + 1, n:n + 1, :]], axis=1)
            s_new = s_old * cols[:, n:n + 1] + cols[:, 8 + n:9 + n] * vrow
            sout_ref[n, h] = s_new
            outs.append(jnp.sum(cols[:, 16 + n:17 + n] * s_new, axis=0, keepdims=True))
        o = jnp.concatenate(outs, axis=0)
        ms = jnp.mean(o * o, axis=-1, keepdims=True)
        c0 = h * GLA_DV
        og_ref[:, c0:c0 + GLA_DV] = o * lax.rsqrt(ms + EPS) * nw * _silu(r_ref[:, c0:c0 + GLA_DV])


def _gla_step(q, k, g, v, r, s, nw):
    n = r.shape[0]
    slab = lambda c: pl.BlockSpec((c, 8, LANES), lambda i: (0, i, 0))
    sspec = pl.BlockSpec((8, GLA_HEADS, GLA_DK, GLA_DV), lambda i: (i, 0, 0, 0))
    return pl.pallas_call(
        _gla_step_kernel,
        grid=(n // 8,),
        in_specs=[slab(4), slab(4), slab(4), slab(8), pl.BlockSpec((8, 1024), lambda i: (i, 0)), sspec,
                  _const_spec(nw.shape)],
        out_specs=(pl.BlockSpec((8, 1024), lambda i: (i, 0)), sspec),
        out_shape=(jax.ShapeDtypeStruct((n, 1024), F32), jax.ShapeDtypeStruct(s.shape, F32)),
        compiler_params=_params(),
        name="gla_step",
    )(q, k, g, v, r, s, nw)


def _s5_prep_kernel(lre_ref, lim_ref, ldt_ref, lre_r_ref, lim_r_ref, ldt_r_ref, bre_ref, bim_ref,
                    ar_ref, ai_ref, bbre_ref, bbim_ref):
    def disc(lre, lim, ldt):
        dt = jnp.exp(ldt)
        mag = jnp.exp(lre * dt)
        ar = mag * jnp.cos(lim * dt)
        ai = mag * jnp.sin(lim * dt)
        return ar, ai

    ar, ai = disc(lre_ref[...], lim_ref[...], ldt_ref[...])
    ar_ref[...] = ar
    ai_ref[...] = ai
    lre, lim = lre_r_ref[...], lim_r_ref[...]
    ar, ai = disc(lre, lim, ldt_r_ref[...])
    den = lre * lre + lim * lim
    wr = ((ar - 1.0) * lre + ai * lim) / den
    wi = (ai * lre - (ar - 1.0) * lim) / den
    bre, bim = bre_ref[...], bim_ref[...]
    bbre_ref[...] = wr * bre - wi * bim
    bbim_ref[...] = wr * bim + wi * bre


def _s5_prep(lam_re, lam_im, log_dt, b_re, b_im):
    n = lam_re.shape[0]
    rows = n * S5_GROUPS
    lre = lam_re.reshape(rows, S5_P)
    lim = lam_im.reshape(rows, S5_P)
    ldt = jnp.broadcast_to(log_dt.reshape(rows, 1), (rows, S5_P))
    rep = lambda a: jnp.repeat(a, S5_GROUP_CH, axis=1)
    args = (lre, lim, ldt, rep(lre), rep(lim), rep(ldt),
            b_re.reshape(rows, S5_P * S5_GROUP_CH), b_im.reshape(rows, S5_P * S5_GROUP_CH))
    wide = jax.ShapeDtypeStruct((rows, S5_P * S5_GROUP_CH), F32)
    narrow = jax.ShapeDtypeStruct((rows, S5_P), F32)
    ar, ai, bbre, bbim = pl.pallas_call(
        _s5_prep_kernel, out_shape=(narrow, narrow, wide, wide), name="s5_prep")(*args)
    shp = (n, S5_GROUPS, S5_P, S5_GROUP_CH)
    return ar.reshape(n, S5_GROUPS, S5_P), ai.reshape(n, S5_GROUPS, S5_P), bbre.reshape(shp), bbim.reshape(shp)


def _s5_kernel(u_ref, sg_ref, h0re_ref, h0im_ref, wbre_ref, wbim_ref, wc_ref, are_ref, aim_ref, d_ref,
               wglu_ref, bglu_ref, y_ref, hre_out, him_out, hre, him, cre, cim, ys, *, nb, tokens):
    step = pl.program_id(0)
    rows = nb * tokens

    @pl.when(step == 0)
    def _():
        cre[...] = h0re_ref[...]
        cim[...] = h0im_ref[...]

    for m in range(4):
        um = u_ref[:, m * LANES:(m + 1) * LANES].astype(BF16)
        hre[:, m * 512:(m + 1) * 512] = jnp.dot(um, wbre_ref[m], preferred_element_type=F32)
        him[:, m * 512:(m + 1) * 512] = jnp.dot(um, wbim_ref[m], preferred_element_type=F32)

    for qd in range(4):
        ql = slice(qd * 512, (qd + 1) * 512)
        ar = jnp.broadcast_to(are_ref[:, ql], (nb, 512))
        ai = jnp.broadcast_to(aim_ref[:, ql], (nb, 512))

        def tok(t, carry, ql=ql, ar=ar, ai=ai):
            hr, hi = carry
            sel = pl.ds(pl.multiple_of(t * nb, nb), nb)
            nr = ar * hr - ai * hi + hre[sel, ql]
            ni = ar * hi + ai * hr + him[sel, ql]
            hre[sel, ql] = nr
            him[sel, ql] = ni
            return nr, ni

        hr, hi = lax.fori_loop(0, tokens, tok, (cre[:, ql], cim[:, ql]))
        cre[:, ql] = hr
        cim[:, ql] = hi

    for m in range(4):
        hc = jnp.concatenate([hre[:, m * 512:(m + 1) * 512], him[:, m * 512:(m + 1) * 512]], axis=1)
        sl = slice(m * LANES, (m + 1) * LANES)
        y = _mm(hc, wc_ref[m]) + d_ref[:, sl] * u_ref[:, sl]
        ys[:, sl] = jax.nn.gelu(y)
    y = ys[...]
    gate = _sigmoid(_mm(y, wglu_ref[...]) + bglu_ref[...])
    y_ref[...] = y * gate * _silu(sg_ref[...])

    @pl.when(step == pl.num_programs(0) - 1)
    def _():
        hre_out[...] = cre[...]
        him_out[...] = cim[...]


def _s5(u, sg, h0re, h0im, w, nb, tokens):
    t = u.shape[0]
    rows = nb * tokens
    row = lambda c: pl.BlockSpec((rows, c), lambda i: (i, 0))
    weights = (w["wbre"], w["wbim"], w["wc"], w["are"], w["aim"], w["d"], w["wglu"], w["bglu"])
    state = jax.ShapeDtypeStruct((nb, S5_STATE), F32)
    return pl.pallas_call(
        functools.partial(_s5_kernel, nb=nb, tokens=tokens),
        grid=(t // rows,),
        in_specs=[row(S5_WIDTH), row(S5_WIDTH), _const_spec(h0re.shape), _const_spec(h0im.shape)]
        + [_const_spec(a.shape) for a in weights],
        out_specs=(row(S5_WIDTH), _const_spec((nb, S5_STATE)), _const_spec((nb, S5_STATE))),
        out_shape=(jax.ShapeDtypeStruct((t, S5_WIDTH), F32), state, state),
        scratch_shapes=[
            pltpu.VMEM((rows, S5_STATE), F32),
            pltpu.VMEM((rows, S5_STATE), F32),
            pltpu.VMEM((nb, S5_STATE), F32),
            pltpu.VMEM((nb, S5_STATE), F32),
            pltpu.VMEM((rows, S5_WIDTH), F32),
        ],
        compiler_params=_params(),
        name="s5",
    )(u, sg, h0re, h0im, *weights)


def _dn_solve(lms, rhss):
    n = len(lms)
    xs = [[] for _ in range(n)]
    for blk in range(CHUNK // SUB):
        lo, hi = SUB * blk, SUB * (blk + 1)
        curs = [rhs[lo:hi] for rhs in rhss]
        if blk:
            curs = [cur - _mm(lm[lo:hi, 0:lo], jnp.concatenate(x, axis=0))
                    for cur, lm, x in zip(curs, lms, xs)]
        ldiags = [lm[lo:hi, lo:hi] for lm in lms]
        for j in range(SUB - 1):
            curs = [cur - ld[:, j:j + 1] * cur[j:j + 1, :] for cur, ld in zip(curs, ldiags)]
        for x, cur in zip(xs, curs):
            x.append(cur)
    return [jnp.concatenate(x, axis=0) for x in xs]


def _dn_prompt_kernel(x_ref, z_ref, beta_ref, gd_ref, cw_ref, nw_ref, c0_ref,
                      og_ref, sout_ref, cout_ref,
                      xs, qs, ks, vs, bcol, gcol, gc, gt, oscr, *, nb):
    rows = CHUNK * nb
    hist = (CONV_W - 1) * nb
    pad = PAD_TOKENS * nb
    step = pl.program_id(0)
    st = sout_ref

    @pl.when(step == 0)
    def _():
        st[...] = jnp.zeros_like(st)
        xs[0:hist, :] = c0_ref[...]
        gc[0, 0:pad, :] = jnp.zeros((pad, LANES), F32)

    xs[hist:hist + rows, :] = x_ref[...]

    def conv_tile(ti, carry):
        r0 = pl.multiple_of(ti * 64, 64)
        for c in range(DN_CONV_DIM // LANES):
            cl = slice(c * LANES, (c + 1) * LANES)
            y = xs[pl.ds(r0, 64), cl] * cw_ref[0:1, cl]
            for i in range(1, CONV_W):
                y = y + xs[pl.ds(r0 + i * nb, 64), cl] * cw_ref[i:i + 1, cl]
            y = _silu(y)
            if c < 2 * DN_HEADS:
                y = y * lax.rsqrt(jnp.sum(y * y, axis=-1, keepdims=True) + EPS)
            if c < DN_HEADS:
                qs[c, pl.ds(r0, 64), :] = y * (DN_DK ** -0.5)
            elif c < 2 * DN_HEADS:
                ks[c - DN_HEADS, pl.ds(r0, 64), :] = y
            else:
                vs[c - 2 * DN_HEADS, pl.ds(r0, 64), :] = y
        return carry

    lax.fori_loop(0, rows // 64, conv_tile, 0)

    xs[0:hist, :] = xs[rows:rows + hist, :]

    gc[0, pad:pad + rows, :] = gd_ref[...]
    _cumsum_tokens(gc, 0, pad, rows, nb)

    def bc_tile(ti, carry):
        r0 = pl.multiple_of(ti * 64, 64)
        gtile = gc[0, pl.ds(pad + r0, 64), :]
        btile = beta_ref[pl.ds(r0, 64), :]
        for h in range(DN_HEADS):
            gcol[h, pl.ds(r0, 64), :] = jnp.broadcast_to(gtile[:, h:h + 1], (64, LANES))
            bcol[h, pl.ds(r0, 64), :] = jnp.broadcast_to(btile[:, h:h + 1], (64, LANES))
        return carry

    lax.fori_loop(0, rows // 64, bc_tile, 0)

    ii = lax.broadcasted_iota(jnp.int32, (CHUNK, CHUNK), 0)
    jj = lax.broadcasted_iota(jnp.int32, (CHUNK, CHUNK), 1)
    eye = (ii == jj).astype(BF16)

    parts = [_split3(gc[0, pl.ds(pad + b, CHUNK, stride=nb), :]) for b in range(nb)]
    tdot = lambda x: lax.dot_general(x, eye, (((0,), (0,)), ((), ())), preferred_element_type=F32)
    firsts = [tdot(p[0]) for p in parts]
    seconds = [tdot(p[1]) for p in parts]
    thirds = [tdot(p[2]) for p in parts]
    for b in range(nb):
        gt[b] = (firsts[b] + seconds[b]) + thirds[b]

    def per_batch(bi, carry):
        def head_group(gi, carry2):
            probs = [(bi * DN_BATCH + jb, gi * DN_GROUP + jh) for jb in range(DN_BATCH) for jh in range(DN_GROUP)]
            sels = [pl.ds(b, CHUNK, stride=nb) for b, _ in probs]
            loaded = [(qs[h, sel, :], ks[h, sel, :], vs[h, sel, :], bcol[h, sel, :], gcol[h, sel, :],
                       gt[b, pl.ds(h, 1), :], st[b, h]) for (b, h), sel in zip(probs, sels)]
            qbs, kbs, vbs, bcs, gcls, grows, s_olds = zip(*loaded)
            decs = [jnp.where(jj <= ii, jnp.exp(jnp.where(jj <= ii, gcl[:, :CHUNK] - grow, 0.0)), 0.0)
                    for gcl, grow in zip(gcls, grows)]
            kks = [_mm_nt(kb, kb) for kb in kbs]
            qks = [_mm_nt(qb, kb) for qb, kb in zip(qbs, kbs)]
            lms = [jnp.where(jj < ii, bc[:, :CHUNK] * kk * dec, 0.0) for bc, kk, dec in zip(bcs, kks, decs)]
            egs = [jnp.exp(gcl) for gcl in gcls]
            rhss = [jnp.concatenate([vb * bc, kb * (bc * eg)], axis=1)
                    for vb, kb, bc, eg in zip(vbs, kbs, bcs, egs)]
            sols = _dn_solve(lms, rhss)
            v_news = [sol[:, :DN_DV] - _mm(sol[:, DN_DV:], s_old) for sol, s_old in zip(sols, s_olds)]
            os_ = [_mm(qb * eg, s_old) + _mm(qk * dec, v_new)
                   for qb, eg, s_old, qk, dec, v_new in zip(qbs, egs, s_olds, qks, decs, v_news)]
            glasts = [gcl[CHUNK - 1:CHUNK, :] for gcl in gcls]
            s_news = [s_old * jnp.exp(glast) + _mm_tn(kb * jnp.exp(glast - gcl), v_new)
                      for s_old, glast, kb, gcl, v_new in zip(s_olds, glasts, kbs, gcls, v_news)]
            for (b, h), sel, o, s_new in zip(probs, sels, os_, s_news):
                st[b, h] = s_new
                oscr[h, sel, :] = o
            return carry2

        lax.fori_loop(0, DN_HEADS // DN_GROUP, head_group, 0)
        return carry

    lax.fori_loop(0, nb // DN_BATCH, per_batch, 0)

    nw = nw_ref[...]

    def epi_tile(ti, carry):
        r0 = pl.multiple_of(ti * 64, 64)
        for h in range(DN_HEADS):
            o = oscr[h, pl.ds(r0, 64), :]
            inv = lax.rsqrt(jnp.mean(o * o, axis=-1, keepdims=True) + EPS)
            cl = slice(h * LANES, (h + 1) * LANES)
            og_ref[pl.ds(r0, 64), cl] = o * inv * nw * _silu(z_ref[pl.ds(r0, 64), cl])
        return carry

    lax.fori_loop(0, rows // 64, epi_tile, 0)

    @pl.when(step == pl.num_programs(0) - 1)
    def _():
        cout_ref[...] = xs[0:hist, :]


def _dn_prompt(xqkv, z, beta, gd, cw, nw, c0, nb):
    t = z.shape[0]
    rows = CHUNK * nb
    hist = (CONV_W - 1) * nb
    row = lambda c: pl.BlockSpec((rows, c), lambda i: (i, 0))
    slab = pltpu.VMEM((DN_HEADS, rows, LANES), F32)
    return pl.pallas_call(
        functools.partial(_dn_prompt_kernel, nb=nb),
        grid=(t // rows,),
        in_specs=[row(DN_CONV_DIM), row(1024), row(LANES), row(LANES), _const_spec(cw.shape),
                  _const_spec(nw.shape), _const_spec(c0.shape)],
        out_specs=(row(1024), _const_spec((nb, DN_HEADS, DN_DK, DN_DV)), _const_spec((hist, DN_CONV_DIM))),
        out_shape=(jax.ShapeDtypeStruct((t, 1024), F32),
                   jax.ShapeDtypeStruct((nb, DN_HEADS, DN_DK, DN_DV), F32),
                   jax.ShapeDtypeStruct((hist, DN_CONV_DIM), F32)),
        scratch_shapes=[
            pltpu.VMEM((hist + rows, DN_CONV_DIM), F32),
            slab, slab, slab, slab, slab,
            pltpu.VMEM((1, PAD_TOKENS * nb + rows, LANES), F32),
            pltpu.VMEM((nb, LANES, CHUNK), F32),
            slab,
        ],
        compiler_params=_params(),
        name="dn_prompt",
    )(xqkv, z, beta, gd, cw, nw, c0)


def _dn_step_kernel(x_ref, cb_ref, z_ref, beta_ref, gd_ref, s_ref, cw_ref, nw_ref, og_ref, sout_ref, cn_ref):
    nw = nw_ref[...]
    cn_ref[0] = cb_ref[1]
    cn_ref[1] = cb_ref[2]
    cn_ref[2] = x_ref[...]
    zeros = jnp.zeros((LANES - 16, LANES), F32)
    beta = beta_ref[...]
    eg_all = jnp.exp(gd_ref[...])

    def conv(c):
        cl = slice(c * LANES, (c + 1) * LANES)
        y = cb_ref[0, :, cl] * cw_ref[0:1, cl]
        y = y + cb_ref[1, :, cl] * cw_ref[1:2, cl]
        y = y + cb_ref[2, :, cl] * cw_ref[2:3, cl]
        y = y + x_ref[:, cl] * cw_ref[3:4, cl]
        return _silu(y)

    def l2(y):
        return y * lax.rsqrt(jnp.sum(y * y, axis=-1, keepdims=True) + EPS)

    for h in range(DN_HEADS):
        qh = l2(conv(h)) * (DN_DK ** -0.5)
        kh = l2(conv(DN_HEADS + h))
        vh = conv(2 * DN_HEADS + h)
        cols = jnp.concatenate([kh, qh, zeros], axis=0).T
        outs = []
        for n in range(8):
            s_old = s_ref[n, h]
            kcol = cols[:, n:n + 1]
            eg = eg_all[n:n + 1, h:h + 1]
            ks_row = jnp.sum(kcol * s_old, axis=0, keepdims=True)
            v_new = beta[n:n + 1, h:h + 1] * (vh[n:n + 1, :] - eg * ks_row)
            s_new = s_old * eg + kcol * v_new
            sout_ref[n, h] = s_new
            outs.append(jnp.sum(cols[:, 8 + n:9 + n] * s_new, axis=0, keepdims=True))
        o = jnp.concatenate(outs, axis=0)
        cl = slice(h * LANES, (h + 1) * LANES)
        og_ref[:, cl] = o * lax.rsqrt(jnp.mean(o * o, axis=-1, keepdims=True) + EPS) * nw * _silu(z_ref[:, cl])


def _dn_step(xqkv, cbuf, z, beta, gd, s, cw, nw):
    n = z.shape[0]
    row = lambda c: pl.BlockSpec((8, c), lambda i: (i, 0))
    cspec = pl.BlockSpec((CONV_W - 1, 8, DN_CONV_DIM), lambda i: (0, i, 0))
    sspec = pl.BlockSpec((8, DN_HEADS, DN_DK, DN_DV), lambda i: (i, 0, 0, 0))
    return pl.pallas_call(
        _dn_step_kernel,
        grid=(n // 8,),
        in_specs=[row(DN_CONV_DIM), cspec, row(1024), row(LANES), row(LANES), sspec,
                  _const_spec(cw.shape), _const_spec(nw.shape)],
        out_specs=(row(1024), sspec, cspec),
        out_shape=(jax.ShapeDtypeStruct((n, 1024), F32), jax.ShapeDtypeStruct(s.shape, F32),
                   jax.ShapeDtypeStruct(cbuf.shape, F32)),
        compiler_params=_params(),
        name="dn_step",
    )(xqkv, cbuf, z, beta, gd, s, cw, nw)


def _pad_cols(a, n):
    return jnp.pad(a, ((0, 0), (0, n - a.shape[1])))


def _even_weights(w_in, w_gate_up, b_gate, w_out):
    return {
        "w": _pad_cols(w_in.astype(BF16), W_IN_PAD),
        "wg": jnp.pad(w_gate_up, ((0, LANES - GLA_RANK), (0, 0))).astype(BF16),
        "bg": b_gate.reshape(1, -1),
        "wo": w_out.astype(BF16),
    }


def _s5_weights(ar, ai, bbre, bbim, c_re, c_im, d, w_glu, b_glu):
    eye = jnp.eye(8, dtype=F32)

    def pack_b(bb):
        t = bb.reshape(4, 8, S5_P, S5_GROUP_CH).transpose(0, 1, 3, 2)
        return (t[:, :, :, None, :] * eye[None, :, None, :, None]).reshape(4, 128, 512).astype(BF16)

    def pack_c(c):
        t = c.reshape(4, 8, S5_GROUP_CH, S5_P).transpose(0, 1, 3, 2)
        return (t[:, :, :, None, :] * eye[None, :, None, :, None]).reshape(4, 512, 128)

    return {
        "wbre": pack_b(bbre), "wbim": pack_b(bbim),
        "wc": jnp.concatenate([pack_c(c_re), -pack_c(c_im)], axis=1).astype(BF16),
        "are": ar.reshape(1, S5_STATE), "aim": ai.reshape(1, S5_STATE),
        "d": d.reshape(1, -1), "wglu": w_glu.astype(BF16), "bglu": b_glu.reshape(1, -1),
    }


def _odd_weights(w_in, conv_w, a_log, dt_bias, w_out):
    return {
        "w": _pad_cols(w_in.astype(BF16), W_IN_PAD),
        "alog": _pad_cols(a_log.reshape(1, -1), LANES), "dtb": _pad_cols(dt_bias.reshape(1, -1), LANES),
        "cw": jnp.pad(conv_w, ((0, 8 - CONV_W), (0, 0))),
        "wo": w_out.astype(BF16),
    }


def kernel(x_prompt, x_sample, state_gla, state_s5_re, state_s5_im, state_delta, state_conv, norm_w, final_norm_w, w_in_even, gla_w_gate_up, gla_b_gate, gla_norm_w, s5_lambda_re, s5_lambda_im, s5_log_dt, s5_b_re, s5_b_im, s5_c_re, s5_c_im, s5_d, s5_w_glu, s5_b_glu, w_out_even, w_in_odd, dn_conv_w, dn_a_log, dn_dt_bias, dn_norm_w, w_out_odd):
    nb, seq, _ = x_prompt.shape
    ns = x_sample.shape[0]
    assert seq % CHUNK == 0 and nb % 8 == 0 and ns % 8 == 0 and x_sample.shape[1] == 1
    depth = norm_w.shape[0]

    xp = x_prompt.transpose(1, 0, 2).reshape(seq * nb, D_MODEL)
    xs = x_sample.reshape(ns, D_MODEL)
    fw = final_norm_w.reshape(1, -1)

    ar, ai, bbre, bbim = _s5_prep(s5_lambda_re, s5_lambda_im, s5_log_dt, s5_b_re, s5_b_im)

    gla_p, gla_s, s5r_p, s5i_p, s5r_s, s5i_s, dn_p, dn_s, cv_p, cv_s = ([] for _ in range(10))
    for layer in range(depth):
        i = layer // 2
        nw = norm_w[layer].reshape(1, -1)
        if layer % 2 == 0:
            w = _even_weights(w_in_even[i], gla_w_gate_up[i], gla_b_gate[i], w_out_even[i])
            w5 = _s5_weights(ar[i], ai[i], bbre[i], bbim[i], s5_c_re[i], s5_c_im[i], s5_d[i],
                             s5_w_glu[i], s5_b_glu[i])
            gnw = gla_norm_w[i].reshape(1, -1)
            q, k, g, v, r, u, sg = _proj_even(xp, nw, w)
            og, s_new = _gla_prompt(q, k, g, v, r, gnw, nb)
            zero = jnp.zeros((nb, S5_STATE), F32)
            y5, hre, him = _s5(u, sg, zero, zero, w5, nb, CHUNK)
            xp = _out_even(xp, og, y5, w["wo"])
            gla_p.append(s_new)
            s5r_p.append(hre.reshape(nb, S5_GROUPS, S5_P))
            s5i_p.append(him.reshape(nb, S5_GROUPS, S5_P))
            q, k, g, v, r, u, sg = _proj_even(xs, nw, w)
            og, s_new = _gla_step(q, k, g, v, r, state_gla[i], gnw)
            y5, hre, him = _s5(u, sg, state_s5_re[i].reshape(ns, S5_STATE), state_s5_im[i].reshape(ns, S5_STATE),
                               w5, ns, 1)
            xs = _out_even(xs, og, y5, w["wo"])
            gla_s.append(s_new)
            s5r_s.append(hre.reshape(ns, S5_GROUPS, S5_P))
            s5i_s.append(him.reshape(ns, S5_GROUPS, S5_P))
        else:
            w = _odd_weights(w_in_odd[i], dn_conv_w[i], dn_a_log[i], dn_dt_bias[i], w_out_odd[i])
            dnw = dn_norm_w[i].reshape(1, -1)
            final = layer == depth - 1
            xqkv, z, beta, gd = _proj_odd(xp, nw, w)
            c0 = jnp.zeros(((CONV_W - 1) * nb, DN_CONV_DIM), F32)
            og, s_new, c_new = _dn_prompt(xqkv, z, beta, gd, w["cw"], dnw, c0, nb)
            xp = _out_odd(xp, og, w["wo"], fw, final)
            dn_p.append(s_new)
            cv_p.append(c_new.reshape(CONV_W - 1, nb, DN_CONV_DIM).transpose(1, 0, 2))
            xqkv, z, beta, gd = _proj_odd(xs, nw, w)
            og, s_new, c_new = _dn_step(xqkv, state_conv[i].transpose(1, 0, 2), z, beta, gd, state_delta[i],
                                        w["cw"], dnw)
            xs = _out_odd(xs, og, w["wo"], fw, final)
            dn_s.append(s_new)
            cv_s.append(c_new.transpose(1, 0, 2))

    y_prompt = xp.reshape(seq, nb, D_MODEL).transpose(1, 0, 2)
    y_sample = xs.reshape(ns, 1, D_MODEL)
    st = jnp.stack
    return (y_prompt, y_sample, st(gla_p), st(gla_s), st(s5r_p), st(s5i_p), st(s5r_s), st(s5i_s),
            st(dn_p), st(dn_s), st(cv_p), st(cv_s))
```

```python
import functools
import math

import jax
import jax.numpy as jnp
from jax import lax
from jax.experimental import pallas as pl
from jax.experimental.pallas import tpu as pltpu

F32 = jnp.float32
BF16 = jnp.bfloat16
EPS = 1e-6

D_MODEL = 1024
GLA_HEADS, GLA_DK, GLA_DV, GLA_RANK = 4, 128, 256, 16
GLA_GATE_NORM = 16.0
S5_GROUPS, S5_GROUP_CH, S5_P, S5_WIDTH = 32, 16, 64, 512
S5_STATE = S5_GROUPS * S5_P
DN_HEADS, DN_DK, DN_DV = 8, 128, 128
DN_CONV_DIM, CONV_W = 3072, 4
CHUNK = 64
SUB = 8
GLA_BAND = 8
PAD_TOKENS = 16
DN_BATCH = 4
DN_GROUP = 8
LANES = 128
W_IN_PAD = 4224
VMEM_LIMIT = 56 * 1024 * 1024


def _mm(a, b):
    return jnp.dot(a.astype(BF16), b.astype(BF16), preferred_element_type=F32)


def _mm_nt(a, b):
    return lax.dot_general(a.astype(BF16), b.astype(BF16), (((1,), (1,)), ((), ())),
                           preferred_element_type=F32)


def _mm_tn(a, b):
    return lax.dot_general(a.astype(BF16), b.astype(BF16), (((0,), (0,)), ((), ())),
                           preferred_element_type=F32)


def _split3(a):
    a1 = a.astype(BF16)
    r1 = a - a1.astype(F32)
    a2 = r1.astype(BF16)
    a3 = (r1 - a2.astype(F32)).astype(BF16)
    return a1, a2, a3


def _mm_tn_exact(a, m_bf16):
    a1, a2, a3 = _split3(a)
    d = lambda x: lax.dot_general(x, m_bf16, (((0,), (0,)), ((), ())), preferred_element_type=F32)
    return (d(a1) + d(a2)) + d(a3)


def _sigmoid(x):
    return 1.0 / (1.0 + jnp.exp(-x))


def _silu(x):
    return x * _sigmoid(x)


def _softplus(x):
    return jnp.maximum(x, 0.0) + jnp.log1p(jnp.exp(-jnp.abs(x)))


def _rms_rows(x, w):
    ms = jnp.mean(x * x, axis=-1, keepdims=True)
    return x * lax.rsqrt(ms + EPS) * w


def _const_spec(shape):
    nd = len(shape)
    return pl.BlockSpec(shape, lambda i, _nd=nd: (0,) * _nd)


def _params(**flags):
    return pltpu.CompilerParams(dimension_semantics=("arbitrary",), vmem_limit_bytes=VMEM_LIMIT,
                                flags=flags or None)


def _row_tile(t):
    return min(512, t)


def _proj_even_kernel(x_ref, nw_ref, w_ref, wg_ref, bg_ref, q_ref, k_ref, g_ref, v_ref, r_ref, u_ref, sg_ref):
    hb = _rms_rows(x_ref[...], nw_ref[...]).astype(BF16)
    d = lambda lo, hi: jnp.dot(hb, w_ref[:, lo:hi], preferred_element_type=F32)
    q = d(0, 512) * (GLA_DK ** -0.5)
    k = d(512, 1024)
    lr_u = d(3072, 3712)
    u_sg = d(3584, 4224)
    logit = jnp.dot(lr_u[:, :LANES].astype(BF16), wg_ref[...], preferred_element_type=F32) + bg_ref[...]
    g = -_softplus(-logit) / GLA_GATE_NORM
    for h in range(GLA_HEADS):
        sl = slice(h * LANES, (h + 1) * LANES)
        q_ref[h] = q[:, sl]
        k_ref[h] = k[:, sl]
        g_ref[h] = g[:, sl]
    v = d(1024, 2048)
    for s in range(2 * GLA_HEADS):
        v_ref[s] = v[:, s * LANES:(s + 1) * LANES]
    r_ref[...] = d(2048, 3072)
    u_ref[...] = lr_u[:, GLA_RANK:GLA_RANK + S5_WIDTH]
    sg_ref[...] = u_sg[:, GLA_RANK:GLA_RANK + S5_WIDTH]


def _proj_even(x, nw, w):
    t = x.shape[0]
    tm = _row_tile(t)
    row = lambda c: pl.BlockSpec((tm, c), lambda i: (i, 0))
    slab = lambda n: pl.BlockSpec((n, tm, LANES), lambda i: (0, i, 0))
    out_shape = (
        jax.ShapeDtypeStruct((GLA_HEADS, t, LANES), F32),
        jax.ShapeDtypeStruct((GLA_HEADS, t, LANES), F32),
        jax.ShapeDtypeStruct((GLA_HEADS, t, LANES), F32),
        jax.ShapeDtypeStruct((2 * GLA_HEADS, t, LANES), F32),
        jax.ShapeDtypeStruct((t, 1024), F32),
        jax.ShapeDtypeStruct((t, S5_WIDTH), F32),
        jax.ShapeDtypeStruct((t, S5_WIDTH), F32),
    )
    weights = (w["w"], w["wg"], w["bg"])
    return pl.pallas_call(
        _proj_even_kernel,
        grid=(t // tm,),
        in_specs=[row(D_MODEL), _const_spec(nw.shape)] + [_const_spec(a.shape) for a in weights],
        out_specs=(slab(4), slab(4), slab(4), slab(8), row(1024), row(S5_WIDTH), row(S5_WIDTH)),
        out_shape=out_shape,
        compiler_params=_params(),
        name="proj_even",
    )(x, nw, *weights)


def _proj_odd_kernel(x_ref, nw_ref, w_ref, alog_ref, dtb_ref, xqkv_ref, z_ref, beta_ref, gd_ref):
    hb = _rms_rows(x_ref[...], nw_ref[...]).astype(BF16)
    d = lambda lo, hi: jnp.dot(hb, w_ref[:, lo:hi], preferred_element_type=F32)
    xqkv_ref[...] = d(0, DN_CONV_DIM)
    z_ref[...] = d(DN_CONV_DIM, DN_CONV_DIM + 1024)
    tail = d(DN_CONV_DIM + 1024, DN_CONV_DIM + 1024 + LANES)
    beta_ref[...] = _sigmoid(tail)
    a = pltpu.roll(tail, LANES - DN_HEADS, 1)
    gd_ref[...] = -jnp.exp(alog_ref[...]) * _softplus(a + dtb_ref[...])


def _proj_odd(x, nw, w):
    t = x.shape[0]
    tm = _row_tile(t)
    row = lambda c: pl.BlockSpec((tm, c), lambda i: (i, 0))
    weights = (w["w"], w["alog"], w["dtb"])
    return pl.pallas_call(
        _proj_odd_kernel,
        grid=(t // tm,),
        in_specs=[row(D_MODEL), _const_spec(nw.shape)] + [_const_spec(a.shape) for a in weights],
        out_specs=(row(DN_CONV_DIM), row(1024), row(LANES), row(LANES)),
        out_shape=(
            jax.ShapeDtypeStruct((t, DN_CONV_DIM), F32),
            jax.ShapeDtypeStruct((t, 1024), F32),
            jax.ShapeDtypeStruct((t, LANES), F32),
            jax.ShapeDtypeStruct((t, LANES), F32),
        ),
        compiler_params=_params(),
        name="proj_odd",
    )(x, nw, *weights)


def _out_even_kernel(x_ref, og_ref, y5_ref, w_ref, o_ref):
    o_ref[...] = x_ref[...] + (_mm(og_ref[...], w_ref[0:1024, :]) + _mm(y5_ref[...], w_ref[1024:, :]))


def _out_even(x, og, y5, w):
    t = x.shape[0]
    tm = _row_tile(t)
    row = lambda c: pl.BlockSpec((tm, c), lambda i: (i, 0))
    return pl.pallas_call(
        _out_even_kernel,
        grid=(t // tm,),
        in_specs=[row(D_MODEL), row(1024), row(S5_WIDTH), _const_spec(w.shape)],
        out_specs=row(D_MODEL),
        out_shape=jax.ShapeDtypeStruct((t, D_MODEL), F32),
        compiler_params=_params(),
        name="out_even",
    )(x, og, y5, w)


def _out_odd_kernel(x_ref, og_ref, w_ref, fw_ref, o_ref, *, final):
    y = x_ref[...] + _mm(og_ref[...], w_ref[...])
    if final:
        y = _rms_rows(y, fw_ref[...])
    o_ref[...] = y


def _out_odd(x, og, w, fw, final):
    t = x.shape[0]
    tm = _row_tile(t)
    row = lambda c: pl.BlockSpec((tm, c), lambda i: (i, 0))
    return pl.pallas_call(
        functools.partial(_out_odd_kernel, final=final),
        grid=(t // tm,),
        in_specs=[row(D_MODEL), row(1024), _const_spec(w.shape), _const_spec(fw.shape)],
        out_specs=row(D_MODEL),
        out_shape=jax.ShapeDtypeStruct((t, D_MODEL), F32),
        compiler_params=_params(),
        name="out_odd",
    )(x, og, w, fw)


def _cumsum_tokens(ref, lead, pad, rows, nb):
    shift = nb
    while shift <= pad:
        ref[lead, pad:pad + rows, :] = ref[lead, pad:pad + rows, :] + ref[lead, pad - shift:pad + rows - shift, :]
        shift *= 2
    while shift < rows:
        ref[lead, pad + shift:pad + rows, :] = (ref[lead, pad + shift:pad + rows, :]
                                                + ref[lead, pad:pad + rows - shift, :])
        shift *= 2


def _gla_prompt_kernel(q_ref, k_ref, g_ref, v_ref, r_ref, far_ref, nw_ref, og_ref, sout_ref,
                       kp, bp, vp, st, oscr, *, nb):
    rows = CHUNK * nb
    pad = PAD_TOKENS * nb
    blk_rows = GLA_BAND * nb
    step = pl.program_id(0)

    @pl.when(step == 0)
    def _():
        st[...] = jnp.zeros_like(st)
        kp[:, 0:pad, :] = jnp.zeros((GLA_HEADS, pad, LANES), F32)
        bp[:, 0:pad, :] = jnp.zeros((GLA_HEADS, pad, LANES), F32)
        vp[:, 0:pad, :] = jnp.zeros((2 * GLA_HEADS, pad, LANES), F32)

    for h in range(GLA_HEADS):
        bp[h, pad:pad + rows, :] = g_ref[h]
        kp[h, pad:pad + rows, :] = k_ref[h]
        _cumsum_tokens(bp, h, pad, rows, nb)
    for s in range(2 * GLA_HEADS):
        vp[s, pad:pad + rows, :] = v_ref[s]

    def band_tile(ti, carry):
        r0 = pl.multiple_of(ti * 64, 64)
        for h in range(GLA_HEADS):
            qt = q_ref[h, pl.ds(r0, 64), :]
            bt = bp[h, pl.ds(pad + r0, 64), :]
            acc0 = jnp.zeros((64, LANES), F32)
            acc1 = jnp.zeros((64, LANES), F32)
            for d in range(GLA_BAND):
                off = pl.multiple_of(pad + r0 - d * nb, 8)
                ks = kp[h, pl.ds(off, 64), :]
                bs = bp[h, pl.ds(off, 64), :]
                w = jnp.sum(qt * ks * jnp.exp(bt - bs), axis=-1, keepdims=True)
                acc0 = acc0 + w * vp[2 * h, pl.ds(off, 64), :]
                acc1 = acc1 + w * vp[2 * h + 1, pl.ds(off, 64), :]
            oscr[2 * h, pl.ds(r0, 64), :] = acc0
            oscr[2 * h + 1, pl.ds(r0, 64), :] = acc1
        return carry

    lax.fori_loop(0, rows // 64, band_tile, 0)

    nblk = CHUNK // GLA_BAND
    for h in range(GLA_HEADS):
        probs = []
        for blk in range(1, nblk):
            lo = blk * blk_rows
            ref_b = bp[h, pad + lo - nb:pad + lo, :]
            qs = q_ref[h, lo:lo + blk_rows, :] * jnp.exp(
                bp[h, pad + lo:pad + lo + blk_rows, :] - jnp.concatenate([ref_b] * GLA_BAND, axis=0))
            ks = kp[h, pad:pad + lo, :] * jnp.exp(
                jnp.concatenate([ref_b] * (GLA_BAND * blk), axis=0) - bp[h, pad:pad + lo, :])
            probs.append(_mm_nt(qs, ks) * far_ref[lo:lo + blk_rows, 0:lo])
        for blk in range(1, nblk):
            lo = blk * blk_rows
            p = probs[blk - 1].astype(BF16)
            for half in range(2):
                sl = 2 * h + half
                oscr[sl, lo:lo + blk_rows, :] = oscr[sl, lo:lo + blk_rows, :] + jnp.dot(
                    p, vp[sl, pad:pad + lo, :].astype(BF16), preferred_element_type=F32)

    def per_batch(b, carry):
        sel = pl.ds(b, CHUNK, stride=nb)
        selp = pl.ds(pad + b, CHUNK, stride=nb)
        loaded = [(q_ref[h, sel, :], kp[h, selp, :], bp[h, selp, :], vp[2 * h, selp, :], vp[2 * h + 1, selp, :],
                   st[b, h]) for h in range(GLA_HEADS)]
        outs = [_mm_nt(qb * jnp.exp(bb), stt) for qb, _, bb, _, _, stt in loaded]
        news = []
        for _, kb, bb, v0, v1, stt in loaded:
            blast = bb[CHUNK - 1:CHUNK, :]
            kd = kb * jnp.exp(blast - bb)
            news.append(stt * jnp.exp(blast) + _mm_tn(jnp.concatenate([v0, v1], axis=1), kd))
        for h in range(GLA_HEADS):
            oscr[2 * h, sel, :] = oscr[2 * h, sel, :] + outs[h][:, :LANES]
            oscr[2 * h + 1, sel, :] = oscr[2 * h + 1, sel, :] + outs[h][:, LANES:]
            st[b, h] = news[h]
        return carry

    lax.fori_loop(0, nb, per_batch, 0)

    nw = nw_ref[...]

    def epi_tile(ti, carry):
        r0 = pl.multiple_of(ti * 64, 64)
        for h in range(GLA_HEADS):
            o0 = oscr[2 * h, pl.ds(r0, 64), :]
            o1 = oscr[2 * h + 1, pl.ds(r0, 64), :]
            ms = (jnp.sum(o0 * o0, axis=-1, keepdims=True) + jnp.sum(o1 * o1, axis=-1, keepdims=True)) / GLA_DV
            inv = lax.rsqrt(ms + EPS)
            c0 = h * GLA_DV
            og_ref[pl.ds(r0, 64), c0:c0 + LANES] = o0 * inv * nw[:, :LANES] * _silu(r_ref[pl.ds(r0, 64), c0:c0 + LANES])
            og_ref[pl.ds(r0, 64), c0 + LANES:c0 + 2 * LANES] = (
                o1 * inv * nw[:, LANES:] * _silu(r_ref[pl.ds(r0, 64), c0 + LANES:c0 + 2 * LANES]))
        return carry

    lax.fori_loop(0, rows // 64, epi_tile, 0)

    @pl.when(step == pl.num_programs(0) - 1)
    def _():
        def wr(b, carry):
            for h in range(GLA_HEADS):
                sout_ref[b, h] = st[b, h].T
            return carry
        lax.fori_loop(0, nb, wr, 0)


def _pair_mask(nb, min_dist):
    n = CHUNK * nb
    r = jnp.arange(n)
    same = (r[:, None] % nb) == (r[None, :] % nb)
    far = (r[None, :] // nb) <= (r[:, None] // nb) - min_dist
    return (same & far).astype(F32)


def _gla_prompt(q, k, g, v, r, nw, nb):
    t = r.shape[0]
    rows = CHUNK * nb
    pad = PAD_TOKENS * nb
    slab = lambda n: pl.BlockSpec((n, rows, LANES), lambda i: (0, i, 0))
    row = lambda c: pl.BlockSpec((rows, c), lambda i: (i, 0))
    far = _pair_mask(nb, GLA_BAND)
    return pl.pallas_call(
        functools.partial(_gla_prompt_kernel, nb=nb),
        grid=(t // rows,),
        in_specs=[slab(4), slab(4), slab(4), slab(8), row(1024), _const_spec(far.shape), _const_spec(nw.shape)],
        out_specs=(row(1024), _const_spec((nb, GLA_HEADS, GLA_DK, GLA_DV))),
        out_shape=(jax.ShapeDtypeStruct((t, 1024), F32),
                   jax.ShapeDtypeStruct((nb, GLA_HEADS, GLA_DK, GLA_DV), F32)),
        scratch_shapes=[
            pltpu.VMEM((GLA_HEADS, pad + rows, LANES), F32),
            pltpu.VMEM((GLA_HEADS, pad + rows, LANES), F32),
            pltpu.VMEM((2 * GLA_HEADS, pad + rows, LANES), F32),
            pltpu.VMEM((nb, GLA_HEADS, GLA_DV, GLA_DK), F32),
            pltpu.VMEM((2 * GLA_HEADS, rows, LANES), F32),
        ],
        compiler_params=_params(),
        name="gla_prompt",
    )(q, k, g, v, r, far, nw)


def _gla_step_kernel(q_ref, k_ref, g_ref, v_ref, r_ref, s_ref, nw_ref, *rest):
    og_ref, sout_ref = rest[-2:]
    nw = nw_ref[...]
    zeros = jnp.zeros((LANES - 24, LANES), F32)
    for h in range(GLA_HEADS):
        stack = jnp.concatenate([jnp.exp(g_ref[h]), k_ref[h], q_ref[h], zeros], axis=0)
        cols = stack.T
        outs = []
        for n in range(8):
            s_old = s_ref[n, h]
            vrow = jnp.concatenate([v_ref[2 * h, n:n + 1, :], v_ref[2 * h + 1, n:n + 1, :]], axis=1)
            s_new = s_old * cols[:, n:n + 1] + cols[:, 8 + n:9 + n] * vrow
            sout_ref[n, h] = s_new
            outs.append(jnp.sum(cols[:, 16 + n:17 + n] * s_new, axis=0, keepdims=True))
        o = jnp.concatenate(outs, axis=0)
        ms = jnp.mean(o * o, axis=-1, keepdims=True)
        c0 = h * GLA_DV
        og_ref[:, c0:c0 + GLA_DV] = o * lax.rsqrt(ms + EPS) * nw * _silu(r_ref[:, c0:c0 + GLA_DV])


def _stacked_state_args(states, layer, prev, n_inputs, out_index):
    tail = states.shape[2:]
    spec = pl.BlockSpec((None, 8) + tail, lambda i, _l=layer: (_l, i) + (0,) * len(tail))
    if prev is None:
        prev = jnp.zeros(states.shape, F32)
    return spec, [pl.BlockSpec(memory_space=pl.ANY)], [prev], {n_inputs: out_index}


def _gla_step(q, k, g, v, r, states, layer, prev, nw):
    n = r.shape[0]
    slab = lambda c: pl.BlockSpec((c, 8, LANES), lambda i: (0, i, 0))
    sspec, extra_specs, extra_args, aliases = _stacked_state_args(states, layer, prev, 7, 1)
    return pl.pallas_call(
        _gla_step_kernel,
        grid=(n // 8,),
        in_specs=[slab(4), slab(4), slab(4), slab(8), pl.BlockSpec((8, 1024), lambda i: (i, 0)), sspec,
                  _const_spec(nw.shape)] + extra_specs,
        out_specs=(pl.BlockSpec((8, 1024), lambda i: (i, 0)), sspec),
        out_shape=(jax.ShapeDtypeStruct((n, 1024), F32), jax.ShapeDtypeStruct(states.shape, F32)),
        input_output_aliases=aliases,
        compiler_params=_params(),
        name="gla_step",
    )(q, k, g, v, r, states, nw, *extra_args)


def _s5_prep_kernel(lre_ref, lim_ref, ldt_ref, lre_r_ref, lim_r_ref, ldt_r_ref, bre_ref, bim_ref,
                    ar_ref, ai_ref, bbre_ref, bbim_ref):
    def disc(lre, lim, ldt):
        dt = jnp.exp(ldt)
        mag = jnp.exp(lre * dt)
        ar = mag * jnp.cos(lim * dt)
        ai = mag * jnp.sin(lim * dt)
        return ar, ai

    ar, ai = disc(lre_ref[...], lim_ref[...], ldt_ref[...])
    ar_ref[...] = ar
    ai_ref[...] = ai
    lre, lim = lre_r_ref[...], lim_r_ref[...]
    ar, ai = disc(lre, lim, ldt_r_ref[...])
    den = lre * lre + lim * lim
    wr = ((ar - 1.0) * lre + ai * lim) / den
    wi = (ai * lre - (ar - 1.0) * lim) / den
    bre, bim = bre_ref[...], bim_ref[...]
    bbre_ref[...] = wr * bre - wi * bim
    bbim_ref[...] = wr * bim + wi * bre


def _s5_prep(lam_re, lam_im, log_dt, b_re, b_im):
    n = lam_re.shape[0]
    rows = n * S5_GROUPS
    lre = lam_re.reshape(rows, S5_P)
    lim = lam_im.reshape(rows, S5_P)
    ldt = jnp.broadcast_to(log_dt.reshape(rows, 1), (rows, S5_P))
    rep = lambda a: jnp.repeat(a, S5_GROUP_CH, axis=1)
    args = (lre, lim, ldt, rep(lre), rep(lim), rep(ldt),
            b_re.reshape(rows, S5_P * S5_GROUP_CH), b_im.reshape(rows, S5_P * S5_GROUP_CH))
    wide = jax.ShapeDtypeStruct((rows, S5_P * S5_GROUP_CH), F32)
    narrow = jax.ShapeDtypeStruct((rows, S5_P), F32)
    ar, ai, bbre, bbim = pl.pallas_call(
        _s5_prep_kernel, out_shape=(narrow, narrow, wide, wide), name="s5_prep")(*args)
    shp = (n, S5_GROUPS, S5_P, S5_GROUP_CH)
    return ar.reshape(n, S5_GROUPS, S5_P), ai.reshape(n, S5_GROUPS, S5_P), bbre.reshape(shp), bbim.reshape(shp)


def _s5_kernel(u_ref, sg_ref, h0re_ref, h0im_ref, wbre_ref, wbim_ref, wc_ref, are_ref, aim_ref, d_ref,
               wglu_ref, bglu_ref, y_ref, hre_out, him_out, hre, him, cre, cim, ys, *, nb, tokens):
    step = pl.program_id(0)
    rows = nb * tokens

    @pl.when(step == 0)
    def _():
        cre[...] = h0re_ref[...]
        cim[...] = h0im_ref[...]

    for m in range(4):
        um = u_ref[:, m * LANES:(m + 1) * LANES].astype(BF16)
        hre[:, m * 512:(m + 1) * 512] = jnp.dot(um, wbre_ref[m], preferred_element_type=F32)
        him[:, m * 512:(m + 1) * 512] = jnp.dot(um, wbim_ref[m], preferred_element_type=F32)

    for qd in range(4):
        ql = slice(qd * 512, (qd + 1) * 512)
        ar = jnp.broadcast_to(are_ref[:, ql], (nb, 512))
        ai = jnp.broadcast_to(aim_ref[:, ql], (nb, 512))

        def tok(t, carry, ql=ql, ar=ar, ai=ai):
            hr, hi = carry
            sel = pl.ds(pl.multiple_of(t * nb, nb), nb)
            nr = ar * hr - ai * hi + hre[sel, ql]
            ni = ar * hi + ai * hr + him[sel, ql]
            hre[sel, ql] = nr
            him[sel, ql] = ni
            return nr, ni

        hr, hi = lax.fori_loop(0, tokens, tok, (cre[:, ql], cim[:, ql]))
        cre[:, ql] = hr
        cim[:, ql] = hi

    for m in range(4):
        hc = jnp.concatenate([hre[:, m * 512:(m + 1) * 512], him[:, m * 512:(m + 1) * 512]], axis=1)
        sl = slice(m * LANES, (m + 1) * LANES)
        y = _mm(hc, wc_ref[m]) + d_ref[:, sl] * u_ref[:, sl]
        ys[:, sl] = jax.nn.gelu(y)
    y = ys[...]
    gate = _sigmoid(_mm(y, wglu_ref[...]) + bglu_ref[...])
    y_ref[...] = y * gate * _silu(sg_ref[...])

    @pl.when(step == pl.num_programs(0) - 1)
    def _():
        hre_out[...] = cre[...]
        him_out[...] = cim[...]


def _s5(u, sg, h0re, h0im, w, nb, tokens):
    t = u.shape[0]
    rows = nb * tokens
    row = lambda c: pl.BlockSpec((rows, c), lambda i: (i, 0))
    weights = (w["wbre"], w["wbim"], w["wc"], w["are"], w["aim"], w["d"], w["wglu"], w["bglu"])
    state = jax.ShapeDtypeStruct((nb, S5_STATE), F32)
    return pl.pallas_call(
        functools.partial(_s5_kernel, nb=nb, tokens=tokens),
        grid=(t // rows,),
        in_specs=[row(S5_WIDTH), row(S5_WIDTH), _const_spec(h0re.shape), _const_spec(h0im.shape)]
        + [_const_spec(a.shape) for a in weights],
        out_specs=(row(S5_WIDTH), _const_spec((nb, S5_STATE)), _const_spec((nb, S5_STATE))),
        out_shape=(jax.ShapeDtypeStruct((t, S5_WIDTH), F32), state, state),
        scratch_shapes=[
            pltpu.VMEM((rows, S5_STATE), F32),
            pltpu.VMEM((rows, S5_STATE), F32),
            pltpu.VMEM((nb, S5_STATE), F32),
            pltpu.VMEM((nb, S5_STATE), F32),
            pltpu.VMEM((rows, S5_WIDTH), F32),
        ],
        compiler_params=_params(),
        name="s5",
    )(u, sg, h0re, h0im, *weights)


def _dn_solve(lms, rhss):
    nblk = CHUNK // SUB
    row = lax.broadcasted_iota(jnp.int32, (SUB, LANES), 0)
    lane = lax.broadcasted_iota(jnp.int32, (SUB, LANES), 1)
    seg = (lane // SUB) * SUB
    in_diag = [(lane >= SUB * blk) & (lane < SUB * (blk + 1)) for blk in range(nblk)]
    eye_pack = jnp.where((lane - seg == row) & (lane < CHUNK), 1.0, 0.0)
    zpad = jnp.zeros((CHUNK, LANES - CHUNK), F32)
    t_invs, lowers = [], []
    for lm in lms:
        wide = jnp.concatenate([lm, zpad], axis=1)
        tiles = [wide[SUB * blk:SUB * (blk + 1)] for blk in range(nblk)]
        dpack = jnp.zeros((SUB, LANES), F32)
        for blk in range(nblk):
            dpack = jnp.where(in_diag[blk], tiles[blk], dpack)
        inv = eye_pack
        for j in range(SUB - 1):
            col_j = jnp.take_along_axis(dpack, seg + j, axis=1)
            inv = inv - col_j * inv[j:j + 1, :]
        t_invs.append(jnp.concatenate(
            [jnp.where(in_diag[blk], inv, 0.0)[:, :CHUNK] for blk in range(nblk)], axis=0).astype(BF16))
        lowers.append(jnp.concatenate(
            [jnp.where(in_diag[blk], 0.0, tiles[blk])[:, :CHUNK] for blk in range(nblk)], axis=0).astype(BF16))
    d = lambda a, b: jnp.dot(a, b, preferred_element_type=F32)
    his = [rhs.astype(BF16) for rhs in rhss]
    los = [(rhs - hi.astype(F32)).astype(BF16) for rhs, hi in zip(rhss, his)]
    trs = [d(t, hi) + d(t, lo_) for t, hi, lo_ in zip(t_invs, his, los)]
    tns = [d(t, lw) for t, lw in zip(t_invs, lowers)]
    xs = [[tr[0:SUB]] for tr in trs]
    for blk in range(1, nblk):
        lo = SUB * blk
        curs = [tr[lo:lo + SUB] - _mm(tn[lo:lo + SUB, 0:lo], jnp.concatenate(x, axis=0))
                for tr, tn, x in zip(trs, tns, xs)]
        for x, cur in zip(xs, curs):
            x.append(cur)
    return [jnp.concatenate(x, axis=0) for x in xs]


def _dn_prompt_kernel(x_ref, z_ref, beta_ref, gd_ref, cw_ref, nw_ref, c0_ref,
                      og_ref, sout_ref, cout_ref,
                      xs, qs, ks, vs, bcol, gcol, gc, gt, oscr, *, nb):
    rows = CHUNK * nb
    hist = (CONV_W - 1) * nb
    pad = PAD_TOKENS * nb
    step = pl.program_id(0)
    st = sout_ref

    @pl.when(step == 0)
    def _():
        st[...] = jnp.zeros_like(st)
        xs[0:hist, :] = c0_ref[...]
        gc[0, 0:pad, :] = jnp.zeros((pad, LANES), F32)

    xs[hist:hist + 64, :] = x_ref[0:64, :]

    def conv_rows(src, base, r0):
        tap = lambda i: base + r0 + i * nb if isinstance(r0, int) else pl.multiple_of(base + r0 + i * nb, 8)
        for c in range(DN_CONV_DIM // LANES):
            cl = slice(c * LANES, (c + 1) * LANES)
            y = src[pl.ds(tap(0), 64), cl] * cw_ref[0:1, cl]
            for i in range(1, CONV_W):
                y = y + src[pl.ds(tap(i), 64), cl] * cw_ref[i:i + 1, cl]
            y = _silu(y)
            if c < 2 * DN_HEADS:
                y = y * lax.rsqrt(jnp.sum(y * y, axis=-1, keepdims=True) + EPS)
            if c < DN_HEADS:
                qs[c, pl.ds(r0, 64), :] = y * (DN_DK ** -0.5)
            elif c < 2 * DN_HEADS:
                ks[c - DN_HEADS, pl.ds(r0, 64), :] = y
            else:
                vs[c - 2 * DN_HEADS, pl.ds(r0, 64), :] = y

    conv_rows(xs, 0, 0)

    def conv_tile(ti, carry):
        conv_rows(x_ref, -hist, pl.multiple_of(ti * 64, 64))
        return carry

    lax.fori_loop(1, rows // 64, conv_tile, 0)

    xs[0:hist, :] = x_ref[rows - hist:rows, :]

    gc[0, pad:pad + rows, :] = gd_ref[...]
    _cumsum_tokens(gc, 0, pad, rows, nb)

    def bc_tile(ti, carry):
        r0 = pl.multiple_of(ti * 64, 64)
        gtile = gc[0, pl.ds(pad + r0, 64), :]
        btile = beta_ref[pl.ds(r0, 64), :]
        for h in range(DN_HEADS):
            gcol[h, pl.ds(r0, 64), :] = jnp.broadcast_to(gtile[:, h:h + 1], (64, LANES))
            bcol[h, pl.ds(r0, 64), :] = jnp.broadcast_to(btile[:, h:h + 1], (64, LANES))
        return carry

    lax.fori_loop(0, rows // 64, bc_tile, 0)

    ii = lax.broadcasted_iota(jnp.int32, (CHUNK, CHUNK), 0)
    jj = lax.broadcasted_iota(jnp.int32, (CHUNK, CHUNK), 1)
    eye = (ii == jj).astype(BF16)

    parts = [_split3(gc[0, pl.ds(pad + b, CHUNK, stride=nb), :]) for b in range(nb)]
    tdot = lambda x: lax.dot_general(x, eye, (((0,), (0,)), ((), ())), preferred_element_type=F32)
    firsts = [tdot(p[0]) for p in parts]
    seconds = [tdot(p[1]) for p in parts]
    thirds = [tdot(p[2]) for p in parts]
    for b in range(nb):
        gt[b] = (firsts[b] + seconds[b]) + thirds[b]

    def per_batch(bi, carry):
        def head_group(gi, carry2):
            probs = [(bi * DN_BATCH + jb, gi * DN_GROUP + jh) for jb in range(DN_BATCH) for jh in range(DN_GROUP)]
            sels = [pl.ds(b, CHUNK, stride=nb) for b, _ in probs]
            loaded = [(qs[h, sel, :], ks[h, sel, :], vs[h, sel, :], bcol[h, sel, :], gcol[h, sel, :],
                       gt[b, pl.ds(h, 1), :], st[b, h]) for (b, h), sel in zip(probs, sels)]
            qbs, kbs, vbs, bcs, gcls, grows, s_olds = zip(*loaded)
            decs = [jnp.where(jj <= ii, jnp.exp(jnp.where(jj <= ii, gcl[:, :CHUNK] - grow, 0.0)), 0.0)
                    for gcl, grow in zip(gcls, grows)]
            kks = [_mm_nt(kb, kb) for kb in kbs]
            qks = [_mm_nt(qb, kb) for qb, kb in zip(qbs, kbs)]
            lms = [jnp.where(jj < ii, bc[:, :CHUNK] * kk * dec, 0.0) for bc, kk, dec in zip(bcs, kks, decs)]
            egs = [jnp.exp(gcl) for gcl in gcls]
            rhss = [jnp.concatenate([vb * bc, kb * (bc * eg)], axis=1)
                    for vb, kb, bc, eg in zip(vbs, kbs, bcs, egs)]
            sols = _dn_solve(lms, rhss)
            v_news = [sol[:, :DN_DV] - _mm(sol[:, DN_DV:], s_old) for sol, s_old in zip(sols, s_olds)]
            os_ = [_mm(qb * eg, s_old) + _mm(qk * dec, v_new)
                   for qb, eg, s_old, qk, dec, v_new in zip(qbs, egs, s_olds, qks, decs, v_news)]
            glasts = [gcl[CHUNK - 1:CHUNK, :] for gcl in gcls]
            s_news = [s_old * jnp.exp(glast) + _mm_tn(kb * jnp.exp(glast - gcl), v_new)
                      for s_old, glast, kb, gcl, v_new in zip(s_olds, glasts, kbs, gcls, v_news)]
            for (b, h), sel, o, s_new in zip(probs, sels, os_, s_news):
                st[b, h] = s_new
                oscr[h, sel, :] = o
            return carry2

        lax.fori_loop(0, DN_HEADS // DN_GROUP, head_group, 0)
        return carry

    lax.fori_loop(0, nb // DN_BATCH, per_batch, 0)

    nw = nw_ref[...]

    def epi_tile(ti, carry):
        r0 = pl.multiple_of(ti * 64, 64)
        for h in range(DN_HEADS):
            o = oscr[h, pl.ds(r0, 64), :]
            inv = lax.rsqrt(jnp.mean(o * o, axis=-1, keepdims=True) + EPS)
            cl = slice(h * LANES, (h + 1) * LANES)
            og_ref[pl.ds(r0, 64), cl] = o * inv * nw * _silu(z_ref[pl.ds(r0, 64), cl])
        return carry

    lax.fori_loop(0, rows // 64, epi_tile, 0)

    @pl.when(step == pl.num_programs(0) - 1)
    def _():
        cout_ref[...] = xs[0:hist, :]


def _dn_prompt(xqkv, z, beta, gd, cw, nw, c0, nb):
    t = z.shape[0]
    rows = CHUNK * nb
    hist = (CONV_W - 1) * nb
    row = lambda c: pl.BlockSpec((rows, c), lambda i: (i, 0))
    slab = pltpu.VMEM((DN_HEADS, rows, LANES), F32)
    return pl.pallas_call(
        functools.partial(_dn_prompt_kernel, nb=nb),
        grid=(t // rows,),
        in_specs=[row(DN_CONV_DIM), row(1024), row(LANES), row(LANES), _const_spec(cw.shape),
                  _const_spec(nw.shape), _const_spec(c0.shape)],
        out_specs=(row(1024), _const_spec((nb, DN_HEADS, DN_DK, DN_DV)), _const_spec((hist, DN_CONV_DIM))),
        out_shape=(jax.ShapeDtypeStruct((t, 1024), F32),
                   jax.ShapeDtypeStruct((nb, DN_HEADS, DN_DK, DN_DV), F32),
                   jax.ShapeDtypeStruct((hist, DN_CONV_DIM), F32)),
        scratch_shapes=[
            pltpu.VMEM((hist + 64, DN_CONV_DIM), F32),
            slab, slab, slab, slab, slab,
            pltpu.VMEM((1, PAD_TOKENS * nb + rows, LANES), F32),
            pltpu.VMEM((nb, LANES, CHUNK), F32),
            slab,
        ],
        compiler_params=_params(),
        name="dn_prompt",
    )(xqkv, z, beta, gd, cw, nw, c0)


def _dn_step_kernel(x_ref, cb_ref, z_ref, beta_ref, gd_ref, s_ref, cw_ref, nw_ref, *rest):
    og_ref, sout_ref, cn_ref = rest[-3:]
    nw = nw_ref[...]
    cn_ref[0] = cb_ref[1]
    cn_ref[1] = cb_ref[2]
    cn_ref[2] = x_ref[...]
    zeros = jnp.zeros((LANES - 16, LANES), F32)
    beta = beta_ref[...]
    eg_all = jnp.exp(gd_ref[...])

    def conv(c):
        cl = slice(c * LANES, (c + 1) * LANES)
        y = cb_ref[0, :, cl] * cw_ref[0:1, cl]
        y = y + cb_ref[1, :, cl] * cw_ref[1:2, cl]
        y = y + cb_ref[2, :, cl] * cw_ref[2:3, cl]
        y = y + x_ref[:, cl] * cw_ref[3:4, cl]
        return _silu(y)

    def l2(y):
        return y * lax.rsqrt(jnp.sum(y * y, axis=-1, keepdims=True) + EPS)

    for h in range(DN_HEADS):
        qh = l2(conv(h)) * (DN_DK ** -0.5)
        kh = l2(conv(DN_HEADS + h))
        vh = conv(2 * DN_HEADS + h)
        cols = jnp.concatenate([kh, qh, zeros], axis=0).T
        outs = []
        for n in range(8):
            s_old = s_ref[n, h]
            kcol = cols[:, n:n + 1]
            eg = eg_all[n:n + 1, h:h + 1]
            ks_row = jnp.sum(kcol * s_old, axis=0, keepdims=True)
            v_new = beta[n:n + 1, h:h + 1] * (vh[n:n + 1, :] - eg * ks_row)
            s_new = s_old * eg + kcol * v_new
            sout_ref[n, h] = s_new
            outs.append(jnp.sum(cols[:, 8 + n:9 + n] * s_new, axis=0, keepdims=True))
        o = jnp.concatenate(outs, axis=0)
        cl = slice(h * LANES, (h + 1) * LANES)
        og_ref[:, cl] = o * lax.rsqrt(jnp.mean(o * o, axis=-1, keepdims=True) + EPS) * nw * _silu(z_ref[:, cl])


def _dn_step(xqkv, cbuf, z, beta, gd, states, layer, prev, cw, nw):
    n = z.shape[0]
    row = lambda c: pl.BlockSpec((8, c), lambda i: (i, 0))
    cspec = pl.BlockSpec((CONV_W - 1, 8, DN_CONV_DIM), lambda i: (0, i, 0))
    sspec, extra_specs, extra_args, aliases = _stacked_state_args(states, layer, prev, 8, 1)
    return pl.pallas_call(
        _dn_step_kernel,
        grid=(n // 8,),
        in_specs=[row(DN_CONV_DIM), cspec, row(1024), row(LANES), row(LANES), sspec,
                  _const_spec(cw.shape), _const_spec(nw.shape)] + extra_specs,
        out_specs=(row(1024), sspec, cspec),
        out_shape=(jax.ShapeDtypeStruct((n, 1024), F32), jax.ShapeDtypeStruct(states.shape, F32),
                   jax.ShapeDtypeStruct(cbuf.shape, F32)),
        input_output_aliases=aliases,
        compiler_params=_params(),
        name="dn_step",
    )(xqkv, cbuf, z, beta, gd, states, cw, nw, *extra_args)


def _pad_cols(a, n):
    return jnp.pad(a, ((0, 0), (0, n - a.shape[1])))


def _even_weights(w_in, w_gate_up, b_gate, w_out):
    return {
        "w": _pad_cols(w_in.astype(BF16), W_IN_PAD),
        "wg": jnp.pad(w_gate_up, ((0, LANES - GLA_RANK), (0, 0))).astype(BF16),
        "bg": b_gate.reshape(1, -1),
        "wo": w_out.astype(BF16),
    }


def _s5_weights(ar, ai, bbre, bbim, c_re, c_im, d, w_glu, b_glu):
    eye = jnp.eye(8, dtype=F32)

    def pack_b(bb):
        t = bb.reshape(4, 8, S5_P, S5_GROUP_CH).transpose(0, 1, 3, 2)
        return (t[:, :, :, None, :] * eye[None, :, None, :, None]).reshape(4, 128, 512).astype(BF16)

    def pack_c(c):
        t = c.reshape(4, 8, S5_GROUP_CH, S5_P).transpose(0, 1, 3, 2)
        return (t[:, :, :, None, :] * eye[None, :, None, :, None]).reshape(4, 512, 128)

    return {
        "wbre": pack_b(bbre), "wbim": pack_b(bbim),
        "wc": jnp.concatenate([pack_c(c_re), -pack_c(c_im)], axis=1).astype(BF16),
        "are": ar.reshape(1, S5_STATE), "aim": ai.reshape(1, S5_STATE),
        "d": d.reshape(1, -1), "wglu": w_glu.astype(BF16), "bglu": b_glu.reshape(1, -1),
    }


def _odd_weights(w_in, conv_w, a_log, dt_bias, w_out):
    return {
        "w": _pad_cols(w_in.astype(BF16), W_IN_PAD),
        "alog": _pad_cols(a_log.reshape(1, -1), LANES), "dtb": _pad_cols(dt_bias.reshape(1, -1), LANES),
        "cw": jnp.pad(conv_w, ((0, 8 - CONV_W), (0, 0))),
        "wo": w_out.astype(BF16),
    }


def kernel(x_prompt, x_sample, state_gla, state_s5_re, state_s5_im, state_delta, state_conv, norm_w, final_norm_w, w_in_even, gla_w_gate_up, gla_b_gate, gla_norm_w, s5_lambda_re, s5_lambda_im, s5_log_dt, s5_b_re, s5_b_im, s5_c_re, s5_c_im, s5_d, s5_w_glu, s5_b_glu, w_out_even, w_in_odd, dn_conv_w, dn_a_log, dn_dt_bias, dn_norm_w, w_out_odd):
    nb, seq, _ = x_prompt.shape
    ns = x_sample.shape[0]
    assert seq % CHUNK == 0 and nb % 8 == 0 and ns % 8 == 0 and x_sample.shape[1] == 1
    depth = norm_w.shape[0]

    xp = x_prompt.transpose(1, 0, 2).reshape(seq * nb, D_MODEL)
    xs = x_sample.reshape(ns, D_MODEL)
    fw = final_norm_w.reshape(1, -1)

    ar, ai, bbre, bbim = _s5_prep(s5_lambda_re, s5_lambda_im, s5_log_dt, s5_b_re, s5_b_im)

    gla_p, s5r_p, s5i_p, s5r_s, s5i_s, dn_p, cv_p, cv_s = ([] for _ in range(8))
    gla_s = dn_s = None
    for layer in range(depth):
        i = layer // 2
        nw = norm_w[layer].reshape(1, -1)
        if layer % 2 == 0:
            w = _even_weights(w_in_even[i], gla_w_gate_up[i], gla_b_gate[i], w_out_even[i])
            w5 = _s5_weights(ar[i], ai[i], bbre[i], bbim[i], s5_c_re[i], s5_c_im[i], s5_d[i],
                             s5_w_glu[i], s5_b_glu[i])
            gnw = gla_norm_w[i].reshape(1, -1)
            q, k, g, v, r, u, sg = _proj_even(xp, nw, w)
            og, s_new = _gla_prompt(q, k, g, v, r, gnw, nb)
            zero = jnp.zeros((nb, S5_STATE), F32)
            y5, hre, him = _s5(u, sg, zero, zero, w5, nb, CHUNK)
            xp = _out_even(xp, og, y5, w["wo"])
            gla_p.append(s_new)
            s5r_p.append(hre.reshape(nb, S5_GROUPS, S5_P))
            s5i_p.append(him.reshape(nb, S5_GROUPS, S5_P))
            q, k, g, v, r, u, sg = _proj_even(xs, nw, w)
            og, gla_s = _gla_step(q, k, g, v, r, state_gla, i, gla_s, gnw)
            y5, hre, him = _s5(u, sg, state_s5_re[i].reshape(ns, S5_STATE), state_s5_im[i].reshape(ns, S5_STATE),
                               w5, ns, 1)
            xs = _out_even(xs, og, y5, w["wo"])
            s5r_s.append(hre.reshape(ns, S5_GROUPS, S5_P))
            s5i_s.append(him.reshape(ns, S5_GROUPS, S5_P))
        else:
            w = _odd_weights(w_in_odd[i], dn_conv_w[i], dn_a_log[i], dn_dt_bias[i], w_out_odd[i])
            dnw = dn_norm_w[i].reshape(1, -1)
            final = layer == depth - 1
            xqkv, z, beta, gd = _proj_odd(xp, nw, w)
            c0 = jnp.zeros(((CONV_W - 1) * nb, DN_CONV_DIM), F32)
            og, s_new, c_new = _dn_prompt(xqkv, z, beta, gd, w["cw"], dnw, c0, nb)
            xp = _out_odd(xp, og, w["wo"], fw, final)
            dn_p.append(s_new)
            cv_p.append(c_new.reshape(CONV_W - 1, nb, DN_CONV_DIM).transpose(1, 0, 2))
            xqkv, z, beta, gd = _proj_odd(xs, nw, w)
            og, dn_s, c_new = _dn_step(xqkv, state_conv[i].transpose(1, 0, 2), z, beta, gd, state_delta, i, dn_s,
                                       w["cw"], dnw)
            xs = _out_odd(xs, og, w["wo"], fw, final)
            cv_s.append(c_new.transpose(1, 0, 2))

    y_prompt = xp.reshape(seq, nb, D_MODEL).transpose(1, 0, 2)
    y_sample = xs.reshape(ns, 1, D_MODEL)
    st = jnp.stack
    return (y_prompt, y_sample, st(gla_p), gla_s, st(s5r_p), st(s5i_p), st(s5r_s), st(s5i_s),
            st(dn_p), dn_s, st(cv_p), st(cv_s))
```

```python
import functools
import math

import jax
import jax.numpy as jnp
from jax import lax
from jax.experimental import pallas as pl
from jax.experimental.pallas import tpu as pltpu

F32 = jnp.float32
BF16 = jnp.bfloat16
EPS = 1e-6

D_MODEL = 1024
GLA_HEADS, GLA_DK, GLA_DV, GLA_RANK = 4, 128, 256, 16
GLA_GATE_NORM = 16.0
S5_GROUPS, S5_GROUP_CH, S5_P, S5_WIDTH = 32, 16, 64, 512
S5_STATE = S5_GROUPS * S5_P
DN_HEADS, DN_DK, DN_DV = 8, 128, 128
DN_CONV_DIM, CONV_W = 3072, 4
CHUNK = 64
SUB = 8
GLA_BAND = 8
PAD_TOKENS = 16
DN_BATCH = 4
DN_GROUP = 8
LANES = 128
W_IN_PAD = 4224
VMEM_LIMIT = 56 * 1024 * 1024


def _mm(a, b):
    return jnp.dot(a.astype(BF16), b.astype(BF16), preferred_element_type=F32)


def _mm_nt(a, b):
    return lax.dot_general(a.astype(BF16), b.astype(BF16), (((1,), (1,)), ((), ())),
                           preferred_element_type=F32)


def _mm_tn(a, b):
    return lax.dot_general(a.astype(BF16), b.astype(BF16), (((0,), (0,)), ((), ())),
                           preferred_element_type=F32)


def _split3(a):
    a1 = a.astype(BF16)
    r1 = a - a1.astype(F32)
    a2 = r1.astype(BF16)
    a3 = (r1 - a2.astype(F32)).astype(BF16)
    return a1, a2, a3


def _mm_tn_exact(a, m_bf16):
    a1, a2, a3 = _split3(a)
    d = lambda x: lax.dot_general(x, m_bf16, (((0,), (0,)), ((), ())), preferred_element_type=F32)
    return (d(a1) + d(a2)) + d(a3)


def _sigmoid(x):
    return 1.0 / (1.0 + jnp.exp(-x))


def _silu(x):
    return x * _sigmoid(x)


def _softplus(x):
    return jnp.maximum(x, 0.0) + jnp.log1p(jnp.exp(-jnp.abs(x)))


def _rms_rows(x, w):
    ms = jnp.mean(x * x, axis=-1, keepdims=True)
    return x * lax.rsqrt(ms + EPS) * w


def _const_spec(shape):
    nd = len(shape)
    return pl.BlockSpec(shape, lambda i, _nd=nd: (0,) * _nd)


def _params(**flags):
    return pltpu.CompilerParams(dimension_semantics=("arbitrary",), vmem_limit_bytes=VMEM_LIMIT,
                                flags=flags or None)


def _row_tile(t):
    return min(512, t)


def _proj_even_kernel(x_ref, nw_ref, w_ref, wg_ref, bg_ref, q_ref, k_ref, g_ref, v_ref, r_ref, u_ref, sg_ref,
                      *relayout, nb_in):
    if nb_in:
        xo_ref, xt = relayout
        for b in range(nb_in):
            for c in range(D_MODEL // LANES):
                xt[c, pl.ds(b, x_ref.shape[1], stride=nb_in), :] = x_ref[b, :, c * LANES:(c + 1) * LANES]
        x = jnp.concatenate([xt[c] for c in range(D_MODEL // LANES)], axis=1)
        xo_ref[...] = x
    else:
        x = x_ref[...]
    hb = _rms_rows(x, nw_ref[...]).astype(BF16)
    d = lambda lo, hi: jnp.dot(hb, w_ref[:, lo:hi], preferred_element_type=F32)
    q = d(0, 512) * (GLA_DK ** -0.5)
    k = d(512, 1024)
    lr_u = d(3072, 3712)
    u_sg = d(3584, 4224)
    logit = jnp.dot(lr_u[:, :LANES].astype(BF16), wg_ref[...], preferred_element_type=F32) + bg_ref[...]
    g = -_softplus(-logit) / GLA_GATE_NORM
    for h in range(GLA_HEADS):
        sl = slice(h * LANES, (h + 1) * LANES)
        q_ref[h] = q[:, sl]
        k_ref[h] = k[:, sl]
        g_ref[h] = g[:, sl]
    v = d(1024, 2048)
    for s in range(2 * GLA_HEADS):
        v_ref[s] = v[:, s * LANES:(s + 1) * LANES]
    r_ref[...] = d(2048, 3072)
    u_ref[...] = lr_u[:, GLA_RANK:GLA_RANK + S5_WIDTH]
    sg_ref[...] = u_sg[:, GLA_RANK:GLA_RANK + S5_WIDTH]


def _proj_even(x, nw, w, batch_major=False):
    nb_in = x.shape[0] if batch_major else 0
    t = x.shape[0] * x.shape[1] if batch_major else x.shape[0]
    tm = _row_tile(t)
    row = lambda c: pl.BlockSpec((tm, c), lambda i: (i, 0))
    slab = lambda n: pl.BlockSpec((n, tm, LANES), lambda i: (0, i, 0))
    x_spec = pl.BlockSpec((nb_in, tm // nb_in, D_MODEL), lambda i: (0, i, 0)) if batch_major else row(D_MODEL)
    extra_out_shape = (jax.ShapeDtypeStruct((t, D_MODEL), F32),) if batch_major else ()
    extra_out_specs = (row(D_MODEL),) if batch_major else ()
    scratch = [pltpu.VMEM((D_MODEL // LANES, tm, LANES), F32)] if batch_major else []
    out_shape = (
        jax.ShapeDtypeStruct((GLA_HEADS, t, LANES), F32),
        jax.ShapeDtypeStruct((GLA_HEADS, t, LANES), F32),
        jax.ShapeDtypeStruct((GLA_HEADS, t, LANES), F32),
        jax.ShapeDtypeStruct((2 * GLA_HEADS, t, LANES), F32),
        jax.ShapeDtypeStruct((t, 1024), F32),
        jax.ShapeDtypeStruct((t, S5_WIDTH), F32),
        jax.ShapeDtypeStruct((t, S5_WIDTH), F32),
    )
    weights = (w["w"], w["wg"], w["bg"])
    return pl.pallas_call(
        functools.partial(_proj_even_kernel, nb_in=nb_in),
        grid=(t // tm,),
        in_specs=[x_spec, _const_spec(nw.shape)] + [_const_spec(a.shape) for a in weights],
        out_specs=(slab(4), slab(4), slab(4), slab(8), row(1024), row(S5_WIDTH), row(S5_WIDTH)) + extra_out_specs,
        out_shape=out_shape + extra_out_shape,
        scratch_shapes=scratch,
        compiler_params=_params(),
        name="proj_even",
    )(x, nw, *weights)


def _proj_odd_kernel(x_ref, nw_ref, w_ref, alog_ref, dtb_ref, xqkv_ref, z_ref, beta_ref, gd_ref):
    hb = _rms_rows(x_ref[...], nw_ref[...]).astype(BF16)
    d = lambda lo, hi: jnp.dot(hb, w_ref[:, lo:hi], preferred_element_type=F32)
    xqkv_ref[...] = d(0, DN_CONV_DIM)
    z_ref[...] = d(DN_CONV_DIM, DN_CONV_DIM + 1024)
    tail = d(DN_CONV_DIM + 1024, DN_CONV_DIM + 1024 + LANES)
    beta_ref[...] = _sigmoid(tail)
    a = pltpu.roll(tail, LANES - DN_HEADS, 1)
    gd_ref[...] = -jnp.exp(alog_ref[...]) * _softplus(a + dtb_ref[...])


def _proj_odd(x, nw, w):
    t = x.shape[0]
    tm = _row_tile(t)
    row = lambda c: pl.BlockSpec((tm, c), lambda i: (i, 0))
    weights = (w["w"], w["alog"], w["dtb"])
    return pl.pallas_call(
        _proj_odd_kernel,
        grid=(t // tm,),
        in_specs=[row(D_MODEL), _const_spec(nw.shape)] + [_const_spec(a.shape) for a in weights],
        out_specs=(row(DN_CONV_DIM), row(1024), row(LANES), row(LANES)),
        out_shape=(
            jax.ShapeDtypeStruct((t, DN_CONV_DIM), F32),
            jax.ShapeDtypeStruct((t, 1024), F32),
            jax.ShapeDtypeStruct((t, LANES), F32),
            jax.ShapeDtypeStruct((t, LANES), F32),
        ),
        compiler_params=_params(),
        name="proj_odd",
    )(x, nw, *weights)


def _out_even_kernel(x_ref, og_ref, y5_ref, w_ref, o_ref):
    o_ref[...] = x_ref[...] + (_mm(og_ref[...], w_ref[0:1024, :]) + _mm(y5_ref[...], w_ref[1024:, :]))


def _out_even(x, og, y5, w):
    t = x.shape[0]
    tm = _row_tile(t)
    row = lambda c: pl.BlockSpec((tm, c), lambda i: (i, 0))
    return pl.pallas_call(
        _out_even_kernel,
        grid=(t // tm,),
        in_specs=[row(D_MODEL), row(1024), row(S5_WIDTH), _const_spec(w.shape)],
        out_specs=row(D_MODEL),
        out_shape=jax.ShapeDtypeStruct((t, D_MODEL), F32),
        compiler_params=_params(),
        name="out_even",
    )(x, og, y5, w)


def _out_odd_kernel(x_ref, og_ref, w_ref, fw_ref, o_ref, *slabs, final, nb_out):
    y = x_ref[...] + _mm(og_ref[...], w_ref[...])
    if final:
        y = _rms_rows(y, fw_ref[...])
    if nb_out:
        ys, = slabs
        for c in range(D_MODEL // LANES):
            ys[c] = y[:, c * LANES:(c + 1) * LANES]
        for b in range(nb_out):
            for c in range(D_MODEL // LANES):
                o_ref[b, :, c * LANES:(c + 1) * LANES] = ys[c, pl.ds(b, o_ref.shape[1], stride=nb_out), :]
    else:
        o_ref[...] = y


def _out_odd(x, og, w, fw, final, nb_out=0):
    t = x.shape[0]
    tm = _row_tile(t)
    row = lambda c: pl.BlockSpec((tm, c), lambda i: (i, 0))
    if nb_out:
        out_spec = pl.BlockSpec((nb_out, tm // nb_out, D_MODEL), lambda i: (0, i, 0))
        out_shape = jax.ShapeDtypeStruct((nb_out, t // nb_out, D_MODEL), F32)
        scratch = [pltpu.VMEM((D_MODEL // LANES, tm, LANES), F32)]
    else:
        out_spec, out_shape, scratch = row(D_MODEL), jax.ShapeDtypeStruct((t, D_MODEL), F32), []
    return pl.pallas_call(
        functools.partial(_out_odd_kernel, final=final, nb_out=nb_out),
        grid=(t // tm,),
        in_specs=[row(D_MODEL), row(1024), _const_spec(w.shape), _const_spec(fw.shape)],
        out_specs=out_spec,
        out_shape=out_shape,
        scratch_shapes=scratch,
        compiler_params=_params(),
        name="out_odd",
    )(x, og, w, fw)


def _cumsum_tokens(ref, lead, pad, rows, nb):
    shift = nb
    while shift <= pad:
        ref[lead, pad:pad + rows, :] = ref[lead, pad:pad + rows, :] + ref[lead, pad - shift:pad + rows - shift, :]
        shift *= 2
    while shift < rows:
        ref[lead, pad + shift:pad + rows, :] = (ref[lead, pad + shift:pad + rows, :]
                                                + ref[lead, pad:pad + rows - shift, :])
        shift *= 2


def _gla_prompt_kernel(q_ref, k_ref, g_ref, v_ref, r_ref, far_ref, nw_ref, og_ref, sout_ref,
                       kp, bp, vp, st, oscr, *, nb):
    rows = CHUNK * nb
    pad = PAD_TOKENS * nb
    blk_rows = GLA_BAND * nb
    step = pl.program_id(0)

    @pl.when(step == 0)
    def _():
        st[...] = jnp.zeros_like(st)
        kp[:, 0:pad, :] = jnp.zeros((GLA_HEADS, pad, LANES), F32)
        bp[:, 0:pad, :] = jnp.zeros((GLA_HEADS, pad, LANES), F32)
        vp[:, 0:pad, :] = jnp.zeros((2 * GLA_HEADS, pad, LANES), F32)

    for h in range(GLA_HEADS):
        bp[h, pad:pad + rows, :] = g_ref[h]
        kp[h, pad:pad + rows, :] = k_ref[h]
        _cumsum_tokens(bp, h, pad, rows, nb)
    for s in range(2 * GLA_HEADS):
        vp[s, pad:pad + rows, :] = v_ref[s]

    def band_tile(ti, carry):
        r0 = pl.multiple_of(ti * 64, 64)
        for h in range(GLA_HEADS):
            qt = q_ref[h, pl.ds(r0, 64), :]
            bt = bp[h, pl.ds(pad + r0, 64), :]
            acc0 = jnp.zeros((64, LANES), F32)
            acc1 = jnp.zeros((64, LANES), F32)
            for d in range(GLA_BAND):
                off = pl.multiple_of(pad + r0 - d * nb, 8)
                ks = kp[h, pl.ds(off, 64), :]
                bs = bp[h, pl.ds(off, 64), :]
                w = jnp.sum(qt * ks * jnp.exp(bt - bs), axis=-1, keepdims=True)
                acc0 = acc0 + w * vp[2 * h, pl.ds(off, 64), :]
                acc1 = acc1 + w * vp[2 * h + 1, pl.ds(off, 64), :]
            oscr[2 * h, pl.ds(r0, 64), :] = acc0
            oscr[2 * h + 1, pl.ds(r0, 64), :] = acc1
        return carry

    lax.fori_loop(0, rows // 64, band_tile, 0)

    nblk = CHUNK // GLA_BAND
    for h in range(GLA_HEADS):
        probs = []
        for blk in range(1, nblk):
            lo = blk * blk_rows
            ref_b = bp[h, pad + lo - nb:pad + lo, :]
            qs = q_ref[h, lo:lo + blk_rows, :] * jnp.exp(
                bp[h, pad + lo:pad + lo + blk_rows, :] - jnp.concatenate([ref_b] * GLA_BAND, axis=0))
            ks = kp[h, pad:pad + lo, :] * jnp.exp(
                jnp.concatenate([ref_b] * (GLA_BAND * blk), axis=0) - bp[h, pad:pad + lo, :])
            probs.append(_mm_nt(qs, ks) * far_ref[lo:lo + blk_rows, 0:lo])
        for blk in range(1, nblk):
            lo = blk * blk_rows
            p = probs[blk - 1].astype(BF16)
            for half in range(2):
                sl = 2 * h + half
                oscr[sl, lo:lo + blk_rows, :] = oscr[sl, lo:lo + blk_rows, :] + jnp.dot(
                    p, vp[sl, pad:pad + lo, :].astype(BF16), preferred_element_type=F32)

    def per_batch(b, carry):
        sel = pl.ds(b, CHUNK, stride=nb)
        selp = pl.ds(pad + b, CHUNK, stride=nb)
        loaded = [(q_ref[h, sel, :], kp[h, selp, :], bp[h, selp, :], vp[2 * h, selp, :], vp[2 * h + 1, selp, :],
                   st[b, h]) for h in range(GLA_HEADS)]
        outs = [_mm_nt(qb * jnp.exp(bb), stt) for qb, _, bb, _, _, stt in loaded]
        news = []
        for _, kb, bb, v0, v1, stt in loaded:
            blast = bb[CHUNK - 1:CHUNK, :]
            kd = kb * jnp.exp(blast - bb)
            news.append(stt * jnp.exp(blast) + _mm_tn(jnp.concatenate([v0, v1], axis=1), kd))
        for h in range(GLA_HEADS):
            oscr[2 * h, sel, :] = oscr[2 * h, sel, :] + outs[h][:, :LANES]
            oscr[2 * h + 1, sel, :] = oscr[2 * h + 1, sel, :] + outs[h][:, LANES:]
            st[b, h] = news[h]
        return carry

    lax.fori_loop(0, nb, per_batch, 0)

    nw = nw_ref[...]

    def epi_tile(ti, carry):
        r0 = pl.multiple_of(ti * 64, 64)
        for h in range(GLA_HEADS):
            o0 = oscr[2 * h, pl.ds(r0, 64), :]
            o1 = oscr[2 * h + 1, pl.ds(r0, 64), :]
            ms = (jnp.sum(o0 * o0, axis=-1, keepdims=True) + jnp.sum(o1 * o1, axis=-1, keepdims=True)) / GLA_DV
            inv = lax.rsqrt(ms + EPS)
            c0 = h * GLA_DV
            og_ref[pl.ds(r0, 64), c0:c0 + LANES] = o0 * inv * nw[:, :LANES] * _silu(r_ref[pl.ds(r0, 64), c0:c0 + LANES])
            og_ref[pl.ds(r0, 64), c0 + LANES:c0 + 2 * LANES] = (
                o1 * inv * nw[:, LANES:] * _silu(r_ref[pl.ds(r0, 64), c0 + LANES:c0 + 2 * LANES]))
        return carry

    lax.fori_loop(0, rows // 64, epi_tile, 0)

    @pl.when(step == pl.num_programs(0) - 1)
    def _():
        def wr(b, carry):
            for h in range(GLA_HEADS):
                sout_ref[b, h] = st[b, h].T
            return carry
        lax.fori_loop(0, nb, wr, 0)


def _pair_mask(nb, min_dist):
    n = CHUNK * nb
    r = jnp.arange(n)
    same = (r[:, None] % nb) == (r[None, :] % nb)
    far = (r[None, :] // nb) <= (r[:, None] // nb) - min_dist
    return (same & far).astype(F32)


def _gla_prompt(q, k, g, v, r, nw, nb):
    t = r.shape[0]
    rows = CHUNK * nb
    pad = PAD_TOKENS * nb
    slab = lambda n: pl.BlockSpec((n, rows, LANES), lambda i: (0, i, 0))
    row = lambda c: pl.BlockSpec((rows, c), lambda i: (i, 0))
    far = _pair_mask(nb, GLA_BAND)
    return pl.pallas_call(
        functools.partial(_gla_prompt_kernel, nb=nb),
        grid=(t // rows,),
        in_specs=[slab(4), slab(4), slab(4), slab(8), row(1024), _const_spec(far.shape), _const_spec(nw.shape)],
        out_specs=(row(1024), _const_spec((nb, GLA_HEADS, GLA_DK, GLA_DV))),
        out_shape=(jax.ShapeDtypeStruct((t, 1024), F32),
                   jax.ShapeDtypeStruct((nb, GLA_HEADS, GLA_DK, GLA_DV), F32)),
        scratch_shapes=[
            pltpu.VMEM((GLA_HEADS, pad + rows, LANES), F32),
            pltpu.VMEM((GLA_HEADS, pad + rows, LANES), F32),
            pltpu.VMEM((2 * GLA_HEADS, pad + rows, LANES), F32),
            pltpu.VMEM((nb, GLA_HEADS, GLA_DV, GLA_DK), F32),
            pltpu.VMEM((2 * GLA_HEADS, rows, LANES), F32),
        ],
        compiler_params=_params(),
        name="gla_prompt",
    )(q, k, g, v, r, far, nw)


def _gla_step_kernel(q_ref, k_ref, g_ref, v_ref, r_ref, s_ref, nw_ref, *rest):
    og_ref, sout_ref = rest[-2:]
    nw = nw_ref[...]
    zeros = jnp.zeros((LANES - 8, LANES), F32)
    prow = lax.broadcasted_iota(jnp.int32, (16, 2 * GLA_DV), 0)
    plane = lax.broadcasted_iota(jnp.int32, (16, 2 * GLA_DV), 1)
    pick = [jnp.where(((prow == n) & (plane < GLA_DV)) | ((prow == 8 + n) & (plane >= GLA_DV)), 1.0, 0.0).astype(BF16)
            for n in range(8)]
    for h in range(GLA_HEADS):
        decay = jnp.concatenate([jnp.exp(g_ref[h]), zeros], axis=0).T
        kq = jnp.concatenate([k_ref[h], q_ref[h]], axis=0).astype(BF16)
        cols = [lax.dot_general(kq, pick[n], (((0,), (0,)), ((), ())), preferred_element_type=F32)
                for n in range(8)]
        outs = []
        for n in range(8):
            s_old = s_ref[n, h]
            vrow = jnp.concatenate([v_ref[2 * h, n:n + 1, :], v_ref[2 * h + 1, n:n + 1, :]], axis=1)
            s_new = s_old * decay[:, n:n + 1] + cols[n][:, :GLA_DV] * vrow
            sout_ref[n, h] = s_new
            outs.append(jnp.sum(cols[n][:, GLA_DV:] * s_new, axis=0, keepdims=True))
        o = jnp.concatenate(outs, axis=0)
        ms = jnp.mean(o * o, axis=-1, keepdims=True)
        c0 = h * GLA_DV
        og_ref[:, c0:c0 + GLA_DV] = o * lax.rsqrt(ms + EPS) * nw * _silu(r_ref[:, c0:c0 + GLA_DV])


def _stacked_state_args(states, layer, prev, n_inputs, out_index):
    tail = states.shape[2:]
    spec = pl.BlockSpec((None, 8) + tail, lambda i, _l=layer: (_l, i) + (0,) * len(tail))
    if prev is None:
        prev = jnp.zeros(states.shape, F32)
    return spec, [pl.BlockSpec(memory_space=pl.ANY)], [prev], {n_inputs: out_index}


def _gla_step(q, k, g, v, r, states, layer, prev, nw):
    n = r.shape[0]
    slab = lambda c: pl.BlockSpec((c, 8, LANES), lambda i: (0, i, 0))
    sspec, extra_specs, extra_args, aliases = _stacked_state_args(states, layer, prev, 7, 1)
    return pl.pallas_call(
        _gla_step_kernel,
        grid=(n // 8,),
        in_specs=[slab(4), slab(4), slab(4), slab(8), pl.BlockSpec((8, 1024), lambda i: (i, 0)), sspec,
                  _const_spec(nw.shape)] + extra_specs,
        out_specs=(pl.BlockSpec((8, 1024), lambda i: (i, 0)), sspec),
        out_shape=(jax.ShapeDtypeStruct((n, 1024), F32), jax.ShapeDtypeStruct(states.shape, F32)),
        input_output_aliases=aliases,
        compiler_params=_params(),
        name="gla_step",
    )(q, k, g, v, r, states, nw, *extra_args)


def _s5_prep_kernel(lre_ref, lim_ref, ldt_ref, lre_r_ref, lim_r_ref, ldt_r_ref, bre_ref, bim_ref,
                    ar_ref, ai_ref, bbre_ref, bbim_ref):
    def disc(lre, lim, ldt):
        dt = jnp.exp(ldt)
        mag = jnp.exp(lre * dt)
        ar = mag * jnp.cos(lim * dt)
        ai = mag * jnp.sin(lim * dt)
        return ar, ai

    ar, ai = disc(lre_ref[...], lim_ref[...], ldt_ref[...])
    ar_ref[...] = ar
    ai_ref[...] = ai
    lre, lim = lre_r_ref[...], lim_r_ref[...]
    ar, ai = disc(lre, lim, ldt_r_ref[...])
    den = lre * lre + lim * lim
    wr = ((ar - 1.0) * lre + ai * lim) / den
    wi = (ai * lre - (ar - 1.0) * lim) / den
    bre, bim = bre_ref[...], bim_ref[...]
    bbre_ref[...] = wr * bre - wi * bim
    bbim_ref[...] = wr * bim + wi * bre


def _s5_prep(lam_re, lam_im, log_dt, b_re, b_im):
    n = lam_re.shape[0]
    rows = n * S5_GROUPS
    lre = lam_re.reshape(rows, S5_P)
    lim = lam_im.reshape(rows, S5_P)
    ldt = jnp.broadcast_to(log_dt.reshape(rows, 1), (rows, S5_P))
    rep = lambda a: jnp.repeat(a, S5_GROUP_CH, axis=1)
    args = (lre, lim, ldt, rep(lre), rep(lim), rep(ldt),
            b_re.reshape(rows, S5_P * S5_GROUP_CH), b_im.reshape(rows, S5_P * S5_GROUP_CH))
    wide = jax.ShapeDtypeStruct((rows, S5_P * S5_GROUP_CH), F32)
    narrow = jax.ShapeDtypeStruct((rows, S5_P), F32)
    ar, ai, bbre, bbim = pl.pallas_call(
        _s5_prep_kernel, out_shape=(narrow, narrow, wide, wide), name="s5_prep")(*args)
    shp = (n, S5_GROUPS, S5_P, S5_GROUP_CH)
    return ar.reshape(n, S5_GROUPS, S5_P), ai.reshape(n, S5_GROUPS, S5_P), bbre.reshape(shp), bbim.reshape(shp)


def _s5_kernel(u_ref, sg_ref, h0re_ref, h0im_ref, wbre_ref, wbim_ref, wc_ref, are_ref, aim_ref, d_ref,
               wglu_ref, bglu_ref, y_ref, hre_out, him_out, hre, him, cre, cim, ys, *, nb, tokens):
    step = pl.program_id(0)
    rows = nb * tokens

    @pl.when(step == 0)
    def _():
        cre[...] = h0re_ref[...]
        cim[...] = h0im_ref[...]

    for m in range(4):
        um = u_ref[:, m * LANES:(m + 1) * LANES].astype(BF16)
        hre[:, m * 512:(m + 1) * 512] = jnp.dot(um, wbre_ref[m], preferred_element_type=F32)
        him[:, m * 512:(m + 1) * 512] = jnp.dot(um, wbim_ref[m], preferred_element_type=F32)

    for qd in range(2):
        ql = slice(qd * 1024, (qd + 1) * 1024)
        ar = jnp.broadcast_to(are_ref[:, ql], (nb, 1024))
        ai = jnp.broadcast_to(aim_ref[:, ql], (nb, 1024))

        def tok(t, carry, ql=ql, ar=ar, ai=ai):
            hr, hi = carry
            sel = pl.ds(pl.multiple_of(t * nb, nb), nb)
            nr = ar * hr - ai * hi + hre[sel, ql]
            ni = ar * hi + ai * hr + him[sel, ql]
            hre[sel, ql] = nr
            him[sel, ql] = ni
            return nr, ni

        hr, hi = lax.fori_loop(0, tokens, tok, (cre[:, ql], cim[:, ql]))
        cre[:, ql] = hr
        cim[:, ql] = hi

    for m in range(4):
        hc = jnp.concatenate([hre[:, m * 512:(m + 1) * 512], him[:, m * 512:(m + 1) * 512]], axis=1)
        sl = slice(m * LANES, (m + 1) * LANES)
        y = _mm(hc, wc_ref[m]) + d_ref[:, sl] * u_ref[:, sl]
        ys[:, sl] = jax.nn.gelu(y)
    y = ys[...]
    gate = _sigmoid(_mm(y, wglu_ref[...]) + bglu_ref[...])
    y_ref[...] = y * gate * _silu(sg_ref[...])

    @pl.when(step == pl.num_programs(0) - 1)
    def _():
        hre_out[...] = cre[...]
        him_out[...] = cim[...]


def _s5(u, sg, h0re, h0im, w, nb, tokens):
    t = u.shape[0]
    rows = nb * tokens
    row = lambda c: pl.BlockSpec((rows, c), lambda i: (i, 0))
    weights = (w["wbre"], w["wbim"], w["wc"], w["are"], w["aim"], w["d"], w["wglu"], w["bglu"])
    state = jax.ShapeDtypeStruct((nb, S5_STATE), F32)
    return pl.pallas_call(
        functools.partial(_s5_kernel, nb=nb, tokens=tokens),
        grid=(t // rows,),
        in_specs=[row(S5_WIDTH), row(S5_WIDTH), _const_spec(h0re.shape), _const_spec(h0im.shape)]
        + [_const_spec(a.shape) for a in weights],
        out_specs=(row(S5_WIDTH), _const_spec((nb, S5_STATE)), _const_spec((nb, S5_STATE))),
        out_shape=(jax.ShapeDtypeStruct((t, S5_WIDTH), F32), state, state),
        scratch_shapes=[
            pltpu.VMEM((rows, S5_STATE), F32),
            pltpu.VMEM((rows, S5_STATE), F32),
            pltpu.VMEM((nb, S5_STATE), F32),
            pltpu.VMEM((nb, S5_STATE), F32),
            pltpu.VMEM((rows, S5_WIDTH), F32),
        ],
        compiler_params=_params(),
        name="s5",
    )(u, sg, h0re, h0im, *weights)


def _dn_solve(lms, rhss):
    nblk = CHUNK // SUB
    row = lax.broadcasted_iota(jnp.int32, (SUB, LANES), 0)
    lane = lax.broadcasted_iota(jnp.int32, (SUB, LANES), 1)
    seg = (lane // SUB) * SUB
    in_diag = [(lane >= SUB * blk) & (lane < SUB * (blk + 1)) for blk in range(nblk)]
    eye_pack = jnp.where((lane - seg == row) & (lane < CHUNK), 1.0, 0.0)
    zpad = jnp.zeros((CHUNK, LANES - CHUNK), F32)
    t_invs, lowers = [], []
    for lm in lms:
        wide = jnp.concatenate([lm, zpad], axis=1)
        tiles = [wide[SUB * blk:SUB * (blk + 1)] for blk in range(nblk)]
        dpack = jnp.zeros((SUB, LANES), F32)
        for blk in range(nblk):
            dpack = jnp.where(in_diag[blk], tiles[blk], dpack)
        inv = eye_pack
        for j in range(SUB - 1):
            col_j = jnp.take_along_axis(dpack, seg + j, axis=1)
            inv = inv - col_j * inv[j:j + 1, :]
        t_invs.append(jnp.concatenate(
            [jnp.where(in_diag[blk], inv, 0.0)[:, :CHUNK] for blk in range(nblk)], axis=0).astype(BF16))
        lowers.append(jnp.concatenate(
            [jnp.where(in_diag[blk], 0.0, tiles[blk])[:, :CHUNK] for blk in range(nblk)], axis=0).astype(BF16))
    d = lambda a, b: jnp.dot(a, b, preferred_element_type=F32)
    his = [rhs.astype(BF16) for rhs in rhss]
    los = [(rhs - hi.astype(F32)).astype(BF16) for rhs, hi in zip(rhss, his)]
    trs = [d(t, hi) + d(t, lo_) for t, hi, lo_ in zip(t_invs, his, los)]
    tns = [d(t, lw) for t, lw in zip(t_invs, lowers)]
    xs = [[tr[0:SUB]] for tr in trs]
    for blk in range(1, nblk):
        lo = SUB * blk
        curs = [tr[lo:lo + SUB] - _mm(tn[lo:lo + SUB, 0:lo], jnp.concatenate(x, axis=0))
                for tr, tn, x in zip(trs, tns, xs)]
        for x, cur in zip(xs, curs):
            x.append(cur)
    return [jnp.concatenate(x, axis=0) for x in xs]


def _dn_prompt_kernel(x_ref, z_ref, beta_ref, gd_ref, cw_ref, nw_ref, c0_ref,
                      og_ref, sout_ref, cout_ref,
                      xs, qs, ks, vs, bcol, gcol, gc, gt, oscr, *, nb):
    rows = CHUNK * nb
    hist = (CONV_W - 1) * nb
    pad = PAD_TOKENS * nb
    step = pl.program_id(0)
    st = sout_ref

    @pl.when(step == 0)
    def _():
        st[...] = jnp.zeros_like(st)
        xs[0:hist, :] = c0_ref[...]
        gc[0, 0:pad, :] = jnp.zeros((pad, LANES), F32)

    xs[hist:hist + 64, :] = x_ref[0:64, :]

    def conv_rows(src, base, r0):
        tap = lambda i: base + r0 + i * nb if isinstance(r0, int) else pl.multiple_of(base + r0 + i * nb, 8)
        for c in range(DN_CONV_DIM // LANES):
            cl = slice(c * LANES, (c + 1) * LANES)
            y = src[pl.ds(tap(0), 64), cl] * cw_ref[0:1, cl]
            for i in range(1, CONV_W):
                y = y + src[pl.ds(tap(i), 64), cl] * cw_ref[i:i + 1, cl]
            y = _silu(y)
            if c < 2 * DN_HEADS:
                y = y * lax.rsqrt(jnp.sum(y * y, axis=-1, keepdims=True) + EPS)
            if c < DN_HEADS:
                qs[c, pl.ds(r0, 64), :] = y * (DN_DK ** -0.5)
            elif c < 2 * DN_HEADS:
                ks[c - DN_HEADS, pl.ds(r0, 64), :] = y
            else:
                vs[c - 2 * DN_HEADS, pl.ds(r0, 64), :] = y

    conv_rows(xs, 0, 0)

    def conv_tile(ti, carry):
        conv_rows(x_ref, -hist, pl.multiple_of(ti * 64, 64))
        return carry

    lax.fori_loop(1, rows // 64, conv_tile, 0)

    xs[0:hist, :] = x_ref[rows - hist:rows, :]

    gc[0, pad:pad + rows, :] = gd_ref[...]
    _cumsum_tokens(gc, 0, pad, rows, nb)

    def bc_tile(ti, carry):
        r0 = pl.multiple_of(ti * 64, 64)
        gtile = gc[0, pl.ds(pad + r0, 64), :]
        btile = beta_ref[pl.ds(r0, 64), :]
        for h in range(DN_HEADS):
            gcol[h, pl.ds(r0, 64), :] = jnp.broadcast_to(gtile[:, h:h + 1], (64, LANES))
            bcol[h, pl.ds(r0, 64), :] = jnp.broadcast_to(btile[:, h:h + 1], (64, LANES))
        return carry

    lax.fori_loop(0, rows // 64, bc_tile, 0)

    ii = lax.broadcasted_iota(jnp.int32, (CHUNK, CHUNK), 0)
    jj = lax.broadcasted_iota(jnp.int32, (CHUNK, CHUNK), 1)
    eye = (ii == jj).astype(BF16)

    parts = [_split3(gc[0, pl.ds(pad + b, CHUNK, stride=nb), :]) for b in range(nb)]
    tdot = lambda x: lax.dot_general(x, eye, (((0,), (0,)), ((), ())), preferred_element_type=F32)
    firsts = [tdot(p[0]) for p in parts]
    seconds = [tdot(p[1]) for p in parts]
    thirds = [tdot(p[2]) for p in parts]
    for b in range(nb):
        gt[b] = (firsts[b] + seconds[b]) + thirds[b]

    def per_batch(bi, carry):
        def head_group(gi, carry2):
            probs = [(bi * DN_BATCH + jb, gi * DN_GROUP + jh) for jb in range(DN_BATCH) for jh in range(DN_GROUP)]
            sels = [pl.ds(b, CHUNK, stride=nb) for b, _ in probs]
            loaded = [(qs[h, sel, :], ks[h, sel, :], vs[h, sel, :], bcol[h, sel, :], gcol[h, sel, :],
                       gt[b, pl.ds(h, 1), :], st[b, h]) for (b, h), sel in zip(probs, sels)]
            qbs, kbs, vbs, bcs, gcls, grows, s_olds = zip(*loaded)
            decs = [jnp.where(jj <= ii, jnp.exp(jnp.where(jj <= ii, gcl[:, :CHUNK] - grow, 0.0)), 0.0)
                    for gcl, grow in zip(gcls, grows)]
            kks = [_mm_nt(kb, kb) for kb in kbs]
            qks = [_mm_nt(qb, kb) for qb, kb in zip(qbs, kbs)]
            lms = [jnp.where(jj < ii, bc[:, :CHUNK] * kk * dec, 0.0) for bc, kk, dec in zip(bcs, kks, decs)]
            egs = [jnp.exp(gcl) for gcl in gcls]
            rhss = [jnp.concatenate([vb * bc, kb * (bc * eg)], axis=1)
                    for vb, kb, bc, eg in zip(vbs, kbs, bcs, egs)]
            sols = _dn_solve(lms, rhss)
            v_news = [sol[:, :DN_DV] - _mm(sol[:, DN_DV:], s_old) for sol, s_old in zip(sols, s_olds)]
            os_ = [_mm(qb * eg, s_old) + _mm(qk * dec, v_new)
                   for qb, eg, s_old, qk, dec, v_new in zip(qbs, egs, s_olds, qks, decs, v_news)]
            glasts = [gcl[CHUNK - 1:CHUNK, :] for gcl in gcls]
            s_news = [s_old * jnp.exp(glast) + _mm_tn(kb * jnp.exp(glast - gcl), v_new)
                      for s_old, glast, kb, gcl, v_new in zip(s_olds, glasts, kbs, gcls, v_news)]
            for (b, h), sel, o, s_new in zip(probs, sels, os_, s_news):
                st[b, h] = s_new
                oscr[h, sel, :] = o
            return carry2

        lax.fori_loop(0, DN_HEADS // DN_GROUP, head_group, 0)
        return carry

    lax.fori_loop(0, nb // DN_BATCH, per_batch, 0)

    nw = nw_ref[...]

    def epi_tile(ti, carry):
        r0 = pl.multiple_of(ti * 64, 64)
        for h in range(DN_HEADS):
            o = oscr[h, pl.ds(r0, 64), :]
            inv = lax.rsqrt(jnp.mean(o * o, axis=-1, keepdims=True) + EPS)
            cl = slice(h * LANES, (h + 1) * LANES)
            og_ref[pl.ds(r0, 64), cl] = o * inv * nw * _silu(z_ref[pl.ds(r0, 64), cl])
        return carry

    lax.fori_loop(0, rows // 64, epi_tile, 0)

    @pl.when(step == pl.num_programs(0) - 1)
    def _():
        cout_ref[...] = xs[0:hist, :]


def _dn_prompt(xqkv, z, beta, gd, cw, nw, c0, nb):
    t = z.shape[0]
    rows = CHUNK * nb
    hist = (CONV_W - 1) * nb
    row = lambda c: pl.BlockSpec((rows, c), lambda i: (i, 0))
    slab = pltpu.VMEM((DN_HEADS, rows, LANES), F32)
    return pl.pallas_call(
        functools.partial(_dn_prompt_kernel, nb=nb),
        grid=(t // rows,),
        in_specs=[row(DN_CONV_DIM), row(1024), row(LANES), row(LANES), _const_spec(cw.shape),
                  _const_spec(nw.shape), _const_spec(c0.shape)],
        out_specs=(row(1024), _const_spec((nb, DN_HEADS, DN_DK, DN_DV)), _const_spec((hist, DN_CONV_DIM))),
        out_shape=(jax.ShapeDtypeStruct((t, 1024), F32),
                   jax.ShapeDtypeStruct((nb, DN_HEADS, DN_DK, DN_DV), F32),
                   jax.ShapeDtypeStruct((hist, DN_CONV_DIM), F32)),
        scratch_shapes=[
            pltpu.VMEM((hist + 64, DN_CONV_DIM), F32),
            slab, slab, slab, slab, slab,
            pltpu.VMEM((1, PAD_TOKENS * nb + rows, LANES), F32),
            pltpu.VMEM((nb, LANES, CHUNK), F32),
            slab,
        ],
        compiler_params=_params(),
        name="dn_prompt",
    )(xqkv, z, beta, gd, cw, nw, c0)


def _dn_step_kernel(x_ref, cb_ref, z_ref, beta_ref, gd_ref, s_ref, cw_ref, nw_ref, *rest):
    og_ref, sout_ref, cn_ref = rest[-3:]
    nw = nw_ref[...]
    cn_ref[0] = cb_ref[1]
    cn_ref[1] = cb_ref[2]
    cn_ref[2] = x_ref[...]
    beta = beta_ref[...]
    eg_all = jnp.exp(gd_ref[...])

    def conv(c):
        cl = slice(c * LANES, (c + 1) * LANES)
        y = cb_ref[0, :, cl] * cw_ref[0:1, cl]
        y = y + cb_ref[1, :, cl] * cw_ref[1:2, cl]
        y = y + cb_ref[2, :, cl] * cw_ref[2:3, cl]
        y = y + x_ref[:, cl] * cw_ref[3:4, cl]
        return _silu(y)

    def l2(y):
        return y * lax.rsqrt(jnp.sum(y * y, axis=-1, keepdims=True) + EPS)

    prow = lax.broadcasted_iota(jnp.int32, (16, 2 * LANES), 0)
    plane = lax.broadcasted_iota(jnp.int32, (16, 2 * LANES), 1)
    pick = [jnp.where(((prow == n) & (plane < LANES)) | ((prow == 8 + n) & (plane >= LANES)), 1.0, 0.0).astype(BF16)
            for n in range(8)]

    for h in range(DN_HEADS):
        qh = l2(conv(h)) * (DN_DK ** -0.5)
        kh = l2(conv(DN_HEADS + h))
        vh = conv(2 * DN_HEADS + h)
        kq = jnp.concatenate([kh, qh], axis=0).astype(BF16)
        cols = [lax.dot_general(kq, pick[n], (((0,), (0,)), ((), ())), preferred_element_type=F32)
                for n in range(8)]
        outs = []
        for n in range(8):
            s_old = s_ref[n, h]
            kcol = cols[n][:, :LANES]
            eg = eg_all[n:n + 1, h:h + 1]
            ks_row = jnp.sum(kcol * s_old, axis=0, keepdims=True)
            v_new = beta[n:n + 1, h:h + 1] * (vh[n:n + 1, :] - eg * ks_row)
            s_new = s_old * eg + kcol * v_new
            sout_ref[n, h] = s_new
            outs.append(jnp.sum(cols[n][:, LANES:] * s_new, axis=0, keepdims=True))
        o = jnp.concatenate(outs, axis=0)
        cl = slice(h * LANES, (h + 1) * LANES)
        og_ref[:, cl] = o * lax.rsqrt(jnp.mean(o * o, axis=-1, keepdims=True) + EPS) * nw * _silu(z_ref[:, cl])


def _dn_step(xqkv, cbuf, z, beta, gd, states, layer, prev, cw, nw):
    n = z.shape[0]
    row = lambda c: pl.BlockSpec((8, c), lambda i: (i, 0))
    cspec = pl.BlockSpec((CONV_W - 1, 8, DN_CONV_DIM), lambda i: (0, i, 0))
    sspec, extra_specs, extra_args, aliases = _stacked_state_args(states, layer, prev, 8, 1)
    return pl.pallas_call(
        _dn_step_kernel,
        grid=(n // 8,),
        in_specs=[row(DN_CONV_DIM), cspec, row(1024), row(LANES), row(LANES), sspec,
                  _const_spec(cw.shape), _const_spec(nw.shape)] + extra_specs,
        out_specs=(row(1024), sspec, cspec),
        out_shape=(jax.ShapeDtypeStruct((n, 1024), F32), jax.ShapeDtypeStruct(states.shape, F32),
                   jax.ShapeDtypeStruct(cbuf.shape, F32)),
        input_output_aliases=aliases,
        compiler_params=_params(),
        name="dn_step",
    )(xqkv, cbuf, z, beta, gd, states, cw, nw, *extra_args)


def _pad_cols(a, n):
    return jnp.pad(a, ((0, 0), (0, n - a.shape[1])))


def _even_weights(w_in, w_gate_up, b_gate, w_out):
    return {
        "w": _pad_cols(w_in.astype(BF16), W_IN_PAD),
        "wg": jnp.pad(w_gate_up, ((0, LANES - GLA_RANK), (0, 0))).astype(BF16),
        "bg": b_gate.reshape(1, -1),
        "wo": w_out.astype(BF16),
    }


def _s5_weights(ar, ai, bbre, bbim, c_re, c_im, d, w_glu, b_glu):
    eye = jnp.eye(8, dtype=F32)

    def pack_b(bb):
        t = bb.reshape(4, 8, S5_P, S5_GROUP_CH).transpose(0, 1, 3, 2)
        return (t[:, :, :, None, :] * eye[None, :, None, :, None]).reshape(4, 128, 512).astype(BF16)

    def pack_c(c):
        t = c.reshape(4, 8, S5_GROUP_CH, S5_P).transpose(0, 1, 3, 2)
        return (t[:, :, :, None, :] * eye[None, :, None, :, None]).reshape(4, 512, 128)

    return {
        "wbre": pack_b(bbre), "wbim": pack_b(bbim),
        "wc": jnp.concatenate([pack_c(c_re), -pack_c(c_im)], axis=1).astype(BF16),
        "are": ar.reshape(1, S5_STATE), "aim": ai.reshape(1, S5_STATE),
        "d": d.reshape(1, -1), "wglu": w_glu.astype(BF16), "bglu": b_glu.reshape(1, -1),
    }


def _odd_weights(w_in, conv_w, a_log, dt_bias, w_out):
    return {
        "w": _pad_cols(w_in.astype(BF16), W_IN_PAD),
        "alog": _pad_cols(a_log.reshape(1, -1), LANES), "dtb": _pad_cols(dt_bias.reshape(1, -1), LANES),
        "cw": jnp.pad(conv_w, ((0, 8 - CONV_W), (0, 0))),
        "wo": w_out.astype(BF16),
    }


def kernel(x_prompt, x_sample, state_gla, state_s5_re, state_s5_im, state_delta, state_conv, norm_w, final_norm_w, w_in_even, gla_w_gate_up, gla_b_gate, gla_norm_w, s5_lambda_re, s5_lambda_im, s5_log_dt, s5_b_re, s5_b_im, s5_c_re, s5_c_im, s5_d, s5_w_glu, s5_b_glu, w_out_even, w_in_odd, dn_conv_w, dn_a_log, dn_dt_bias, dn_norm_w, w_out_odd):
    nb, seq, _ = x_prompt.shape
    ns = x_sample.shape[0]
    assert seq % CHUNK == 0 and nb % 8 == 0 and ns % 8 == 0 and x_sample.shape[1] == 1
    depth = norm_w.shape[0]

    assert depth % 2 == 0 and state_gla.shape[0] == depth // 2 and state_delta.shape[0] == depth // 2
    xp = None
    xs = x_sample.reshape(ns, D_MODEL)
    fw = final_norm_w.reshape(1, -1)

    ar, ai, bbre, bbim = _s5_prep(s5_lambda_re, s5_lambda_im, s5_log_dt, s5_b_re, s5_b_im)

    gla_p, s5r_p, s5i_p, s5r_s, s5i_s, dn_p, cv_p, cv_s = ([] for _ in range(8))
    gla_s = dn_s = None
    for layer in range(depth):
        i = layer // 2
        nw = norm_w[layer].reshape(1, -1)
        if layer % 2 == 0:
            w = _even_weights(w_in_even[i], gla_w_gate_up[i], gla_b_gate[i], w_out_even[i])
            w5 = _s5_weights(ar[i], ai[i], bbre[i], bbim[i], s5_c_re[i], s5_c_im[i], s5_d[i],
                             s5_w_glu[i], s5_b_glu[i])
            gnw = gla_norm_w[i].reshape(1, -1)
            if layer == 0:
                q, k, g, v, r, u, sg, xp = _proj_even(x_prompt, nw, w, batch_major=True)
            else:
                q, k, g, v, r, u, sg = _proj_even(xp, nw, w)
            og, s_new = _gla_prompt(q, k, g, v, r, gnw, nb)
            zero = jnp.zeros((nb, S5_STATE), F32)
            y5, hre, him = _s5(u, sg, zero, zero, w5, nb, CHUNK)
            xp = _out_even(xp, og, y5, w["wo"])
            gla_p.append(s_new)
            s5r_p.append(hre.reshape(nb, S5_GROUPS, S5_P))
            s5i_p.append(him.reshape(nb, S5_GROUPS, S5_P))
            q, k, g, v, r, u, sg = _proj_even(xs, nw, w)
            og, gla_s = _gla_step(q, k, g, v, r, state_gla, i, gla_s, gnw)
            y5, hre, him = _s5(u, sg, state_s5_re[i].reshape(ns, S5_STATE), state_s5_im[i].reshape(ns, S5_STATE),
                               w5, ns, 1)
            xs = _out_even(xs, og, y5, w["wo"])
            s5r_s.append(hre.reshape(ns, S5_GROUPS, S5_P))
            s5i_s.append(him.reshape(ns, S5_GROUPS, S5_P))
        else:
            w = _odd_weights(w_in_odd[i], dn_conv_w[i], dn_a_log[i], dn_dt_bias[i], w_out_odd[i])
            dnw = dn_norm_w[i].reshape(1, -1)
            final = layer == depth - 1
            xqkv, z, beta, gd = _proj_odd(xp, nw, w)
            c0 = jnp.zeros(((CONV_W - 1) * nb, DN_CONV_DIM), F32)
            og, s_new, c_new = _dn_prompt(xqkv, z, beta, gd, w["cw"], dnw, c0, nb)
            xp = _out_odd(xp, og, w["wo"], fw, final, nb_out=nb if final else 0)
            dn_p.append(s_new)
            cv_p.append(c_new.reshape(CONV_W - 1, nb, DN_CONV_DIM).transpose(1, 0, 2))
            xqkv, z, beta, gd = _proj_odd(xs, nw, w)
            og, dn_s, c_new = _dn_step(xqkv, state_conv[i].transpose(1, 0, 2), z, beta, gd, state_delta, i, dn_s,
                                       w["cw"], dnw)
            xs = _out_odd(xs, og, w["wo"], fw, final)
            cv_s.append(c_new.transpose(1, 0, 2))

    y_prompt = xp
    y_sample = xs.reshape(ns, 1, D_MODEL)
    st = jnp.stack
    return (y_prompt, y_sample, st(gla_p), gla_s, st(s5r_p), st(s5i_p), st(s5r_s), st(s5i_s),
            st(dn_p), dn_s, st(cv_p), st(cv_s))
```

```python
import functools
import math

import jax
import jax.numpy as jnp
from jax import lax
from jax.experimental import pallas as pl
from jax.experimental.pallas import tpu as pltpu

F32 = jnp.float32
BF16 = jnp.bfloat16
EPS = 1e-6

D_MODEL = 1024
GLA_HEADS, GLA_DK, GLA_DV, GLA_RANK = 4, 128, 256, 16
GLA_GATE_NORM = 16.0
S5_GROUPS, S5_GROUP_CH, S5_P, S5_WIDTH = 32, 16, 64, 512
S5_STATE = S5_GROUPS * S5_P
DN_HEADS, DN_DK, DN_DV = 8, 128, 128
DN_CONV_DIM, CONV_W = 3072, 4
CHUNK = 64
SUB = 8
GLA_BAND = 8
GLA_SAFE_SPAN = 60.0
PAD_TOKENS = 16
DN_BATCH = 4
DN_GROUP = 8
LANES = 128
W_IN_PAD = 4224
VMEM_LIMIT = 56 * 1024 * 1024


def _mm(a, b):
    return jnp.dot(a.astype(BF16), b.astype(BF16), preferred_element_type=F32)


def _mm_nt(a, b):
    return lax.dot_general(a.astype(BF16), b.astype(BF16), (((1,), (1,)), ((), ())),
                           preferred_element_type=F32)


def _mm_tn(a, b):
    return lax.dot_general(a.astype(BF16), b.astype(BF16), (((0,), (0,)), ((), ())),
                           preferred_element_type=F32)


def _split3(a):
    a1 = a.astype(BF16)
    r1 = a - a1.astype(F32)
    a2 = r1.astype(BF16)
    a3 = (r1 - a2.astype(F32)).astype(BF16)
    return a1, a2, a3


def _mm_tn_exact(a, m_bf16):
    a1, a2, a3 = _split3(a)
    d = lambda x: lax.dot_general(x, m_bf16, (((0,), (0,)), ((), ())), preferred_element_type=F32)
    return (d(a1) + d(a2)) + d(a3)


def _sigmoid(x):
    return 1.0 / (1.0 + jnp.exp(-x))


def _silu(x):
    return x * _sigmoid(x)


def _softplus(x):
    return jnp.maximum(x, 0.0) + jnp.log1p(jnp.exp(-jnp.abs(x)))


def _rms_rows(x, w):
    ms = jnp.mean(x * x, axis=-1, keepdims=True)
    return x * lax.rsqrt(ms + EPS) * w


def _const_spec(shape):
    nd = len(shape)
    return pl.BlockSpec(shape, lambda i, _nd=nd: (0,) * _nd)


def _params(**flags):
    return pltpu.CompilerParams(dimension_semantics=("arbitrary",), vmem_limit_bytes=VMEM_LIMIT,
                                flags=flags or None)


def _row_tile(t):
    return min(512, t)


def _proj_even_kernel(x_ref, nw_ref, w_ref, wg_ref, bg_ref, q_ref, k_ref, g_ref, v_ref, r_ref, u_ref, sg_ref,
                      *relayout, nb_in):
    if nb_in:
        xo_ref, xt = relayout
        for b in range(nb_in):
            for c in range(D_MODEL // LANES):
                xt[c, pl.ds(b, x_ref.shape[1], stride=nb_in), :] = x_ref[b, :, c * LANES:(c + 1) * LANES]
        x = jnp.concatenate([xt[c] for c in range(D_MODEL // LANES)], axis=1)
        xo_ref[...] = x
    else:
        x = x_ref[...]
    hb = _rms_rows(x, nw_ref[...]).astype(BF16)
    d = lambda lo, hi: jnp.dot(hb, w_ref[:, lo:hi], preferred_element_type=F32)
    q = d(0, 512) * (GLA_DK ** -0.5)
    k = d(512, 1024)
    lr_u = d(3072, 3712)
    u_sg = d(3584, 4224)
    logit = jnp.dot(lr_u[:, :LANES].astype(BF16), wg_ref[...], preferred_element_type=F32) + bg_ref[...]
    g = -_softplus(-logit) / GLA_GATE_NORM
    for h in range(GLA_HEADS):
        sl = slice(h * LANES, (h + 1) * LANES)
        q_ref[h] = q[:, sl]
        k_ref[h] = k[:, sl]
        g_ref[h] = g[:, sl]
    v = d(1024, 2048)
    for s in range(2 * GLA_HEADS):
        v_ref[s] = v[:, s * LANES:(s + 1) * LANES]
    r_ref[...] = d(2048, 3072)
    u_ref[...] = lr_u[:, GLA_RANK:GLA_RANK + S5_WIDTH]
    sg_ref[...] = u_sg[:, GLA_RANK:GLA_RANK + S5_WIDTH]


def _proj_even(x, nw, w, batch_major=False):
    nb_in = x.shape[0] if batch_major else 0
    t = x.shape[0] * x.shape[1] if batch_major else x.shape[0]
    tm = _row_tile(t)
    row = lambda c: pl.BlockSpec((tm, c), lambda i: (i, 0))
    slab = lambda n: pl.BlockSpec((n, tm, LANES), lambda i: (0, i, 0))
    x_spec = pl.BlockSpec((nb_in, tm // nb_in, D_MODEL), lambda i: (0, i, 0)) if batch_major else row(D_MODEL)
    extra_out_shape = (jax.ShapeDtypeStruct((t, D_MODEL), F32),) if batch_major else ()
    extra_out_specs = (row(D_MODEL),) if batch_major else ()
    scratch = [pltpu.VMEM((D_MODEL // LANES, tm, LANES), F32)] if batch_major else []
    out_shape = (
        jax.ShapeDtypeStruct((GLA_HEADS, t, LANES), F32),
        jax.ShapeDtypeStruct((GLA_HEADS, t, LANES), F32),
        jax.ShapeDtypeStruct((GLA_HEADS, t, LANES), F32),
        jax.ShapeDtypeStruct((2 * GLA_HEADS, t, LANES), F32),
        jax.ShapeDtypeStruct((t, 1024), F32),
        jax.ShapeDtypeStruct((t, S5_WIDTH), F32),
        jax.ShapeDtypeStruct((t, S5_WIDTH), F32),
    )
    weights = (w["w"], w["wg"], w["bg"])
    return pl.pallas_call(
        functools.partial(_proj_even_kernel, nb_in=nb_in),
        grid=(t // tm,),
        in_specs=[x_spec, _const_spec(nw.shape)] + [_const_spec(a.shape) for a in weights],
        out_specs=(slab(4), slab(4), slab(4), slab(8), row(1024), row(S5_WIDTH), row(S5_WIDTH)) + extra_out_specs,
        out_shape=out_shape + extra_out_shape,
        scratch_shapes=scratch,
        compiler_params=_params(),
        name="proj_even",
    )(x, nw, *weights)


def _proj_odd_kernel(x_ref, nw_ref, w_ref, alog_ref, dtb_ref, xqkv_ref, z_ref, beta_ref, gd_ref):
    hb = _rms_rows(x_ref[...], nw_ref[...]).astype(BF16)
    d = lambda lo, hi: jnp.dot(hb, w_ref[:, lo:hi], preferred_element_type=F32)
    xqkv_ref[...] = d(0, DN_CONV_DIM)
    z_ref[...] = d(DN_CONV_DIM, DN_CONV_DIM + 1024)
    tail = d(DN_CONV_DIM + 1024, DN_CONV_DIM + 1024 + LANES)
    beta_ref[...] = _sigmoid(tail)
    a = pltpu.roll(tail, LANES - DN_HEADS, 1)
    gd_ref[...] = -jnp.exp(alog_ref[...]) * _softplus(a + dtb_ref[...])


def _proj_odd(x, nw, w):
    t = x.shape[0]
    tm = _row_tile(t)
    row = lambda c: pl.BlockSpec((tm, c), lambda i: (i, 0))
    weights = (w["w"], w["alog"], w["dtb"])
    return pl.pallas_call(
        _proj_odd_kernel,
        grid=(t // tm,),
        in_specs=[row(D_MODEL), _const_spec(nw.shape)] + [_const_spec(a.shape) for a in weights],
        out_specs=(row(DN_CONV_DIM), row(1024), row(LANES), row(LANES)),
        out_shape=(
            jax.ShapeDtypeStruct((t, DN_CONV_DIM), F32),
            jax.ShapeDtypeStruct((t, 1024), F32),
            jax.ShapeDtypeStruct((t, LANES), F32),
            jax.ShapeDtypeStruct((t, LANES), F32),
        ),
        compiler_params=_params(),
        name="proj_odd",
    )(x, nw, *weights)


def _out_even_kernel(x_ref, og_ref, y5_ref, w_ref, o_ref):
    o_ref[...] = x_ref[...] + (_mm(og_ref[...], w_ref[0:1024, :]) + _mm(y5_ref[...], w_ref[1024:, :]))


def _out_even(x, og, y5, w):
    t = x.shape[0]
    tm = _row_tile(t)
    row = lambda c: pl.BlockSpec((tm, c), lambda i: (i, 0))
    return pl.pallas_call(
        _out_even_kernel,
        grid=(t // tm,),
        in_specs=[row(D_MODEL), row(1024), row(S5_WIDTH), _const_spec(w.shape)],
        out_specs=row(D_MODEL),
        out_shape=jax.ShapeDtypeStruct((t, D_MODEL), F32),
        compiler_params=_params(),
        name="out_even",
    )(x, og, y5, w)


def _out_odd_kernel(x_ref, og_ref, w_ref, fw_ref, o_ref, *slabs, final, nb_out):
    y = x_ref[...] + _mm(og_ref[...], w_ref[...])
    if final:
        y = _rms_rows(y, fw_ref[...])
    if nb_out:
        ys, = slabs
        for c in range(D_MODEL // LANES):
            ys[c] = y[:, c * LANES:(c + 1) * LANES]
        for b in range(nb_out):
            for c in range(D_MODEL // LANES):
                o_ref[b, :, c * LANES:(c + 1) * LANES] = ys[c, pl.ds(b, o_ref.shape[1], stride=nb_out), :]
    else:
        o_ref[...] = y


def _out_odd(x, og, w, fw, final, nb_out=0):
    t = x.shape[0]
    tm = _row_tile(t)
    row = lambda c: pl.BlockSpec((tm, c), lambda i: (i, 0))
    if nb_out:
        out_spec = pl.BlockSpec((nb_out, tm // nb_out, D_MODEL), lambda i: (0, i, 0))
        out_shape = jax.ShapeDtypeStruct((nb_out, t // nb_out, D_MODEL), F32)
        scratch = [pltpu.VMEM((D_MODEL // LANES, tm, LANES), F32)]
    else:
        out_spec, out_shape, scratch = row(D_MODEL), jax.ShapeDtypeStruct((t, D_MODEL), F32), []
    return pl.pallas_call(
        functools.partial(_out_odd_kernel, final=final, nb_out=nb_out),
        grid=(t // tm,),
        in_specs=[row(D_MODEL), row(1024), _const_spec(w.shape), _const_spec(fw.shape)],
        out_specs=out_spec,
        out_shape=out_shape,
        scratch_shapes=scratch,
        compiler_params=_params(),
        name="out_odd",
    )(x, og, w, fw)


def _cumsum_tokens(ref, lead, pad, rows, nb):
    shift = nb
    while shift <= pad:
        ref[lead, pad:pad + rows, :] = ref[lead, pad:pad + rows, :] + ref[lead, pad - shift:pad + rows - shift, :]
        shift *= 2
    while shift < rows:
        ref[lead, pad + shift:pad + rows, :] = (ref[lead, pad + shift:pad + rows, :]
                                                + ref[lead, pad:pad + rows - shift, :])
        shift *= 2


def _gla_prompt_kernel(q_ref, k_ref, g_ref, v_ref, r_ref, far_ref, near_ref, nw_ref, og_ref, sout_ref,
                       kp, bp, vp, st, oscr, *, nb):
    rows = CHUNK * nb
    pad = PAD_TOKENS * nb
    blk_rows = GLA_BAND * nb
    step = pl.program_id(0)

    @pl.when(step == 0)
    def _():
        st[...] = jnp.zeros_like(st)
        kp[:, 0:pad, :] = jnp.zeros((GLA_HEADS, pad, LANES), F32)
        bp[:, 0:pad, :] = jnp.zeros((GLA_HEADS, pad, LANES), F32)
        vp[:, 0:pad, :] = jnp.zeros((2 * GLA_HEADS, pad, LANES), F32)

    for h in range(GLA_HEADS):
        bp[h, pad:pad + rows, :] = g_ref[h]
        kp[h, pad:pad + rows, :] = k_ref[h]
        _cumsum_tokens(bp, h, pad, rows, nb)
    for s in range(2 * GLA_HEADS):
        vp[s, pad:pad + rows, :] = v_ref[s]

    nblk = CHUNK // GLA_BAND

    span = jnp.zeros((nb, LANES), F32)
    for h in range(GLA_HEADS):
        for blk in range(nblk):
            lo = pad + blk * blk_rows
            span = jnp.maximum(span, bp[h, lo - nb:lo, :] - bp[h, lo + blk_rows - nb:lo + blk_rows, :])
    safe = jnp.max(span) <= GLA_SAFE_SPAN

    def pairs_on_mxu(first_blk, mask_ref, accumulate):
        for h in range(GLA_HEADS):
            probs = []
            for blk in range(first_blk, nblk):
                lo = blk * blk_rows
                hi = lo + (1 - first_blk) * blk_rows
                ref_b = bp[h, pad + lo - nb:pad + lo, :]
                qs = q_ref[h, lo:lo + blk_rows, :] * jnp.exp(
                    bp[h, pad + lo:pad + lo + blk_rows, :] - jnp.concatenate([ref_b] * GLA_BAND, axis=0))
                ks = kp[h, pad:pad + hi, :] * jnp.exp(
                    jnp.concatenate([ref_b] * (hi // nb), axis=0) - bp[h, pad:pad + hi, :])
                probs.append((hi, _mm_nt(qs, ks) * mask_ref[lo:lo + blk_rows, 0:hi]))
            for blk in range(first_blk, nblk):
                lo = blk * blk_rows
                hi, p = probs[blk - first_blk]
                p = p.astype(BF16)
                for half in range(2):
                    sl = 2 * h + half
                    pv = jnp.dot(p, vp[sl, pad:pad + hi, :].astype(BF16), preferred_element_type=F32)
                    oscr[sl, lo:lo + blk_rows, :] = oscr[sl, lo:lo + blk_rows, :] + pv if accumulate else pv

    @pl.when(safe)
    def _():
        pairs_on_mxu(0, near_ref, False)

    def band_tile(ti, carry):
        r0 = pl.multiple_of(ti * 64, 64)
        for h in range(GLA_HEADS):
            qt = q_ref[h, pl.ds(r0, 64), :]
            bt = bp[h, pl.ds(pad + r0, 64), :]
            acc0 = jnp.zeros((64, LANES), F32)
            acc1 = jnp.zeros((64, LANES), F32)
            for d in range(GLA_BAND):
                off = pl.multiple_of(pad + r0 - d * nb, 8)
                ks = kp[h, pl.ds(off, 64), :]
                bs = bp[h, pl.ds(off, 64), :]
                w = jnp.sum(qt * ks * jnp.exp(bt - bs), axis=-1, keepdims=True)
                acc0 = acc0 + w * vp[2 * h, pl.ds(off, 64), :]
                acc1 = acc1 + w * vp[2 * h + 1, pl.ds(off, 64), :]
            oscr[2 * h, pl.ds(r0, 64), :] = acc0
            oscr[2 * h + 1, pl.ds(r0, 64), :] = acc1
        return carry

    @pl.when(jnp.logical_not(safe))
    def _():
        lax.fori_loop(0, rows // 64, band_tile, 0)
        pairs_on_mxu(1, far_ref, True)

    def per_batch(b, carry):
        sel = pl.ds(b, CHUNK, stride=nb)
        selp = pl.ds(pad + b, CHUNK, stride=nb)
        loaded = [(q_ref[h, sel, :], kp[h, selp, :], bp[h, selp, :], vp[2 * h, selp, :], vp[2 * h + 1, selp, :],
                   st[b, h]) for h in range(GLA_HEADS)]
        outs = [_mm_nt(qb * jnp.exp(bb), stt) for qb, _, bb, _, _, stt in loaded]
        news = []
        for _, kb, bb, v0, v1, stt in loaded:
            blast = bb[CHUNK - 1:CHUNK, :]
            kd = kb * jnp.exp(blast - bb)
            news.append(stt * jnp.exp(blast) + _mm_tn(jnp.concatenate([v0, v1], axis=1), kd))
        for h in range(GLA_HEADS):
            oscr[2 * h, sel, :] = oscr[2 * h, sel, :] + outs[h][:, :LANES]
            oscr[2 * h + 1, sel, :] = oscr[2 * h + 1, sel, :] + outs[h][:, LANES:]
            st[b, h] = news[h]
        return carry

    lax.fori_loop(0, nb, per_batch, 0)

    nw = nw_ref[...]

    def epi_tile(ti, carry):
        r0 = pl.multiple_of(ti * 64, 64)
        for h in range(GLA_HEADS):
            o0 = oscr[2 * h, pl.ds(r0, 64), :]
            o1 = oscr[2 * h + 1, pl.ds(r0, 64), :]
            ms = (jnp.sum(o0 * o0, axis=-1, keepdims=True) + jnp.sum(o1 * o1, axis=-1, keepdims=True)) / GLA_DV
            inv = lax.rsqrt(ms + EPS)
            c0 = h * GLA_DV
            og_ref[pl.ds(r0, 64), c0:c0 + LANES] = o0 * inv * nw[:, :LANES] * _silu(r_ref[pl.ds(r0, 64), c0:c0 + LANES])
            og_ref[pl.ds(r0, 64), c0 + LANES:c0 + 2 * LANES] = (
                o1 * inv * nw[:, LANES:] * _silu(r_ref[pl.ds(r0, 64), c0 + LANES:c0 + 2 * LANES]))
        return carry

    lax.fori_loop(0, rows // 64, epi_tile, 0)

    @pl.when(step == pl.num_programs(0) - 1)
    def _():
        def wr(b, carry):
            for h in range(GLA_HEADS):
                sout_ref[b, h] = st[b, h].T
            return carry
        lax.fori_loop(0, nb, wr, 0)


def _pair_mask(nb, min_dist):
    n = CHUNK * nb
    r = jnp.arange(n)
    same = (r[:, None] % nb) == (r[None, :] % nb)
    far = (r[None, :] // nb) <= (r[:, None] // nb) - min_dist
    return (same & far).astype(F32)


def _gla_prompt(q, k, g, v, r, nw, nb):
    t = r.shape[0]
    rows = CHUNK * nb
    pad = PAD_TOKENS * nb
    slab = lambda n: pl.BlockSpec((n, rows, LANES), lambda i: (0, i, 0))
    row = lambda c: pl.BlockSpec((rows, c), lambda i: (i, 0))
    far = _pair_mask(nb, GLA_BAND)
    near = _pair_mask(nb, 0)
    return pl.pallas_call(
        functools.partial(_gla_prompt_kernel, nb=nb),
        grid=(t // rows,),
        in_specs=[slab(4), slab(4), slab(4), slab(8), row(1024), _const_spec(far.shape), _const_spec(near.shape),
                  _const_spec(nw.shape)],
        out_specs=(row(1024), _const_spec((nb, GLA_HEADS, GLA_DK, GLA_DV))),
        out_shape=(jax.ShapeDtypeStruct((t, 1024), F32),
                   jax.ShapeDtypeStruct((nb, GLA_HEADS, GLA_DK, GLA_DV), F32)),
        scratch_shapes=[
            pltpu.VMEM((GLA_HEADS, pad + rows, LANES), F32),
            pltpu.VMEM((GLA_HEADS, pad + rows, LANES), F32),
            pltpu.VMEM((2 * GLA_HEADS, pad + rows, LANES), F32),
            pltpu.VMEM((nb, GLA_HEADS, GLA_DV, GLA_DK), F32),
            pltpu.VMEM((2 * GLA_HEADS, rows, LANES), F32),
        ],
        compiler_params=_params(),
        name="gla_prompt",
    )(q, k, g, v, r, far, near, nw)


def _gla_step_kernel(q_ref, k_ref, g_ref, v_ref, r_ref, s_ref, nw_ref, *rest):
    og_ref, sout_ref = rest[-2:]
    nw = nw_ref[...]
    zeros = jnp.zeros((LANES - 8, LANES), F32)
    prow = lax.broadcasted_iota(jnp.int32, (16, 2 * GLA_DV), 0)
    plane = lax.broadcasted_iota(jnp.int32, (16, 2 * GLA_DV), 1)
    pick = [jnp.where(((prow == n) & (plane < GLA_DV)) | ((prow == 8 + n) & (plane >= GLA_DV)), 1.0, 0.0).astype(BF16)
            for n in range(8)]
    for h in range(GLA_HEADS):
        decay = jnp.concatenate([jnp.exp(g_ref[h]), zeros], axis=0).T
        kq = jnp.concatenate([k_ref[h], q_ref[h]], axis=0).astype(BF16)
        cols = [lax.dot_general(kq, pick[n], (((0,), (0,)), ((), ())), preferred_element_type=F32)
                for n in range(8)]
        outs = []
        for n in range(8):
            s_old = s_ref[n, h]
            vrow = jnp.concatenate([v_ref[2 * h, n:n + 1, :], v_ref[2 * h + 1, n:n + 1, :]], axis=1)
            s_new = s_old * decay[:, n:n + 1] + cols[n][:, :GLA_DV] * vrow
            sout_ref[n, h] = s_new
            outs.append(jnp.sum(cols[n][:, GLA_DV:] * s_new, axis=0, keepdims=True))
        o = jnp.concatenate(outs, axis=0)
        ms = jnp.mean(o * o, axis=-1, keepdims=True)
        c0 = h * GLA_DV
        og_ref[:, c0:c0 + GLA_DV] = o * lax.rsqrt(ms + EPS) * nw * _silu(r_ref[:, c0:c0 + GLA_DV])


def _stacked_state_args(states, layer, prev, n_inputs, out_index):
    tail = states.shape[2:]
    spec = pl.BlockSpec((None, 8) + tail, lambda i, _l=layer: (_l, i) + (0,) * len(tail))
    if prev is None:
        prev = jnp.zeros(states.shape, F32)
    return spec, [pl.BlockSpec(memory_space=pl.ANY)], [prev], {n_inputs: out_index}


def _gla_step(q, k, g, v, r, states, layer, prev, nw):
    n = r.shape[0]
    slab = lambda c: pl.BlockSpec((c, 8, LANES), lambda i: (0, i, 0))
    sspec, extra_specs, extra_args, aliases = _stacked_state_args(states, layer, prev, 7, 1)
    return pl.pallas_call(
        _gla_step_kernel,
        grid=(n // 8,),
        in_specs=[slab(4), slab(4), slab(4), slab(8), pl.BlockSpec((8, 1024), lambda i: (i, 0)), sspec,
                  _const_spec(nw.shape)] + extra_specs,
        out_specs=(pl.BlockSpec((8, 1024), lambda i: (i, 0)), sspec),
        out_shape=(jax.ShapeDtypeStruct((n, 1024), F32), jax.ShapeDtypeStruct(states.shape, F32)),
        input_output_aliases=aliases,
        compiler_params=_params(),
        name="gla_step",
    )(q, k, g, v, r, states, nw, *extra_args)


def _s5_prep_kernel(lre_ref, lim_ref, ldt_ref, lre_r_ref, lim_r_ref, ldt_r_ref, bre_ref, bim_ref,
                    ar_ref, ai_ref, bbre_ref, bbim_ref):
    def disc(lre, lim, ldt):
        dt = jnp.exp(ldt)
        mag = jnp.exp(lre * dt)
        ar = mag * jnp.cos(lim * dt)
        ai = mag * jnp.sin(lim * dt)
        return ar, ai

    ar, ai = disc(lre_ref[...], lim_ref[...], ldt_ref[...])
    ar_ref[...] = ar
    ai_ref[...] = ai
    lre, lim = lre_r_ref[...], lim_r_ref[...]
    ar, ai = disc(lre, lim, ldt_r_ref[...])
    den = lre * lre + lim * lim
    wr = ((ar - 1.0) * lre + ai * lim) / den
    wi = (ai * lre - (ar - 1.0) * lim) / den
    bre, bim = bre_ref[...], bim_ref[...]
    bbre_ref[...] = wr * bre - wi * bim
    bbim_ref[...] = wr * bim + wi * bre


def _s5_prep(lam_re, lam_im, log_dt, b_re, b_im):
    n = lam_re.shape[0]
    rows = n * S5_GROUPS
    lre = lam_re.reshape(rows, S5_P)
    lim = lam_im.reshape(rows, S5_P)
    ldt = jnp.broadcast_to(log_dt.reshape(rows, 1), (rows, S5_P))
    rep = lambda a: jnp.repeat(a, S5_GROUP_CH, axis=1)
    args = (lre, lim, ldt, rep(lre), rep(lim), rep(ldt),
            b_re.reshape(rows, S5_P * S5_GROUP_CH), b_im.reshape(rows, S5_P * S5_GROUP_CH))
    wide = jax.ShapeDtypeStruct((rows, S5_P * S5_GROUP_CH), F32)
    narrow = jax.ShapeDtypeStruct((rows, S5_P), F32)
    ar, ai, bbre, bbim = pl.pallas_call(
        _s5_prep_kernel, out_shape=(narrow, narrow, wide, wide), name="s5_prep")(*args)
    shp = (n, S5_GROUPS, S5_P, S5_GROUP_CH)
    return ar.reshape(n, S5_GROUPS, S5_P), ai.reshape(n, S5_GROUPS, S5_P), bbre.reshape(shp), bbim.reshape(shp)


def _s5_kernel(u_ref, sg_ref, h0re_ref, h0im_ref, wbre_ref, wbim_ref, wc_ref, are_ref, aim_ref, d_ref,
               wglu_ref, bglu_ref, y_ref, hre_out, him_out, hre, him, cre, cim, ys, *, nb, tokens):
    step = pl.program_id(0)
    rows = nb * tokens

    @pl.when(step == 0)
    def _():
        cre[...] = h0re_ref[...]
        cim[...] = h0im_ref[...]

    for m in range(4):
        um = u_ref[:, m * LANES:(m + 1) * LANES].astype(BF16)
        hre[:, m * 512:(m + 1) * 512] = jnp.dot(um, wbre_ref[m], preferred_element_type=F32)
        him[:, m * 512:(m + 1) * 512] = jnp.dot(um, wbim_ref[m], preferred_element_type=F32)

    for qd in range(2):
        ql = slice(qd * 1024, (qd + 1) * 1024)
        ar = jnp.broadcast_to(are_ref[:, ql], (nb, 1024))
        ai = jnp.broadcast_to(aim_ref[:, ql], (nb, 1024))

        def tok(t, carry, ql=ql, ar=ar, ai=ai):
            hr, hi = carry
            sel = pl.ds(pl.multiple_of(t * nb, nb), nb)
            nr = ar * hr - ai * hi + hre[sel, ql]
            ni = ar * hi + ai * hr + him[sel, ql]
            hre[sel, ql] = nr
            him[sel, ql] = ni
            return nr, ni

        hr, hi = lax.fori_loop(0, tokens, tok, (cre[:, ql], cim[:, ql]))
        cre[:, ql] = hr
        cim[:, ql] = hi

    for m in range(4):
        hc = jnp.concatenate([hre[:, m * 512:(m + 1) * 512], him[:, m * 512:(m + 1) * 512]], axis=1)
        sl = slice(m * LANES, (m + 1) * LANES)
        y = _mm(hc, wc_ref[m]) + d_ref[:, sl] * u_ref[:, sl]
        ys[:, sl] = jax.nn.gelu(y)
    y = ys[...]
    gate = _sigmoid(_mm(y, wglu_ref[...]) + bglu_ref[...])
    y_ref[...] = y * gate * _silu(sg_ref[...])

    @pl.when(step == pl.num_programs(0) - 1)
    def _():
        hre_out[...] = cre[...]
        him_out[...] = cim[...]


def _s5(u, sg, h0re, h0im, w, nb, tokens):
    t = u.shape[0]
    rows = nb * tokens
    row = lambda c: pl.BlockSpec((rows, c), lambda i: (i, 0))
    weights = (w["wbre"], w["wbim"], w["wc"], w["are"], w["aim"], w["d"], w["wglu"], w["bglu"])
    state = jax.ShapeDtypeStruct((nb, S5_STATE), F32)
    return pl.pallas_call(
        functools.partial(_s5_kernel, nb=nb, tokens=tokens),
        grid=(t // rows,),
        in_specs=[row(S5_WIDTH), row(S5_WIDTH), _const_spec(h0re.shape), _const_spec(h0im.shape)]
        + [_const_spec(a.shape) for a in weights],
        out_specs=(row(S5_WIDTH), _const_spec((nb, S5_STATE)), _const_spec((nb, S5_STATE))),
        out_shape=(jax.ShapeDtypeStruct((t, S5_WIDTH), F32), state, state),
        scratch_shapes=[
            pltpu.VMEM((rows, S5_STATE), F32),
            pltpu.VMEM((rows, S5_STATE), F32),
            pltpu.VMEM((nb, S5_STATE), F32),
            pltpu.VMEM((nb, S5_STATE), F32),
            pltpu.VMEM((rows, S5_WIDTH), F32),
        ],
        compiler_params=_params(),
        name="s5",
    )(u, sg, h0re, h0im, *weights)


def _dn_solve(lms, rhss):
    nblk = CHUNK // SUB
    row = lax.broadcasted_iota(jnp.int32, (SUB, LANES), 0)
    lane = lax.broadcasted_iota(jnp.int32, (SUB, LANES), 1)
    seg = (lane // SUB) * SUB
    in_diag = [(lane >= SUB * blk) & (lane < SUB * (blk + 1)) for blk in range(nblk)]
    eye_pack = jnp.where((lane - seg == row) & (lane < CHUNK), 1.0, 0.0)
    zpad = jnp.zeros((CHUNK, LANES - CHUNK), F32)
    t_invs, lowers = [], []
    for lm in lms:
        wide = jnp.concatenate([lm, zpad], axis=1)
        tiles = [wide[SUB * blk:SUB * (blk + 1)] for blk in range(nblk)]
        dpack = jnp.zeros((SUB, LANES), F32)
        for blk in range(nblk):
            dpack = jnp.where(in_diag[blk], tiles[blk], dpack)
        inv = eye_pack
        for j in range(SUB - 1):
            col_j = jnp.take_along_axis(dpack, seg + j, axis=1)
            inv = inv - col_j * inv[j:j + 1, :]
        t_invs.append(jnp.concatenate(
            [jnp.where(in_diag[blk], inv, 0.0)[:, :CHUNK] for blk in range(nblk)], axis=0).astype(BF16))
        lowers.append(jnp.concatenate(
            [jnp.where(in_diag[blk], 0.0, tiles[blk])[:, :CHUNK] for blk in range(nblk)], axis=0).astype(BF16))
    d = lambda a, b: jnp.dot(a, b, preferred_element_type=F32)
    his = [rhs.astype(BF16) for rhs in rhss]
    los = [(rhs - hi.astype(F32)).astype(BF16) for rhs, hi in zip(rhss, his)]
    trs = [d(t, hi) + d(t, lo_) for t, hi, lo_ in zip(t_invs, his, los)]
    tns = [d(t, lw) for t, lw in zip(t_invs, lowers)]
    xs = [[tr[0:SUB]] for tr in trs]
    for blk in range(1, nblk):
        lo = SUB * blk
        curs = [tr[lo:lo + SUB] - _mm(tn[lo:lo + SUB, 0:lo], jnp.concatenate(x, axis=0))
                for tr, tn, x in zip(trs, tns, xs)]
        for x, cur in zip(xs, curs):
            x.append(cur)
    return [jnp.concatenate(x, axis=0) for x in xs]


def _dn_prompt_kernel(x_ref, z_ref, beta_ref, gd_ref, cw_ref, nw_ref, c0_ref,
                      og_ref, sout_ref, cout_ref,
                      xs, qs, ks, vs, bcol, gcol, gc, gt, oscr, *, nb):
    rows = CHUNK * nb
    hist = (CONV_W - 1) * nb
    pad = PAD_TOKENS * nb
    step = pl.program_id(0)
    st = sout_ref

    @pl.when(step == 0)
    def _():
        st[...] = jnp.zeros_like(st)
        xs[0:hist, :] = c0_ref[...]
        gc[0, 0:pad, :] = jnp.zeros((pad, LANES), F32)

    xs[hist:hist + 64, :] = x_ref[0:64, :]

    def conv_rows(src, base, r0):
        tap = lambda i: base + r0 + i * nb if isinstance(r0, int) else pl.multiple_of(base + r0 + i * nb, 8)
        for c in range(DN_CONV_DIM // LANES):
            cl = slice(c * LANES, (c + 1) * LANES)
            y = src[pl.ds(tap(0), 64), cl] * cw_ref[0:1, cl]
            for i in range(1, CONV_W):
                y = y + src[pl.ds(tap(i), 64), cl] * cw_ref[i:i + 1, cl]
            y = _silu(y)
            if c < 2 * DN_HEADS:
                y = y * lax.rsqrt(jnp.sum(y * y, axis=-1, keepdims=True) + EPS)
            if c < DN_HEADS:
                qs[c, pl.ds(r0, 64), :] = y * (DN_DK ** -0.5)
            elif c < 2 * DN_HEADS:
                ks[c - DN_HEADS, pl.ds(r0, 64), :] = y
            else:
                vs[c - 2 * DN_HEADS, pl.ds(r0, 64), :] = y

    conv_rows(xs, 0, 0)

    def conv_tile(ti, carry):
        conv_rows(x_ref, -hist, pl.multiple_of(ti * 64, 64))
        return carry

    lax.fori_loop(1, rows // 64, conv_tile, 0)

    xs[0:hist, :] = x_ref[rows - hist:rows, :]

    gc[0, pad:pad + rows, :] = gd_ref[...]
    _cumsum_tokens(gc, 0, pad, rows, nb)

    def bc_tile(ti, carry):
        r0 = pl.multiple_of(ti * 64, 64)
        gtile = gc[0, pl.ds(pad + r0, 64), :]
        btile = beta_ref[pl.ds(r0, 64), :]
        for h in range(DN_HEADS):
            gcol[h, pl.ds(r0, 64), :] = jnp.broadcast_to(gtile[:, h:h + 1], (64, LANES))
            bcol[h, pl.ds(r0, 64), :] = jnp.broadcast_to(btile[:, h:h + 1], (64, LANES))
        return carry

    lax.fori_loop(0, rows // 64, bc_tile, 0)

    ii = lax.broadcasted_iota(jnp.int32, (CHUNK, CHUNK), 0)
    jj = lax.broadcasted_iota(jnp.int32, (CHUNK, CHUNK), 1)
    eye = (ii == jj).astype(BF16)

    parts = [_split3(gc[0, pl.ds(pad + b, CHUNK, stride=nb), :]) for b in range(nb)]
    tdot = lambda x: lax.dot_general(x, eye, (((0,), (0,)), ((), ())), preferred_element_type=F32)
    firsts = [tdot(p[0]) for p in parts]
    seconds = [tdot(p[1]) for p in parts]
    thirds = [tdot(p[2]) for p in parts]
    for b in range(nb):
        gt[b] = (firsts[b] + seconds[b]) + thirds[b]

    def per_batch(bi, carry):
        def head_group(gi, carry2):
            probs = [(bi * DN_BATCH + jb, gi * DN_GROUP + jh) for jb in range(DN_BATCH) for jh in range(DN_GROUP)]
            sels = [pl.ds(b, CHUNK, stride=nb) for b, _ in probs]
            loaded = [(qs[h, sel, :], ks[h, sel, :], vs[h, sel, :], bcol[h, sel, :], gcol[h, sel, :],
                       gt[b, pl.ds(h, 1), :], st[b, h]) for (b, h), sel in zip(probs, sels)]
            qbs, kbs, vbs, bcs, gcls, grows, s_olds = zip(*loaded)
            decs = [jnp.where(jj <= ii, jnp.exp(jnp.where(jj <= ii, gcl[:, :CHUNK] - grow, 0.0)), 0.0)
                    for gcl, grow in zip(gcls, grows)]
            kks = [_mm_nt(kb, kb) for kb in kbs]
            qks = [_mm_nt(qb, kb) for qb, kb in zip(qbs, kbs)]
            lms = [jnp.where(jj < ii, bc[:, :CHUNK] * kk * dec, 0.0) for bc, kk, dec in zip(bcs, kks, decs)]
            egs = [jnp.exp(gcl) for gcl in gcls]
            rhss = [jnp.concatenate([vb * bc, kb * (bc * eg)], axis=1)
                    for vb, kb, bc, eg in zip(vbs, kbs, bcs, egs)]
            sols = _dn_solve(lms, rhss)
            v_news = [sol[:, :DN_DV] - _mm(sol[:, DN_DV:], s_old) for sol, s_old in zip(sols, s_olds)]
            os_ = [_mm(qb * eg, s_old) + _mm(qk * dec, v_new)
                   for qb, eg, s_old, qk, dec, v_new in zip(qbs, egs, s_olds, qks, decs, v_news)]
            glasts = [gcl[CHUNK - 1:CHUNK, :] for gcl in gcls]
            s_news = [s_old * jnp.exp(glast) + _mm_tn(kb * jnp.exp(glast - gcl), v_new)
                      for s_old, glast, kb, gcl, v_new in zip(s_olds, glasts, kbs, gcls, v_news)]
            for (b, h), sel, o, s_new in zip(probs, sels, os_, s_news):
                st[b, h] = s_new
                oscr[h, sel, :] = o
            return carry2

        lax.fori_loop(0, DN_HEADS // DN_GROUP, head_group, 0)
        return carry

    lax.fori_loop(0, nb // DN_BATCH, per_batch, 0)

    nw = nw_ref[...]

    def epi_tile(ti, carry):
        r0 = pl.multiple_of(ti * 64, 64)
        for h in range(DN_HEADS):
            o = oscr[h, pl.ds(r0, 64), :]
            inv = lax.rsqrt(jnp.mean(o * o, axis=-1, keepdims=True) + EPS)
            cl = slice(h * LANES, (h + 1) * LANES)
            og_ref[pl.ds(r0, 64), cl] = o * inv * nw * _silu(z_ref[pl.ds(r0, 64), cl])
        return carry

    lax.fori_loop(0, rows // 64, epi_tile, 0)

    @pl.when(step == pl.num_programs(0) - 1)
    def _():
        cout_ref[...] = xs[0:hist, :]


def _dn_prompt(xqkv, z, beta, gd, cw, nw, c0, nb):
    t = z.shape[0]
    rows = CHUNK * nb
    hist = (CONV_W - 1) * nb
    row = lambda c: pl.BlockSpec((rows, c), lambda i: (i, 0))
    slab = pltpu.VMEM((DN_HEADS, rows, LANES), F32)
    return pl.pallas_call(
        functools.partial(_dn_prompt_kernel, nb=nb),
        grid=(t // rows,),
        in_specs=[row(DN_CONV_DIM), row(1024), row(LANES), row(LANES), _const_spec(cw.shape),
                  _const_spec(nw.shape), _const_spec(c0.shape)],
        out_specs=(row(1024), _const_spec((nb, DN_HEADS, DN_DK, DN_DV)), _const_spec((hist, DN_CONV_DIM))),
        out_shape=(jax.ShapeDtypeStruct((t, 1024), F32),
                   jax.ShapeDtypeStruct((nb, DN_HEADS, DN_DK, DN_DV), F32),
                   jax.ShapeDtypeStruct((hist, DN_CONV_DIM), F32)),
        scratch_shapes=[
            pltpu.VMEM((hist + 64, DN_CONV_DIM), F32),
            slab, slab, slab, slab, slab,
            pltpu.VMEM((1, PAD_TOKENS * nb + rows, LANES), F32),
            pltpu.VMEM((nb, LANES, CHUNK), F32),
            slab,
        ],
        compiler_params=_params(),
        name="dn_prompt",
    )(xqkv, z, beta, gd, cw, nw, c0)


def _dn_step_kernel(x_ref, cb_ref, z_ref, beta_ref, gd_ref, s_ref, cw_ref, nw_ref, *rest):
    og_ref, sout_ref, cn_ref = rest[-3:]
    nw = nw_ref[...]
    cn_ref[0] = cb_ref[1]
    cn_ref[1] = cb_ref[2]
    cn_ref[2] = x_ref[...]
    beta = beta_ref[...]
    eg_all = jnp.exp(gd_ref[...])

    def conv(c):
        cl = slice(c * LANES, (c + 1) * LANES)
        y = cb_ref[0, :, cl] * cw_ref[0:1, cl]
        y = y + cb_ref[1, :, cl] * cw_ref[1:2, cl]
        y = y + cb_ref[2, :, cl] * cw_ref[2:3, cl]
        y = y + x_ref[:, cl] * cw_ref[3:4, cl]
        return _silu(y)

    def l2(y):
        return y * lax.rsqrt(jnp.sum(y * y, axis=-1, keepdims=True) + EPS)

    prow = lax.broadcasted_iota(jnp.int32, (16, 2 * LANES), 0)
    plane = lax.broadcasted_iota(jnp.int32, (16, 2 * LANES), 1)
    pick = [jnp.where(((prow == n) & (plane < LANES)) | ((prow == 8 + n) & (plane >= LANES)), 1.0, 0.0).astype(BF16)
            for n in range(8)]

    for h in range(DN_HEADS):
        qh = l2(conv(h)) * (DN_DK ** -0.5)
        kh = l2(conv(DN_HEADS + h))
        vh = conv(2 * DN_HEADS + h)
        kq = jnp.concatenate([kh, qh], axis=0).astype(BF16)
        cols = [lax.dot_general(kq, pick[n], (((0,), (0,)), ((), ())), preferred_element_type=F32)
                for n in range(8)]
        outs = []
        for n in range(8):
            s_old = s_ref[n, h]
            kcol = cols[n][:, :LANES]
            eg = eg_all[n:n + 1, h:h + 1]
            ks_row = jnp.sum(kcol * s_old, axis=0, keepdims=True)
            v_new = beta[n:n + 1, h:h + 1] * (vh[n:n + 1, :] - eg * ks_row)
            s_new = s_old * eg + kcol * v_new
            sout_ref[n, h] = s_new
            outs.append(jnp.sum(cols[n][:, LANES:] * s_new, axis=0, keepdims=True))
        o = jnp.concatenate(outs, axis=0)
        cl = slice(h * LANES, (h + 1) * LANES)
        og_ref[:, cl] = o * lax.rsqrt(jnp.mean(o * o, axis=-1, keepdims=True) + EPS) * nw * _silu(z_ref[:, cl])


def _dn_step(xqkv, cbuf, z, beta, gd, states, layer, prev, cw, nw):
    n = z.shape[0]
    row = lambda c: pl.BlockSpec((8, c), lambda i: (i, 0))
    cspec = pl.BlockSpec((CONV_W - 1, 8, DN_CONV_DIM), lambda i: (0, i, 0))
    sspec, extra_specs, extra_args, aliases = _stacked_state_args(states, layer, prev, 8, 1)
    return pl.pallas_call(
        _dn_step_kernel,
        grid=(n // 8,),
        in_specs=[row(DN_CONV_DIM), cspec, row(1024), row(LANES), row(LANES), sspec,
                  _const_spec(cw.shape), _const_spec(nw.shape)] + extra_specs,
        out_specs=(row(1024), sspec, cspec),
        out_shape=(jax.ShapeDtypeStruct((n, 1024), F32), jax.ShapeDtypeStruct(states.shape, F32),
                   jax.ShapeDtypeStruct(cbuf.shape, F32)),
        input_output_aliases=aliases,
        compiler_params=_params(),
        name="dn_step",
    )(xqkv, cbuf, z, beta, gd, states, cw, nw, *extra_args)


def _pad_cols(a, n):
    return jnp.pad(a, ((0, 0), (0, n - a.shape[1])))


def _even_weights(w_in, w_gate_up, b_gate, w_out):
    return {
        "w": _pad_cols(w_in.astype(BF16), W_IN_PAD),
        "wg": jnp.pad(w_gate_up, ((0, LANES - GLA_RANK), (0, 0))).astype(BF16),
        "bg": b_gate.reshape(1, -1),
        "wo": w_out.astype(BF16),
    }


def _s5_weights(ar, ai, bbre, bbim, c_re, c_im, d, w_glu, b_glu):
    eye = jnp.eye(8, dtype=F32)

    def pack_b(bb):
        t = bb.reshape(4, 8, S5_P, S5_GROUP_CH).transpose(0, 1, 3, 2)
        return (t[:, :, :, None, :] * eye[None, :, None, :, None]).reshape(4, 128, 512).astype(BF16)

    def pack_c(c):
        t = c.reshape(4, 8, S5_GROUP_CH, S5_P).transpose(0, 1, 3, 2)
        return (t[:, :, :, None, :] * eye[None, :, None, :, None]).reshape(4, 512, 128)

    return {
        "wbre": pack_b(bbre), "wbim": pack_b(bbim),
        "wc": jnp.concatenate([pack_c(c_re), -pack_c(c_im)], axis=1).astype(BF16),
        "are": ar.reshape(1, S5_STATE), "aim": ai.reshape(1, S5_STATE),
        "d": d.reshape(1, -1), "wglu": w_glu.astype(BF16), "bglu": b_glu.reshape(1, -1),
    }


def _odd_weights(w_in, conv_w, a_log, dt_bias, w_out):
    return {
        "w": _pad_cols(w_in.astype(BF16), W_IN_PAD),
        "alog": _pad_cols(a_log.reshape(1, -1), LANES), "dtb": _pad_cols(dt_bias.reshape(1, -1), LANES),
        "cw": jnp.pad(conv_w, ((0, 8 - CONV_W), (0, 0))),
        "wo": w_out.astype(BF16),
    }


def kernel(x_prompt, x_sample, state_gla, state_s5_re, state_s5_im, state_delta, state_conv, norm_w, final_norm_w, w_in_even, gla_w_gate_up, gla_b_gate, gla_norm_w, s5_lambda_re, s5_lambda_im, s5_log_dt, s5_b_re, s5_b_im, s5_c_re, s5_c_im, s5_d, s5_w_glu, s5_b_glu, w_out_even, w_in_odd, dn_conv_w, dn_a_log, dn_dt_bias, dn_norm_w, w_out_odd):
    nb, seq, _ = x_prompt.shape
    ns = x_sample.shape[0]
    assert seq % CHUNK == 0 and nb % 8 == 0 and ns % 8 == 0 and x_sample.shape[1] == 1
    depth = norm_w.shape[0]

    assert depth % 2 == 0 and state_gla.shape[0] == depth // 2 and state_delta.shape[0] == depth // 2
    xp = None
    xs = x_sample.reshape(ns, D_MODEL)
    fw = final_norm_w.reshape(1, -1)

    ar, ai, bbre, bbim = _s5_prep(s5_lambda_re, s5_lambda_im, s5_log_dt, s5_b_re, s5_b_im)

    gla_p, s5r_p, s5i_p, s5r_s, s5i_s, dn_p, cv_p, cv_s = ([] for _ in range(8))
    gla_s = dn_s = None
    for layer in range(depth):
        i = layer // 2
        nw = norm_w[layer].reshape(1, -1)
        if layer % 2 == 0:
            w = _even_weights(w_in_even[i], gla_w_gate_up[i], gla_b_gate[i], w_out_even[i])
            w5 = _s5_weights(ar[i], ai[i], bbre[i], bbim[i], s5_c_re[i], s5_c_im[i], s5_d[i],
                             s5_w_glu[i], s5_b_glu[i])
            gnw = gla_norm_w[i].reshape(1, -1)
            if layer == 0:
                q, k, g, v, r, u, sg, xp = _proj_even(x_prompt, nw, w, batch_major=True)
            else:
                q, k, g, v, r, u, sg = _proj_even(xp, nw, w)
            og, s_new = _gla_prompt(q, k, g, v, r, gnw, nb)
            zero = jnp.zeros((nb, S5_STATE), F32)
            y5, hre, him = _s5(u, sg, zero, zero, w5, nb, CHUNK)
            xp = _out_even(xp, og, y5, w["wo"])
            gla_p.append(s_new)
            s5r_p.append(hre.reshape(nb, S5_GROUPS, S5_P))
            s5i_p.append(him.reshape(nb, S5_GROUPS, S5_P))
            q, k, g, v, r, u, sg = _proj_even(xs, nw, w)
            og, gla_s = _gla_step(q, k, g, v, r, state_gla, i, gla_s, gnw)
            y5, hre, him = _s5(u, sg, state_s5_re[i].reshape(ns, S5_STATE), state_s5_im[i].reshape(ns, S5_STATE),
                               w5, ns, 1)
            xs = _out_even(xs, og, y5, w["wo"])
            s5r_s.append(hre.reshape(ns, S5_GROUPS, S5_P))
            s5i_s.append(him.reshape(ns, S5_GROUPS, S5_P))
        else:
            w = _odd_weights(w_in_odd[i], dn_conv_w[i], dn_a_log[i], dn_dt_bias[i], w_out_odd[i])
            dnw = dn_norm_w[i].reshape(1, -1)
            final = layer == depth - 1
            xqkv, z, beta, gd = _proj_odd(xp, nw, w)
            c0 = jnp.zeros(((CONV_W - 1) * nb, DN_CONV_DIM), F32)
            og, s_new, c_new = _dn_prompt(xqkv, z, beta, gd, w["cw"], dnw, c0, nb)
            xp = _out_odd(xp, og, w["wo"], fw, final, nb_out=nb if final else 0)
            dn_p.append(s_new)
            cv_p.append(c_new.reshape(CONV_W - 1, nb, DN_CONV_DIM).transpose(1, 0, 2))
            xqkv, z, beta, gd = _proj_odd(xs, nw, w)
            og, dn_s, c_new = _dn_step(xqkv, state_conv[i].transpose(1, 0, 2), z, beta, gd, state_delta, i, dn_s,
                                       w["cw"], dnw)
            xs = _out_odd(xs, og, w["wo"], fw, final)
            cv_s.append(c_new.transpose(1, 0, 2))

    y_prompt = xp
    y_sample = xs.reshape(ns, 1, D_MODEL)
    st = jnp.stack
    return (y_prompt, y_sample, st(gla_p), gla_s, st(s5r_p), st(s5i_p), st(s5r_s), st(s5i_s),
            st(dn_p), dn_s, st(cv_p), st(cv_s))
```

```python
import functools
import math

import jax
import jax.numpy as jnp
from jax import lax
from jax.experimental import pallas as pl
from jax.experimental.pallas import tpu as pltpu

F32 = jnp.float32
BF16 = jnp.bfloat16
EPS = 1e-6

D_MODEL = 1024
GLA_HEADS, GLA_DK, GLA_DV, GLA_RANK = 4, 128, 256, 16
GLA_GATE_NORM = 16.0
S5_GROUPS, S5_GROUP_CH, S5_P, S5_WIDTH = 32, 16, 64, 512
S5_STATE = S5_GROUPS * S5_P
DN_HEADS, DN_DK, DN_DV = 8, 128, 128
DN_CONV_DIM, CONV_W = 3072, 4
CHUNK = 64
SUB = 8
GLA_BAND = 8
GLA_BATCH = 4
GLA_SAFE_SPAN = 60.0
PAD_TOKENS = 16
DN_BATCH = 4
DN_GROUP = 8
LANES = 128
W_IN_PAD = 4224
VMEM_LIMIT = 56 * 1024 * 1024


def _mm(a, b):
    return jnp.dot(a.astype(BF16), b.astype(BF16), preferred_element_type=F32)


def _mm_nt(a, b):
    return lax.dot_general(a.astype(BF16), b.astype(BF16), (((1,), (1,)), ((), ())),
                           preferred_element_type=F32)


def _mm_tn(a, b):
    return lax.dot_general(a.astype(BF16), b.astype(BF16), (((0,), (0,)), ((), ())),
                           preferred_element_type=F32)


def _split3(a):
    a1 = a.astype(BF16)
    r1 = a - a1.astype(F32)
    a2 = r1.astype(BF16)
    a3 = (r1 - a2.astype(F32)).astype(BF16)
    return a1, a2, a3


def _mm_tn_exact(a, m_bf16):
    a1, a2, a3 = _split3(a)
    d = lambda x: lax.dot_general(x, m_bf16, (((0,), (0,)), ((), ())), preferred_element_type=F32)
    return (d(a1) + d(a2)) + d(a3)


def _sigmoid(x):
    return 1.0 / (1.0 + jnp.exp(-x))


def _silu(x):
    return x * _sigmoid(x)


def _softplus(x):
    return jnp.maximum(x, 0.0) + jnp.log1p(jnp.exp(-jnp.abs(x)))


def _rms_rows(x, w):
    ms = jnp.mean(x * x, axis=-1, keepdims=True)
    return x * lax.rsqrt(ms + EPS) * w


def _const_spec(shape):
    nd = len(shape)
    return pl.BlockSpec(shape, lambda i, _nd=nd: (0,) * _nd)


def _params(**flags):
    return pltpu.CompilerParams(dimension_semantics=("arbitrary",), vmem_limit_bytes=VMEM_LIMIT,
                                flags=flags or None)


def _row_tile(t):
    return min(512, t)


def _proj_even_kernel(x_ref, nw_ref, w_ref, wg_ref, bg_ref, q_ref, k_ref, g_ref, v_ref, r_ref, u_ref, sg_ref,
                      *relayout, nb_in):
    if nb_in:
        xo_ref, xt = relayout
        for b in range(nb_in):
            for c in range(D_MODEL // LANES):
                xt[c, pl.ds(b, x_ref.shape[1], stride=nb_in), :] = x_ref[b, :, c * LANES:(c + 1) * LANES]
        x = jnp.concatenate([xt[c] for c in range(D_MODEL // LANES)], axis=1)
        xo_ref[...] = x
    else:
        x = x_ref[...]
    hb = _rms_rows(x, nw_ref[...]).astype(BF16)
    d = lambda lo, hi: jnp.dot(hb, w_ref[:, lo:hi], preferred_element_type=F32)
    q = d(0, 512) * (GLA_DK ** -0.5)
    k = d(512, 1024)
    lr_u = d(3072, 3712)
    u_sg = d(3584, 4224)
    logit = jnp.dot(lr_u[:, :LANES].astype(BF16), wg_ref[...], preferred_element_type=F32) + bg_ref[...]
    g = -_softplus(-logit) / GLA_GATE_NORM
    for h in range(GLA_HEADS):
        sl = slice(h * LANES, (h + 1) * LANES)
        q_ref[h] = q[:, sl]
        k_ref[h] = k[:, sl]
        g_ref[h] = g[:, sl]
    v = d(1024, 2048)
    for s in range(2 * GLA_HEADS):
        v_ref[s] = v[:, s * LANES:(s + 1) * LANES]
    r_ref[...] = d(2048, 3072)
    u_ref[...] = lr_u[:, GLA_RANK:GLA_RANK + S5_WIDTH]
    sg_ref[...] = u_sg[:, GLA_RANK:GLA_RANK + S5_WIDTH]


def _proj_even(x, nw, w, batch_major=False):
    nb_in = x.shape[0] if batch_major else 0
    t = x.shape[0] * x.shape[1] if batch_major else x.shape[0]
    tm = _row_tile(t)
    row = lambda c: pl.BlockSpec((tm, c), lambda i: (i, 0))
    slab = lambda n: pl.BlockSpec((n, tm, LANES), lambda i: (0, i, 0))
    x_spec = pl.BlockSpec((nb_in, tm // nb_in, D_MODEL), lambda i: (0, i, 0)) if batch_major else row(D_MODEL)
    extra_out_shape = (jax.ShapeDtypeStruct((t, D_MODEL), F32),) if batch_major else ()
    extra_out_specs = (row(D_MODEL),) if batch_major else ()
    scratch = [pltpu.VMEM((D_MODEL // LANES, tm, LANES), F32)] if batch_major else []
    out_shape = (
        jax.ShapeDtypeStruct((GLA_HEADS, t, LANES), F32),
        jax.ShapeDtypeStruct((GLA_HEADS, t, LANES), F32),
        jax.ShapeDtypeStruct((GLA_HEADS, t, LANES), F32),
        jax.ShapeDtypeStruct((2 * GLA_HEADS, t, LANES), F32),
        jax.ShapeDtypeStruct((t, 1024), F32),
        jax.ShapeDtypeStruct((t, S5_WIDTH), F32),
        jax.ShapeDtypeStruct((t, S5_WIDTH), F32),
    )
    weights = (w["w"], w["wg"], w["bg"])
    return pl.pallas_call(
        functools.partial(_proj_even_kernel, nb_in=nb_in),
        grid=(t // tm,),
        in_specs=[x_spec, _const_spec(nw.shape)] + [_const_spec(a.shape) for a in weights],
        out_specs=(slab(4), slab(4), slab(4), slab(8), row(1024), row(S5_WIDTH), row(S5_WIDTH)) + extra_out_specs,
        out_shape=out_shape + extra_out_shape,
        scratch_shapes=scratch,
        compiler_params=_params(),
        name="proj_even",
    )(x, nw, *weights)


def _proj_odd_kernel(x_ref, nw_ref, w_ref, alog_ref, dtb_ref, xqkv_ref, z_ref, beta_ref, gd_ref):
    hb = _rms_rows(x_ref[...], nw_ref[...]).astype(BF16)
    d = lambda lo, hi: jnp.dot(hb, w_ref[:, lo:hi], preferred_element_type=F32)
    xqkv_ref[...] = d(0, DN_CONV_DIM)
    z_ref[...] = d(DN_CONV_DIM, DN_CONV_DIM + 1024)
    tail = d(DN_CONV_DIM + 1024, DN_CONV_DIM + 1024 + LANES)
    beta_ref[...] = _sigmoid(tail)
    a = pltpu.roll(tail, LANES - DN_HEADS, 1)
    gd_ref[...] = -jnp.exp(alog_ref[...]) * _softplus(a + dtb_ref[...])


def _proj_odd(x, nw, w):
    t = x.shape[0]
    tm = _row_tile(t)
    row = lambda c: pl.BlockSpec((tm, c), lambda i: (i, 0))
    weights = (w["w"], w["alog"], w["dtb"])
    return pl.pallas_call(
        _proj_odd_kernel,
        grid=(t // tm,),
        in_specs=[row(D_MODEL), _const_spec(nw.shape)] + [_const_spec(a.shape) for a in weights],
        out_specs=(row(DN_CONV_DIM), row(1024), row(LANES), row(LANES)),
        out_shape=(
            jax.ShapeDtypeStruct((t, DN_CONV_DIM), F32),
            jax.ShapeDtypeStruct((t, 1024), F32),
            jax.ShapeDtypeStruct((t, LANES), F32),
            jax.ShapeDtypeStruct((t, LANES), F32),
        ),
        compiler_params=_params(),
        name="proj_odd",
    )(x, nw, *weights)


def _out_even_kernel(x_ref, og_ref, y5_ref, w_ref, o_ref):
    o_ref[...] = x_ref[...] + (_mm(og_ref[...], w_ref[0:1024, :]) + _mm(y5_ref[...], w_ref[1024:, :]))


def _out_even(x, og, y5, w):
    t = x.shape[0]
    tm = _row_tile(t)
    row = lambda c: pl.BlockSpec((tm, c), lambda i: (i, 0))
    return pl.pallas_call(
        _out_even_kernel,
        grid=(t // tm,),
        in_specs=[row(D_MODEL), row(1024), row(S5_WIDTH), _const_spec(w.shape)],
        out_specs=row(D_MODEL),
        out_shape=jax.ShapeDtypeStruct((t, D_MODEL), F32),
        compiler_params=_params(),
        name="out_even",
    )(x, og, y5, w)


def _out_odd_kernel(x_ref, og_ref, w_ref, fw_ref, o_ref, *slabs, final, nb_out):
    y = x_ref[...] + _mm(og_ref[...], w_ref[...])
    if final:
        y = _rms_rows(y, fw_ref[...])
    if nb_out:
        ys, = slabs
        for c in range(D_MODEL // LANES):
            ys[c] = y[:, c * LANES:(c + 1) * LANES]
        for b in range(nb_out):
            for c in range(D_MODEL // LANES):
                o_ref[b, :, c * LANES:(c + 1) * LANES] = ys[c, pl.ds(b, o_ref.shape[1], stride=nb_out), :]
    else:
        o_ref[...] = y


def _out_odd(x, og, w, fw, final, nb_out=0):
    t = x.shape[0]
    tm = _row_tile(t)
    row = lambda c: pl.BlockSpec((tm, c), lambda i: (i, 0))
    if nb_out:
        out_spec = pl.BlockSpec((nb_out, tm // nb_out, D_MODEL), lambda i: (0, i, 0))
        out_shape = jax.ShapeDtypeStruct((nb_out, t // nb_out, D_MODEL), F32)
        scratch = [pltpu.VMEM((D_MODEL // LANES, tm, LANES), F32)]
    else:
        out_spec, out_shape, scratch = row(D_MODEL), jax.ShapeDtypeStruct((t, D_MODEL), F32), []
    return pl.pallas_call(
        functools.partial(_out_odd_kernel, final=final, nb_out=nb_out),
        grid=(t // tm,),
        in_specs=[row(D_MODEL), row(1024), _const_spec(w.shape), _const_spec(fw.shape)],
        out_specs=out_spec,
        out_shape=out_shape,
        scratch_shapes=scratch,
        compiler_params=_params(),
        name="out_odd",
    )(x, og, w, fw)


def _cumsum_tokens(ref, lead, pad, rows, nb):
    shift = nb
    while shift <= pad:
        ref[lead, pad:pad + rows, :] = ref[lead, pad:pad + rows, :] + ref[lead, pad - shift:pad + rows - shift, :]
        shift *= 2
    while shift < rows:
        ref[lead, pad + shift:pad + rows, :] = (ref[lead, pad + shift:pad + rows, :]
                                                + ref[lead, pad:pad + rows - shift, :])
        shift *= 2


def _gla_prompt_kernel(q_ref, k_ref, g_ref, v_ref, r_ref, far_ref, near_ref, nw_ref, og_ref, sout_ref,
                       kp, bp, vp, st, oscr, ocar, *, nb):
    rows = CHUNK * nb
    pad = PAD_TOKENS * nb
    blk_rows = GLA_BAND * nb
    step = pl.program_id(0)

    @pl.when(step == 0)
    def _():
        st[...] = jnp.zeros_like(st)
        kp[:, 0:pad, :] = jnp.zeros((GLA_HEADS, pad, LANES), F32)
        bp[:, 0:pad, :] = jnp.zeros((GLA_HEADS, pad, LANES), F32)
        vp[:, 0:pad, :] = jnp.zeros((2 * GLA_HEADS, pad, LANES), F32)

    for h in range(GLA_HEADS):
        bp[h, pad:pad + rows, :] = g_ref[h]
        kp[h, pad:pad + rows, :] = k_ref[h]
        _cumsum_tokens(bp, h, pad, rows, nb)
    for s in range(2 * GLA_HEADS):
        vp[s, pad:pad + rows, :] = v_ref[s]

    nblk = CHUNK // GLA_BAND

    span = jnp.zeros((nb, LANES), F32)
    for h in range(GLA_HEADS):
        for blk in range(nblk):
            lo = pad + blk * blk_rows
            span = jnp.maximum(span, bp[h, lo - nb:lo, :] - bp[h, lo + blk_rows - nb:lo + blk_rows, :])
    safe = jnp.max(span) <= GLA_SAFE_SPAN

    def pairs_on_mxu(first_blk, mask_ref, accumulate):
        for h in range(GLA_HEADS):
            probs = []
            for blk in range(first_blk, nblk):
                lo = blk * blk_rows
                hi = lo + (1 - first_blk) * blk_rows
                ref_b = bp[h, pad + lo - nb:pad + lo, :]
                qs = q_ref[h, lo:lo + blk_rows, :] * jnp.exp(
                    bp[h, pad + lo:pad + lo + blk_rows, :] - jnp.concatenate([ref_b] * GLA_BAND, axis=0))
                ks = kp[h, pad:pad + hi, :] * jnp.exp(
                    jnp.concatenate([ref_b] * (hi // nb), axis=0) - bp[h, pad:pad + hi, :])
                probs.append((hi, _mm_nt(qs, ks) * mask_ref[lo:lo + blk_rows, 0:hi]))
            for blk in range(first_blk, nblk):
                lo = blk * blk_rows
                hi, p = probs[blk - first_blk]
                p = p.astype(BF16)
                for half in range(2):
                    sl = 2 * h + half
                    pv = jnp.dot(p, vp[sl, pad:pad + hi, :].astype(BF16), preferred_element_type=F32)
                    oscr[sl, lo:lo + blk_rows, :] = oscr[sl, lo:lo + blk_rows, :] + pv if accumulate else pv

    @pl.when(safe)
    def _():
        pairs_on_mxu(0, near_ref, False)

    def band_tile(ti, carry):
        r0 = pl.multiple_of(ti * 64, 64)
        for h in range(GLA_HEADS):
            qt = q_ref[h, pl.ds(r0, 64), :]
            bt = bp[h, pl.ds(pad + r0, 64), :]
            acc0 = jnp.zeros((64, LANES), F32)
            acc1 = jnp.zeros((64, LANES), F32)
            for d in range(GLA_BAND):
                off = pl.multiple_of(pad + r0 - d * nb, 8)
                ks = kp[h, pl.ds(off, 64), :]
                bs = bp[h, pl.ds(off, 64), :]
                w = jnp.sum(qt * ks * jnp.exp(bt - bs), axis=-1, keepdims=True)
                acc0 = acc0 + w * vp[2 * h, pl.ds(off, 64), :]
                acc1 = acc1 + w * vp[2 * h + 1, pl.ds(off, 64), :]
            oscr[2 * h, pl.ds(r0, 64), :] = acc0
            oscr[2 * h + 1, pl.ds(r0, 64), :] = acc1
        return carry

    @pl.when(jnp.logical_not(safe))
    def _():
        lax.fori_loop(0, rows // 64, band_tile, 0)
        pairs_on_mxu(1, far_ref, True)

    def per_batch(bi, carry):
        probs = [(bi * GLA_BATCH + jb, h) for jb in range(GLA_BATCH) for h in range(GLA_HEADS)]
        sels = [pl.ds(b, CHUNK, stride=nb) for b, _ in probs]
        loaded = [(q_ref[h, sel, :], kp[h, pl.ds(pad + b, CHUNK, stride=nb), :],
                   bp[h, pl.ds(pad + b, CHUNK, stride=nb), :], vp[2 * h, pl.ds(pad + b, CHUNK, stride=nb), :],
                   vp[2 * h + 1, pl.ds(pad + b, CHUNK, stride=nb), :], st[b, h])
                  for (b, h), sel in zip(probs, sels)]
        outs = [_mm_nt(qb * jnp.exp(bb), stt) for qb, _, bb, _, _, stt in loaded]
        news = []
        for _, kb, bb, v0, v1, stt in loaded:
            blast = bb[CHUNK - 1:CHUNK, :]
            kd = kb * jnp.exp(blast - bb)
            news.append(stt * jnp.exp(blast) + _mm_tn(jnp.concatenate([v0, v1], axis=1), kd))
        for (b, h), sel, out, new in zip(probs, sels, outs, news):
            ocar[2 * h, sel, :] = out[:, :LANES]
            ocar[2 * h + 1, sel, :] = out[:, LANES:]
            st[b, h] = new
        return carry

    lax.fori_loop(0, nb // GLA_BATCH, per_batch, 0)

    nw = nw_ref[...]

    def epi_tile(ti, carry):
        r0 = pl.multiple_of(ti * 64, 64)
        for h in range(GLA_HEADS):
            o0 = oscr[2 * h, pl.ds(r0, 64), :] + ocar[2 * h, pl.ds(r0, 64), :]
            o1 = oscr[2 * h + 1, pl.ds(r0, 64), :] + ocar[2 * h + 1, pl.ds(r0, 64), :]
            ms = (jnp.sum(o0 * o0, axis=-1, keepdims=True) + jnp.sum(o1 * o1, axis=-1, keepdims=True)) / GLA_DV
            inv = lax.rsqrt(ms + EPS)
            c0 = h * GLA_DV
            og_ref[pl.ds(r0, 64), c0:c0 + LANES] = (
                o0 * inv * nw[:, :LANES] * _silu(r_ref[pl.ds(r0, 64), c0:c0 + LANES])).astype(BF16)
            og_ref[pl.ds(r0, 64), c0 + LANES:c0 + 2 * LANES] = (
                o1 * inv * nw[:, LANES:] * _silu(r_ref[pl.ds(r0, 64), c0 + LANES:c0 + 2 * LANES])).astype(BF16)
        return carry

    lax.fori_loop(0, rows // 64, epi_tile, 0)

    @pl.when(step == pl.num_programs(0) - 1)
    def _():
        def wr(b, carry):
            for h in range(GLA_HEADS):
                sout_ref[b, h] = st[b, h].T
            return carry
        lax.fori_loop(0, nb, wr, 0)


def _pair_mask(nb, min_dist):
    n = CHUNK * nb
    r = jnp.arange(n)
    same = (r[:, None] % nb) == (r[None, :] % nb)
    far = (r[None, :] // nb) <= (r[:, None] // nb) - min_dist
    return (same & far).astype(F32)


def _gla_prompt(q, k, g, v, r, nw, nb):
    t = r.shape[0]
    rows = CHUNK * nb
    pad = PAD_TOKENS * nb
    slab = lambda n: pl.BlockSpec((n, rows, LANES), lambda i: (0, i, 0))
    row = lambda c: pl.BlockSpec((rows, c), lambda i: (i, 0))
    far = _pair_mask(nb, GLA_BAND)
    near = _pair_mask(nb, 0)
    return pl.pallas_call(
        functools.partial(_gla_prompt_kernel, nb=nb),
        grid=(t // rows,),
        in_specs=[slab(4), slab(4), slab(4), slab(8), row(1024), _const_spec(far.shape), _const_spec(near.shape),
                  _const_spec(nw.shape)],
        out_specs=(row(1024), _const_spec((nb, GLA_HEADS, GLA_DK, GLA_DV))),
        out_shape=(jax.ShapeDtypeStruct((t, 1024), BF16),
                   jax.ShapeDtypeStruct((nb, GLA_HEADS, GLA_DK, GLA_DV), F32)),
        scratch_shapes=[
            pltpu.VMEM((GLA_HEADS, pad + rows, LANES), F32),
            pltpu.VMEM((GLA_HEADS, pad + rows, LANES), F32),
            pltpu.VMEM((2 * GLA_HEADS, pad + rows, LANES), F32),
            pltpu.VMEM((nb, GLA_HEADS, GLA_DV, GLA_DK), F32),
            pltpu.VMEM((2 * GLA_HEADS, rows, LANES), F32),
            pltpu.VMEM((2 * GLA_HEADS, rows, LANES), F32),
        ],
        compiler_params=_params(),
        name="gla_prompt",
    )(q, k, g, v, r, far, near, nw)


def _gla_step_kernel(q_ref, k_ref, g_ref, v_ref, r_ref, s_ref, nw_ref, *rest):
    og_ref, sout_ref = rest[-2:]
    nw = nw_ref[...]
    zeros = jnp.zeros((LANES - 8, LANES), F32)
    prow = lax.broadcasted_iota(jnp.int32, (16, 2 * GLA_DV), 0)
    plane = lax.broadcasted_iota(jnp.int32, (16, 2 * GLA_DV), 1)
    pick = [jnp.where(((prow == n) & (plane < GLA_DV)) | ((prow == 8 + n) & (plane >= GLA_DV)), 1.0, 0.0).astype(BF16)
            for n in range(8)]
    for h in range(GLA_HEADS):
        decay = jnp.concatenate([jnp.exp(g_ref[h]), zeros], axis=0).T
        kq = jnp.concatenate([k_ref[h], q_ref[h]], axis=0).astype(BF16)
        cols = [lax.dot_general(kq, pick[n], (((0,), (0,)), ((), ())), preferred_element_type=F32)
                for n in range(8)]
        outs = []
        for n in range(8):
            s_old = s_ref[n, h]
            vrow = jnp.concatenate([v_ref[2 * h, n:n + 1, :], v_ref[2 * h + 1, n:n + 1, :]], axis=1)
            s_new = s_old * decay[:, n:n + 1] + cols[n][:, :GLA_DV] * vrow
            sout_ref[n, h] = s_new
            outs.append(jnp.sum(cols[n][:, GLA_DV:] * s_new, axis=0, keepdims=True))
        o = jnp.concatenate(outs, axis=0)
        ms = jnp.mean(o * o, axis=-1, keepdims=True)
        c0 = h * GLA_DV
        og_ref[:, c0:c0 + GLA_DV] = o * lax.rsqrt(ms + EPS) * nw * _silu(r_ref[:, c0:c0 + GLA_DV])


def _stacked_state_args(states, layer, prev, n_inputs, out_index):
    tail = states.shape[2:]
    spec = pl.BlockSpec((None, 8) + tail, lambda i, _l=layer: (_l, i) + (0,) * len(tail))
    if prev is None:
        prev = jnp.zeros(states.shape, F32)
    return spec, [pl.BlockSpec(memory_space=pl.ANY)], [prev], {n_inputs: out_index}


def _gla_step(q, k, g, v, r, states, layer, prev, nw):
    n = r.shape[0]
    slab = lambda c: pl.BlockSpec((c, 8, LANES), lambda i: (0, i, 0))
    sspec, extra_specs, extra_args, aliases = _stacked_state_args(states, layer, prev, 7, 1)
    return pl.pallas_call(
        _gla_step_kernel,
        grid=(n // 8,),
        in_specs=[slab(4), slab(4), slab(4), slab(8), pl.BlockSpec((8, 1024), lambda i: (i, 0)), sspec,
                  _const_spec(nw.shape)] + extra_specs,
        out_specs=(pl.BlockSpec((8, 1024), lambda i: (i, 0)), sspec),
        out_shape=(jax.ShapeDtypeStruct((n, 1024), F32), jax.ShapeDtypeStruct(states.shape, F32)),
        input_output_aliases=aliases,
        compiler_params=_params(),
        name="gla_step",
    )(q, k, g, v, r, states, nw, *extra_args)


def _s5_prep_kernel(lre_ref, lim_ref, ldt_ref, lre_r_ref, lim_r_ref, ldt_r_ref, bre_ref, bim_ref,
                    ar_ref, ai_ref, bbre_ref, bbim_ref):
    def disc(lre, lim, ldt):
        dt = jnp.exp(ldt)
        mag = jnp.exp(lre * dt)
        ar = mag * jnp.cos(lim * dt)
        ai = mag * jnp.sin(lim * dt)
        return ar, ai

    ar, ai = disc(lre_ref[...], lim_ref[...], ldt_ref[...])
    ar_ref[...] = ar
    ai_ref[...] = ai
    lre, lim = lre_r_ref[...], lim_r_ref[...]
    ar, ai = disc(lre, lim, ldt_r_ref[...])
    den = lre * lre + lim * lim
    wr = ((ar - 1.0) * lre + ai * lim) / den
    wi = (ai * lre - (ar - 1.0) * lim) / den
    bre, bim = bre_ref[...], bim_ref[...]
    bbre_ref[...] = wr * bre - wi * bim
    bbim_ref[...] = wr * bim + wi * bre


def _s5_prep(lam_re, lam_im, log_dt, b_re, b_im):
    n = lam_re.shape[0]
    rows = n * S5_GROUPS
    lre = lam_re.reshape(rows, S5_P)
    lim = lam_im.reshape(rows, S5_P)
    ldt = jnp.broadcast_to(log_dt.reshape(rows, 1), (rows, S5_P))
    rep = lambda a: jnp.repeat(a, S5_GROUP_CH, axis=1)
    args = (lre, lim, ldt, rep(lre), rep(lim), rep(ldt),
            b_re.reshape(rows, S5_P * S5_GROUP_CH), b_im.reshape(rows, S5_P * S5_GROUP_CH))
    wide = jax.ShapeDtypeStruct((rows, S5_P * S5_GROUP_CH), F32)
    narrow = jax.ShapeDtypeStruct((rows, S5_P), F32)
    ar, ai, bbre, bbim = pl.pallas_call(
        _s5_prep_kernel, out_shape=(narrow, narrow, wide, wide), name="s5_prep")(*args)
    shp = (n, S5_GROUPS, S5_P, S5_GROUP_CH)
    return ar.reshape(n, S5_GROUPS, S5_P), ai.reshape(n, S5_GROUPS, S5_P), bbre.reshape(shp), bbim.reshape(shp)


def _s5_kernel(u_ref, sg_ref, h0re_ref, h0im_ref, wbre_ref, wbim_ref, wc_ref, are_ref, aim_ref, d_ref,
               wglu_ref, bglu_ref, y_ref, hre_out, him_out, hre, him, cre, cim, ys, *, nb, tokens):
    step = pl.program_id(0)
    rows = nb * tokens

    @pl.when(step == 0)
    def _():
        cre[...] = h0re_ref[...]
        cim[...] = h0im_ref[...]

    for m in range(4):
        um = u_ref[:, m * LANES:(m + 1) * LANES].astype(BF16)
        hre[:, m * 512:(m + 1) * 512] = jnp.dot(um, wbre_ref[m], preferred_element_type=F32)
        him[:, m * 512:(m + 1) * 512] = jnp.dot(um, wbim_ref[m], preferred_element_type=F32)

    for qd in range(2):
        ql = slice(qd * 1024, (qd + 1) * 1024)
        ar = jnp.broadcast_to(are_ref[:, ql], (nb, 1024))
        ai = jnp.broadcast_to(aim_ref[:, ql], (nb, 1024))

        def tok(t, carry, ql=ql, ar=ar, ai=ai):
            hr, hi = carry
            sel = pl.ds(pl.multiple_of(t * nb, nb), nb)
            nr = ar * hr - ai * hi + hre[sel, ql]
            ni = ar * hi + ai * hr + him[sel, ql]
            hre[sel, ql] = nr
            him[sel, ql] = ni
            return nr, ni

        hr, hi = lax.fori_loop(0, tokens, tok, (cre[:, ql], cim[:, ql]))
        cre[:, ql] = hr
        cim[:, ql] = hi

    for m in range(4):
        hc = jnp.concatenate([hre[:, m * 512:(m + 1) * 512], him[:, m * 512:(m + 1) * 512]], axis=1)
        sl = slice(m * LANES, (m + 1) * LANES)
        y = _mm(hc, wc_ref[m]) + d_ref[:, sl] * u_ref[:, sl]
        ys[:, sl] = jax.nn.gelu(y)
    y = ys[...]
    gate = _sigmoid(_mm(y, wglu_ref[...]) + bglu_ref[...])
    y_ref[...] = (y * gate * _silu(sg_ref[...])).astype(BF16)

    @pl.when(step == pl.num_programs(0) - 1)
    def _():
        hre_out[...] = cre[...]
        him_out[...] = cim[...]


def _s5(u, sg, h0re, h0im, w, nb, tokens):
    t = u.shape[0]
    rows = nb * tokens
    row = lambda c: pl.BlockSpec((rows, c), lambda i: (i, 0))
    weights = (w["wbre"], w["wbim"], w["wc"], w["are"], w["aim"], w["d"], w["wglu"], w["bglu"])
    state = jax.ShapeDtypeStruct((nb, S5_STATE), F32)
    return pl.pallas_call(
        functools.partial(_s5_kernel, nb=nb, tokens=tokens),
        grid=(t // rows,),
        in_specs=[row(S5_WIDTH), row(S5_WIDTH), _const_spec(h0re.shape), _const_spec(h0im.shape)]
        + [_const_spec(a.shape) for a in weights],
        out_specs=(row(S5_WIDTH), _const_spec((nb, S5_STATE)), _const_spec((nb, S5_STATE))),
        out_shape=(jax.ShapeDtypeStruct((t, S5_WIDTH), BF16), state, state),
        scratch_shapes=[
            pltpu.VMEM((rows, S5_STATE), F32),
            pltpu.VMEM((rows, S5_STATE), F32),
            pltpu.VMEM((nb, S5_STATE), F32),
            pltpu.VMEM((nb, S5_STATE), F32),
            pltpu.VMEM((rows, S5_WIDTH), F32),
        ],
        compiler_params=_params(),
        name="s5",
    )(u, sg, h0re, h0im, *weights)


def _dn_solve(lms, rhss):
    nblk = CHUNK // SUB
    row = lax.broadcasted_iota(jnp.int32, (SUB, LANES), 0)
    lane = lax.broadcasted_iota(jnp.int32, (SUB, LANES), 1)
    seg = (lane // SUB) * SUB
    in_diag = [(lane >= SUB * blk) & (lane < SUB * (blk + 1)) for blk in range(nblk)]
    eye_pack = jnp.where((lane - seg == row) & (lane < CHUNK), 1.0, 0.0)
    zpad = jnp.zeros((CHUNK, LANES - CHUNK), F32)
    t_invs, lowers = [], []
    for lm in lms:
        wide = jnp.concatenate([lm, zpad], axis=1)
        tiles = [wide[SUB * blk:SUB * (blk + 1)] for blk in range(nblk)]
        dpack = jnp.zeros((SUB, LANES), F32)
        for blk in range(nblk):
            dpack = jnp.where(in_diag[blk], tiles[blk], dpack)
        inv = eye_pack
        for j in range(SUB - 1):
            col_j = jnp.take_along_axis(dpack, seg + j, axis=1)
            inv = inv - col_j * inv[j:j + 1, :]
        t_invs.append(jnp.concatenate(
            [jnp.where(in_diag[blk], inv, 0.0)[:, :CHUNK] for blk in range(nblk)], axis=0).astype(BF16))
        lowers.append(jnp.concatenate(
            [jnp.where(in_diag[blk], 0.0, tiles[blk])[:, :CHUNK] for blk in range(nblk)], axis=0).astype(BF16))
    d = lambda a, b: jnp.dot(a, b, preferred_element_type=F32)
    his = [rhs.astype(BF16) for rhs in rhss]
    los = [(rhs - hi.astype(F32)).astype(BF16) for rhs, hi in zip(rhss, his)]
    width = rhss[0].shape[1]
    applied = [d(t, jnp.concatenate([hi, lo_, lw], axis=1)) for t, hi, lo_, lw in zip(t_invs, his, los, lowers)]
    trs = [ap[:, :width] + ap[:, width:2 * width] for ap in applied]
    tns = [ap[:, 2 * width:] for ap in applied]
    xs = [[tr[0:SUB]] for tr in trs]
    for blk in range(1, nblk):
        lo = SUB * blk
        curs = [tr[lo:lo + SUB] - _mm(tn[lo:lo + SUB, 0:lo], jnp.concatenate(x, axis=0))
                for tr, tn, x in zip(trs, tns, xs)]
        for x, cur in zip(xs, curs):
            x.append(cur)
    return [jnp.concatenate(x, axis=0) for x in xs]


def _dn_prompt_kernel(x_ref, z_ref, beta_ref, gd_ref, cw_ref, nw_ref, c0_ref,
                      og_ref, sout_ref, cout_ref,
                      xs, qs, ks, vs, bcol, gcol, gc, gt, oscr, *, nb):
    rows = CHUNK * nb
    hist = (CONV_W - 1) * nb
    pad = PAD_TOKENS * nb
    step = pl.program_id(0)
    st = sout_ref

    @pl.when(step == 0)
    def _():
        st[...] = jnp.zeros_like(st)
        xs[0:hist, :] = c0_ref[...]
        gc[0, 0:pad, :] = jnp.zeros((pad, LANES), F32)

    xs[hist:hist + 64, :] = x_ref[0:64, :]

    def conv_rows(src, base, r0):
        tap = lambda i: base + r0 + i * nb if isinstance(r0, int) else pl.multiple_of(base + r0 + i * nb, 8)
        for c in range(DN_CONV_DIM // LANES):
            cl = slice(c * LANES, (c + 1) * LANES)
            y = src[pl.ds(tap(0), 64), cl] * cw_ref[0:1, cl]
            for i in range(1, CONV_W):
                y = y + src[pl.ds(tap(i), 64), cl] * cw_ref[i:i + 1, cl]
            y = _silu(y)
            if c < 2 * DN_HEADS:
                y = y * lax.rsqrt(jnp.sum(y * y, axis=-1, keepdims=True) + EPS)
            if c < DN_HEADS:
                qs[c, pl.ds(r0, 64), :] = y * (DN_DK ** -0.5)
            elif c < 2 * DN_HEADS:
                ks[c - DN_HEADS, pl.ds(r0, 64), :] = y
            else:
                vs[c - 2 * DN_HEADS, pl.ds(r0, 64), :] = y

    conv_rows(xs, 0, 0)

    def conv_tile(ti, carry):
        conv_rows(x_ref, -hist, pl.multiple_of(ti * 64, 64))
        return carry

    lax.fori_loop(1, rows // 64, conv_tile, 0)

    xs[0:hist, :] = x_ref[rows - hist:rows, :]

    gc[0, pad:pad + rows, :] = gd_ref[...]
    _cumsum_tokens(gc, 0, pad, rows, nb)

    def bc_tile(ti, carry):
        r0 = pl.multiple_of(ti * 64, 64)
        gtile = gc[0, pl.ds(pad + r0, 64), :]
        btile = beta_ref[pl.ds(r0, 64), :]
        for h in range(DN_HEADS):
            gcol[h, pl.ds(r0, 64), :] = jnp.broadcast_to(gtile[:, h:h + 1], (64, LANES))
            bcol[h, pl.ds(r0, 64), :] = jnp.broadcast_to(btile[:, h:h + 1], (64, LANES))
        return carry

    lax.fori_loop(0, rows // 64, bc_tile, 0)

    ii = lax.broadcasted_iota(jnp.int32, (CHUNK, CHUNK), 0)
    jj = lax.broadcasted_iota(jnp.int32, (CHUNK, CHUNK), 1)
    eye = (ii == jj).astype(BF16)

    parts = [_split3(gc[0, pl.ds(pad + b, CHUNK, stride=nb), :]) for b in range(nb)]
    tdot = lambda x: lax.dot_general(x, eye, (((0,), (0,)), ((), ())), preferred_element_type=F32)
    firsts = [tdot(p[0]) for p in parts]
    seconds = [tdot(p[1]) for p in parts]
    thirds = [tdot(p[2]) for p in parts]
    for b in range(nb):
        gt[b] = (firsts[b] + seconds[b]) + thirds[b]

    def per_batch(bi, carry):
        def head_group(gi, carry2):
            probs = [(bi * DN_BATCH + jb, gi * DN_GROUP + jh) for jb in range(DN_BATCH) for jh in range(DN_GROUP)]
            sels = [pl.ds(b, CHUNK, stride=nb) for b, _ in probs]
            loaded = [(qs[h, sel, :], ks[h, sel, :], vs[h, sel, :], bcol[h, sel, :], gcol[h, sel, :],
                       gt[b, pl.ds(h, 1), :], st[b, h]) for (b, h), sel in zip(probs, sels)]
            qbs, kbs, vbs, bcs, gcls, grows, s_olds = zip(*loaded)
            decs = [jnp.where(jj <= ii, jnp.exp(jnp.where(jj <= ii, gcl[:, :CHUNK] - grow, 0.0)), 0.0)
                    for gcl, grow in zip(gcls, grows)]
            kqs = [_mm_nt(jnp.concatenate([kb, qb], axis=0), kb) for qb, kb in zip(qbs, kbs)]
            kks = [kq[:CHUNK] for kq in kqs]
            qks = [kq[CHUNK:] for kq in kqs]
            lms = [jnp.where(jj < ii, bc[:, :CHUNK] * kk * dec, 0.0) for bc, kk, dec in zip(bcs, kks, decs)]
            egs = [jnp.exp(gcl) for gcl in gcls]
            rhss = [jnp.concatenate([vb * bc, kb * (bc * eg)], axis=1)
                    for vb, kb, bc, eg in zip(vbs, kbs, bcs, egs)]
            sols = _dn_solve(lms, rhss)
            v_news = [sol[:, :DN_DV] - _mm(sol[:, DN_DV:], s_old) for sol, s_old in zip(sols, s_olds)]
            os_ = [_mm(qb * eg, s_old) + _mm(qk * dec, v_new)
                   for qb, eg, s_old, qk, dec, v_new in zip(qbs, egs, s_olds, qks, decs, v_news)]
            glasts = [gcl[CHUNK - 1:CHUNK, :] for gcl in gcls]
            s_news = [s_old * jnp.exp(glast) + _mm_tn(kb * jnp.exp(glast - gcl), v_new)
                      for s_old, glast, kb, gcl, v_new in zip(s_olds, glasts, kbs, gcls, v_news)]
            for (b, h), sel, o, s_new in zip(probs, sels, os_, s_news):
                st[b, h] = s_new
                oscr[h, sel, :] = o
            return carry2

        lax.fori_loop(0, DN_HEADS // DN_GROUP, head_group, 0)
        return carry

    lax.fori_loop(0, nb // DN_BATCH, per_batch, 0)

    nw = nw_ref[...]

    def epi_tile(ti, carry):
        r0 = pl.multiple_of(ti * 64, 64)
        for h in range(DN_HEADS):
            o = oscr[h, pl.ds(r0, 64), :]
            inv = lax.rsqrt(jnp.mean(o * o, axis=-1, keepdims=True) + EPS)
            cl = slice(h * LANES, (h + 1) * LANES)
            og_ref[pl.ds(r0, 64), cl] = (o * inv * nw * _silu(z_ref[pl.ds(r0, 64), cl])).astype(BF16)
        return carry

    lax.fori_loop(0, rows // 64, epi_tile, 0)

    @pl.when(step == pl.num_programs(0) - 1)
    def _():
        cout_ref[...] = xs[0:hist, :]


def _dn_prompt(xqkv, z, beta, gd, cw, nw, c0, nb):
    t = z.shape[0]
    rows = CHUNK * nb
    hist = (CONV_W - 1) * nb
    row = lambda c: pl.BlockSpec((rows, c), lambda i: (i, 0))
    slab = pltpu.VMEM((DN_HEADS, rows, LANES), F32)
    return pl.pallas_call(
        functools.partial(_dn_prompt_kernel, nb=nb),
        grid=(t // rows,),
        in_specs=[row(DN_CONV_DIM), row(1024), row(LANES), row(LANES), _const_spec(cw.shape),
                  _const_spec(nw.shape), _const_spec(c0.shape)],
        out_specs=(row(1024), _const_spec((nb, DN_HEADS, DN_DK, DN_DV)), _const_spec((hist, DN_CONV_DIM))),
        out_shape=(jax.ShapeDtypeStruct((t, 1024), BF16),
                   jax.ShapeDtypeStruct((nb, DN_HEADS, DN_DK, DN_DV), F32),
                   jax.ShapeDtypeStruct((hist, DN_CONV_DIM), F32)),
        scratch_shapes=[
            pltpu.VMEM((hist + 64, DN_CONV_DIM), F32),
            slab, slab, slab, slab, slab,
            pltpu.VMEM((1, PAD_TOKENS * nb + rows, LANES), F32),
            pltpu.VMEM((nb, LANES, CHUNK), F32),
            slab,
        ],
        compiler_params=_params(),
        name="dn_prompt",
    )(xqkv, z, beta, gd, cw, nw, c0)


def _dn_step_kernel(x_ref, cb_ref, z_ref, beta_ref, gd_ref, s_ref, cw_ref, nw_ref, *rest):
    og_ref, sout_ref, cn_ref = rest[-3:]
    nw = nw_ref[...]
    cn_ref[0] = cb_ref[1]
    cn_ref[1] = cb_ref[2]
    cn_ref[2] = x_ref[...]
    beta = beta_ref[...]
    eg_all = jnp.exp(gd_ref[...])

    def conv(c):
        cl = slice(c * LANES, (c + 1) * LANES)
        y = cb_ref[0, :, cl] * cw_ref[0:1, cl]
        y = y + cb_ref[1, :, cl] * cw_ref[1:2, cl]
        y = y + cb_ref[2, :, cl] * cw_ref[2:3, cl]
        y = y + x_ref[:, cl] * cw_ref[3:4, cl]
        return _silu(y)

    def l2(y):
        return y * lax.rsqrt(jnp.sum(y * y, axis=-1, keepdims=True) + EPS)

    prow = lax.broadcasted_iota(jnp.int32, (16, 2 * LANES), 0)
    plane = lax.broadcasted_iota(jnp.int32, (16, 2 * LANES), 1)
    pick = [jnp.where(((prow == n) & (plane < LANES)) | ((prow == 8 + n) & (plane >= LANES)), 1.0, 0.0).astype(BF16)
            for n in range(8)]

    for h in range(DN_HEADS):
        qh = l2(conv(h)) * (DN_DK ** -0.5)
        kh = l2(conv(DN_HEADS + h))
        vh = conv(2 * DN_HEADS + h)
        kq = jnp.concatenate([kh, qh], axis=0).astype(BF16)
        cols = [lax.dot_general(kq, pick[n], (((0,), (0,)), ((), ())), preferred_element_type=F32)
                for n in range(8)]
        outs = []
        for n in range(8):
            s_old = s_ref[n, h]
            kcol = cols[n][:, :LANES]
            eg = eg_all[n:n + 1, h:h + 1]
            ks_row = jnp.sum(kcol * s_old, axis=0, keepdims=True)
            v_new = beta[n:n + 1, h:h + 1] * (vh[n:n + 1, :] - eg * ks_row)
            s_new = s_old * eg + kcol * v_new
            sout_ref[n, h] = s_new
            outs.append(jnp.sum(cols[n][:, LANES:] * s_new, axis=0, keepdims=True))
        o = jnp.concatenate(outs, axis=0)
        cl = slice(h * LANES, (h + 1) * LANES)
        og_ref[:, cl] = o * lax.rsqrt(jnp.mean(o * o, axis=-1, keepdims=True) + EPS) * nw * _silu(z_ref[:, cl])


def _dn_step(xqkv, cbuf, z, beta, gd, states, layer, prev, cw, nw):
    n = z.shape[0]
    row = lambda c: pl.BlockSpec((8, c), lambda i: (i, 0))
    cspec = pl.BlockSpec((CONV_W - 1, 8, DN_CONV_DIM), lambda i: (0, i, 0))
    sspec, extra_specs, extra_args, aliases = _stacked_state_args(states, layer, prev, 8, 1)
    return pl.pallas_call(
        _dn_step_kernel,
        grid=(n // 8,),
        in_specs=[row(DN_CONV_DIM), cspec, row(1024), row(LANES), row(LANES), sspec,
                  _const_spec(cw.shape), _const_spec(nw.shape)] + extra_specs,
        out_specs=(row(1024), sspec, cspec),
        out_shape=(jax.ShapeDtypeStruct((n, 1024), F32), jax.ShapeDtypeStruct(states.shape, F32),
                   jax.ShapeDtypeStruct(cbuf.shape, F32)),
        input_output_aliases=aliases,
        compiler_params=_params(),
        name="dn_step",
    )(xqkv, cbuf, z, beta, gd, states, cw, nw, *extra_args)


def _pad_cols(a, n):
    return jnp.pad(a, ((0, 0), (0, n - a.shape[1])))


def _even_weights(w_in, w_gate_up, b_gate, w_out):
    return {
        "w": _pad_cols(w_in.astype(BF16), W_IN_PAD),
        "wg": jnp.pad(w_gate_up, ((0, LANES - GLA_RANK), (0, 0))).astype(BF16),
        "bg": b_gate.reshape(1, -1),
        "wo": w_out.astype(BF16),
    }


def _s5_weights(ar, ai, bbre, bbim, c_re, c_im, d, w_glu, b_glu):
    eye = jnp.eye(8, dtype=F32)

    def pack_b(bb):
        t = bb.reshape(4, 8, S5_P, S5_GROUP_CH).transpose(0, 1, 3, 2)
        return (t[:, :, :, None, :] * eye[None, :, None, :, None]).reshape(4, 128, 512).astype(BF16)

    def pack_c(c):
        t = c.reshape(4, 8, S5_GROUP_CH, S5_P).transpose(0, 1, 3, 2)
        return (t[:, :, :, None, :] * eye[None, :, None, :, None]).reshape(4, 512, 128)

    return {
        "wbre": pack_b(bbre), "wbim": pack_b(bbim),
        "wc": jnp.concatenate([pack_c(c_re), -pack_c(c_im)], axis=1).astype(BF16),
        "are": ar.reshape(1, S5_STATE), "aim": ai.reshape(1, S5_STATE),
        "d": d.reshape(1, -1), "wglu": w_glu.astype(BF16), "bglu": b_glu.reshape(1, -1),
    }


def _odd_weights(w_in, conv_w, a_log, dt_bias, w_out):
    return {
        "w": _pad_cols(w_in.astype(BF16), W_IN_PAD),
        "alog": _pad_cols(a_log.reshape(1, -1), LANES), "dtb": _pad_cols(dt_bias.reshape(1, -1), LANES),
        "cw": jnp.pad(conv_w, ((0, 8 - CONV_W), (0, 0))),
        "wo": w_out.astype(BF16),
    }


def kernel(x_prompt, x_sample, state_gla, state_s5_re, state_s5_im, state_delta, state_conv, norm_w, final_norm_w, w_in_even, gla_w_gate_up, gla_b_gate, gla_norm_w, s5_lambda_re, s5_lambda_im, s5_log_dt, s5_b_re, s5_b_im, s5_c_re, s5_c_im, s5_d, s5_w_glu, s5_b_glu, w_out_even, w_in_odd, dn_conv_w, dn_a_log, dn_dt_bias, dn_norm_w, w_out_odd):
    nb, seq, _ = x_prompt.shape
    ns = x_sample.shape[0]
    assert seq % CHUNK == 0 and nb % 8 == 0 and ns % 8 == 0 and x_sample.shape[1] == 1
    depth = norm_w.shape[0]

    assert depth % 2 == 0 and state_gla.shape[0] == depth // 2 and state_delta.shape[0] == depth // 2
    xp = None
    xs = x_sample.reshape(ns, D_MODEL)
    fw = final_norm_w.reshape(1, -1)

    ar, ai, bbre, bbim = _s5_prep(s5_lambda_re, s5_lambda_im, s5_log_dt, s5_b_re, s5_b_im)

    gla_p, s5r_p, s5i_p, s5r_s, s5i_s, dn_p, cv_p, cv_s = ([] for _ in range(8))
    gla_s = dn_s = None
    for layer in range(depth):
        i = layer // 2
        nw = norm_w[layer].reshape(1, -1)
        if layer % 2 == 0:
            w = _even_weights(w_in_even[i], gla_w_gate_up[i], gla_b_gate[i], w_out_even[i])
            w5 = _s5_weights(ar[i], ai[i], bbre[i], bbim[i], s5_c_re[i], s5_c_im[i], s5_d[i],
                             s5_w_glu[i], s5_b_glu[i])
            gnw = gla_norm_w[i].reshape(1, -1)
            if layer == 0:
                q, k, g, v, r, u, sg, xp = _proj_even(x_prompt, nw, w, batch_major=True)
            else:
                q, k, g, v, r, u, sg = _proj_even(xp, nw, w)
            og, s_new = _gla_prompt(q, k, g, v, r, gnw, nb)
            zero = jnp.zeros((nb, S5_STATE), F32)
            y5, hre, him = _s5(u, sg, zero, zero, w5, nb, CHUNK)
            xp = _out_even(xp, og, y5, w["wo"])
            gla_p.append(s_new)
            s5r_p.append(hre.reshape(nb, S5_GROUPS, S5_P))
            s5i_p.append(him.reshape(nb, S5_GROUPS, S5_P))
            q, k, g, v, r, u, sg = _proj_even(xs, nw, w)
            og, gla_s = _gla_step(q, k, g, v, r, state_gla, i, gla_s, gnw)
            y5, hre, him = _s5(u, sg, state_s5_re[i].reshape(ns, S5_STATE), state_s5_im[i].reshape(ns, S5_STATE),
                               w5, ns, 1)
            xs = _out_even(xs, og, y5, w["wo"])
            s5r_s.append(hre.reshape(ns, S5_GROUPS, S5_P))
            s5i_s.append(him.reshape(ns, S5_GROUPS, S5_P))
        else:
            w = _odd_weights(w_in_odd[i], dn_conv_w[i], dn_a_log[i], dn_dt_bias[i], w_out_odd[i])
            dnw = dn_norm_w[i].reshape(1, -1)
            final = layer == depth - 1
            xqkv, z, beta, gd = _proj_odd(xp, nw, w)
            c0 = jnp.zeros(((CONV_W - 1) * nb, DN_CONV_DIM), F32)
            og, s_new, c_new = _dn_prompt(xqkv, z, beta, gd, w["cw"], dnw, c0, nb)
            xp = _out_odd(xp, og, w["wo"], fw, final, nb_out=nb if final else 0)
            dn_p.append(s_new)
            cv_p.append(c_new.reshape(CONV_W - 1, nb, DN_CONV_DIM).transpose(1, 0, 2))
            xqkv, z, beta, gd = _proj_odd(xs, nw, w)
            og, dn_s, c_new = _dn_step(xqkv, state_conv[i].transpose(1, 0, 2), z, beta, gd, state_delta, i, dn_s,
                                       w["cw"], dnw)
            xs = _out_odd(xs, og, w["wo"], fw, final)
            cv_s.append(c_new.transpose(1, 0, 2))

    y_prompt = xp
    y_sample = xs.reshape(ns, 1, D_MODEL)
    st = jnp.stack
    return (y_prompt, y_sample, st(gla_p), gla_s, st(s5r_p), st(s5i_p), st(s5r_s), st(s5i_s),
            st(dn_p), dn_s, st(cv_p), st(cv_s))
```

```python
import functools
import math

import jax
import jax.numpy as jnp
from jax import lax
from jax.experimental import pallas as pl
from jax.experimental.pallas import tpu as pltpu

F32 = jnp.float32
BF16 = jnp.bfloat16
EPS = 1e-6

D_MODEL = 1024
GLA_HEADS, GLA_DK, GLA_DV, GLA_RANK = 4, 128, 256, 16
GLA_GATE_NORM = 16.0
S5_GROUPS, S5_GROUP_CH, S5_P, S5_WIDTH = 32, 16, 64, 512
S5_STATE = S5_GROUPS * S5_P
DN_HEADS, DN_DK, DN_DV = 8, 128, 128
DN_CONV_DIM, CONV_W = 3072, 4
CHUNK = 64
SUB = 8
GLA_BAND = 8
GLA_BATCH = 4
GLA_SAFE_SPAN = 60.0
PAD_TOKENS = 16
DN_BATCH = 4
DN_GROUP = 8
LANES = 128
W_IN_PAD = 4224
VMEM_LIMIT = 56 * 1024 * 1024


def _mm(a, b):
    return jnp.dot(a.astype(BF16), b.astype(BF16), preferred_element_type=F32)


def _mm_nt(a, b):
    return lax.dot_general(a.astype(BF16), b.astype(BF16), (((1,), (1,)), ((), ())),
                           preferred_element_type=F32)


def _mm_tn(a, b):
    return lax.dot_general(a.astype(BF16), b.astype(BF16), (((0,), (0,)), ((), ())),
                           preferred_element_type=F32)


def _split3(a):
    a1 = a.astype(BF16)
    r1 = a - a1.astype(F32)
    a2 = r1.astype(BF16)
    a3 = (r1 - a2.astype(F32)).astype(BF16)
    return a1, a2, a3


def _mm_tn_exact(a, m_bf16):
    a1, a2, a3 = _split3(a)
    d = lambda x: lax.dot_general(x, m_bf16, (((0,), (0,)), ((), ())), preferred_element_type=F32)
    return (d(a1) + d(a2)) + d(a3)


def _sigmoid(x):
    return 1.0 / (1.0 + jnp.exp(-x))


def _silu(x):
    return x * _sigmoid(x)


def _softplus(x):
    return jnp.maximum(x, 0.0) + jnp.log1p(jnp.exp(-jnp.abs(x)))


def _rms_rows(x, w):
    ms = jnp.mean(x * x, axis=-1, keepdims=True)
    return x * lax.rsqrt(ms + EPS) * w


def _const_spec(shape):
    nd = len(shape)
    return pl.BlockSpec(shape, lambda i, _nd=nd: (0,) * _nd)


def _params(**flags):
    return pltpu.CompilerParams(dimension_semantics=("arbitrary",), vmem_limit_bytes=VMEM_LIMIT,
                                flags=flags or None)


def _row_tile(t):
    return min(512, t)


def _add_pending(x, pending):
    if len(pending) == 2:
        og_ref, wo_ref = pending
        return x + _mm(og_ref[...], wo_ref[...])
    og_ref, y5_ref, wo_ref = pending
    return x + (_mm(og_ref[...], wo_ref[0:1024, :]) + _mm(y5_ref[...], wo_ref[1024:, :]))


def _proj_even_kernel(x_ref, nw_ref, w_ref, wg_ref, bg_ref, *rest, nb_in, n_pending):
    pending = rest[:n_pending]
    q_ref, k_ref, g_ref, v_ref, r_ref, u_ref, sg_ref = rest[n_pending:n_pending + 7]
    tail = rest[n_pending + 7:]
    if nb_in:
        xo_ref, xt = tail
        for b in range(nb_in):
            for c in range(D_MODEL // LANES):
                xt[c, pl.ds(b, x_ref.shape[1], stride=nb_in), :] = x_ref[b, :, c * LANES:(c + 1) * LANES]
        x = jnp.concatenate([xt[c] for c in range(D_MODEL // LANES)], axis=1)
        xo_ref[...] = x
    elif n_pending:
        xo_ref, = tail
        x = _add_pending(x_ref[...], pending)
        xo_ref[...] = x
    else:
        x = x_ref[...]
    hb = _rms_rows(x, nw_ref[...]).astype(BF16)
    d = lambda lo, hi: jnp.dot(hb, w_ref[:, lo:hi], preferred_element_type=F32)
    q = d(0, 512) * (GLA_DK ** -0.5)
    k = d(512, 1024)
    lr_u = d(3072, 3712)
    u_sg = d(3584, 4224)
    logit = jnp.dot(lr_u[:, :LANES].astype(BF16), wg_ref[...], preferred_element_type=F32) + bg_ref[...]
    g = -_softplus(-logit) / GLA_GATE_NORM
    for h in range(GLA_HEADS):
        sl = slice(h * LANES, (h + 1) * LANES)
        q_ref[h] = q[:, sl]
        k_ref[h] = k[:, sl]
        g_ref[h] = g[:, sl]
    v = d(1024, 2048)
    for s in range(2 * GLA_HEADS):
        v_ref[s] = v[:, s * LANES:(s + 1) * LANES]
    r_ref[...] = d(2048, 3072)
    u_ref[...] = lr_u[:, GLA_RANK:GLA_RANK + S5_WIDTH]
    sg_ref[...] = u_sg[:, GLA_RANK:GLA_RANK + S5_WIDTH]


def _pending_specs(pending, tm):
    *acts, wo = pending
    return [pl.BlockSpec((tm, a.shape[1]), lambda i: (i, 0)) for a in acts] + [_const_spec(wo.shape)]


def _proj_even(x, nw, w, batch_major=False, pending=()):
    nb_in = x.shape[0] if batch_major else 0
    t = x.shape[0] * x.shape[1] if batch_major else x.shape[0]
    tm = _row_tile(t)
    row = lambda c: pl.BlockSpec((tm, c), lambda i: (i, 0))
    slab = lambda n: pl.BlockSpec((n, tm, LANES), lambda i: (0, i, 0))
    x_spec = pl.BlockSpec((nb_in, tm // nb_in, D_MODEL), lambda i: (0, i, 0)) if batch_major else row(D_MODEL)
    new_x = batch_major or bool(pending)
    extra_out_shape = (jax.ShapeDtypeStruct((t, D_MODEL), F32),) if new_x else ()
    extra_out_specs = (row(D_MODEL),) if new_x else ()
    scratch = [pltpu.VMEM((D_MODEL // LANES, tm, LANES), F32)] if batch_major else []
    pend_specs = _pending_specs(pending, tm) if pending else []
    out_shape = (
        jax.ShapeDtypeStruct((GLA_HEADS, t, LANES), F32),
        jax.ShapeDtypeStruct((GLA_HEADS, t, LANES), F32),
        jax.ShapeDtypeStruct((GLA_HEADS, t, LANES), F32),
        jax.ShapeDtypeStruct((2 * GLA_HEADS, t, LANES), F32),
        jax.ShapeDtypeStruct((t, 1024), F32),
        jax.ShapeDtypeStruct((t, S5_WIDTH), F32),
        jax.ShapeDtypeStruct((t, S5_WIDTH), F32),
    )
    weights = (w["w"], w["wg"], w["bg"])
    return pl.pallas_call(
        functools.partial(_proj_even_kernel, nb_in=nb_in, n_pending=len(pending)),
        grid=(t // tm,),
        in_specs=[x_spec, _const_spec(nw.shape)] + [_const_spec(a.shape) for a in weights] + pend_specs,
        out_specs=(slab(4), slab(4), slab(4), slab(8), row(1024), row(S5_WIDTH), row(S5_WIDTH)) + extra_out_specs,
        out_shape=out_shape + extra_out_shape,
        scratch_shapes=scratch,
        compiler_params=_params(),
        name="proj_even",
    )(x, nw, *weights, *pending)


def _proj_odd_kernel(x_ref, nw_ref, w_ref, alog_ref, dtb_ref, *rest, n_pending):
    pending = rest[:n_pending]
    xqkv_ref, z_ref, beta_ref, gd_ref = rest[n_pending:n_pending + 4]
    x = x_ref[...]
    if n_pending:
        xo_ref, = rest[n_pending + 4:]
        x = _add_pending(x, pending)
        xo_ref[...] = x
    hb = _rms_rows(x, nw_ref[...]).astype(BF16)
    d = lambda lo, hi: jnp.dot(hb, w_ref[:, lo:hi], preferred_element_type=F32)
    xqkv_ref[...] = d(0, DN_CONV_DIM)
    z_ref[...] = d(DN_CONV_DIM, DN_CONV_DIM + 1024)
    tail = d(DN_CONV_DIM + 1024, DN_CONV_DIM + 1024 + LANES)
    beta_ref[...] = _sigmoid(tail)
    a = pltpu.roll(tail, LANES - DN_HEADS, 1)
    gd_ref[...] = -jnp.exp(alog_ref[...]) * _softplus(a + dtb_ref[...])


def _proj_odd(x, nw, w, pending=()):
    t = x.shape[0]
    tm = _row_tile(t)
    row = lambda c: pl.BlockSpec((tm, c), lambda i: (i, 0))
    weights = (w["w"], w["alog"], w["dtb"])
    pend_specs = _pending_specs(pending, tm) if pending else []
    extra_out_shape = (jax.ShapeDtypeStruct((t, D_MODEL), F32),) if pending else ()
    extra_out_specs = (row(D_MODEL),) if pending else ()
    return pl.pallas_call(
        functools.partial(_proj_odd_kernel, n_pending=len(pending)),
        grid=(t // tm,),
        in_specs=[row(D_MODEL), _const_spec(nw.shape)] + [_const_spec(a.shape) for a in weights] + pend_specs,
        out_specs=(row(DN_CONV_DIM), row(1024), row(LANES), row(LANES)) + extra_out_specs,
        out_shape=(
            jax.ShapeDtypeStruct((t, DN_CONV_DIM), F32),
            jax.ShapeDtypeStruct((t, 1024), F32),
            jax.ShapeDtypeStruct((t, LANES), F32),
            jax.ShapeDtypeStruct((t, LANES), F32),
        ) + extra_out_shape,
        compiler_params=_params(),
        name="proj_odd",
    )(x, nw, *weights, *pending)


def _out_even_kernel(x_ref, og_ref, y5_ref, w_ref, o_ref):
    o_ref[...] = x_ref[...] + (_mm(og_ref[...], w_ref[0:1024, :]) + _mm(y5_ref[...], w_ref[1024:, :]))


def _out_even(x, og, y5, w):
    t = x.shape[0]
    tm = _row_tile(t)
    row = lambda c: pl.BlockSpec((tm, c), lambda i: (i, 0))
    return pl.pallas_call(
        _out_even_kernel,
        grid=(t // tm,),
        in_specs=[row(D_MODEL), row(1024), row(S5_WIDTH), _const_spec(w.shape)],
        out_specs=row(D_MODEL),
        out_shape=jax.ShapeDtypeStruct((t, D_MODEL), F32),
        compiler_params=_params(),
        name="out_even",
    )(x, og, y5, w)


def _out_odd_kernel(x_ref, og_ref, w_ref, fw_ref, o_ref, *slabs, final, nb_out):
    y = x_ref[...] + _mm(og_ref[...], w_ref[...])
    if final:
        y = _rms_rows(y, fw_ref[...])
    if nb_out:
        ys, = slabs
        for c in range(D_MODEL // LANES):
            ys[c] = y[:, c * LANES:(c + 1) * LANES]
        for b in range(nb_out):
            for c in range(D_MODEL // LANES):
                o_ref[b, :, c * LANES:(c + 1) * LANES] = ys[c, pl.ds(b, o_ref.shape[1], stride=nb_out), :]
    else:
        o_ref[...] = y


def _out_odd(x, og, w, fw, final, nb_out=0):
    t = x.shape[0]
    tm = _row_tile(t)
    row = lambda c: pl.BlockSpec((tm, c), lambda i: (i, 0))
    if nb_out:
        out_spec = pl.BlockSpec((nb_out, tm // nb_out, D_MODEL), lambda i: (0, i, 0))
        out_shape = jax.ShapeDtypeStruct((nb_out, t // nb_out, D_MODEL), F32)
        scratch = [pltpu.VMEM((D_MODEL // LANES, tm, LANES), F32)]
    else:
        out_spec, out_shape, scratch = row(D_MODEL), jax.ShapeDtypeStruct((t, D_MODEL), F32), []
    return pl.pallas_call(
        functools.partial(_out_odd_kernel, final=final, nb_out=nb_out),
        grid=(t // tm,),
        in_specs=[row(D_MODEL), row(1024), _const_spec(w.shape), _const_spec(fw.shape)],
        out_specs=out_spec,
        out_shape=out_shape,
        scratch_shapes=scratch,
        compiler_params=_params(),
        name="out_odd",
    )(x, og, w, fw)


def _cumsum_tokens(ref, lead, pad, rows, nb):
    shift = nb
    while shift <= pad:
        ref[lead, pad:pad + rows, :] = ref[lead, pad:pad + rows, :] + ref[lead, pad - shift:pad + rows - shift, :]
        shift *= 2
    while shift < rows:
        ref[lead, pad + shift:pad + rows, :] = (ref[lead, pad + shift:pad + rows, :]
                                                + ref[lead, pad:pad + rows - shift, :])
        shift *= 2


def _gla_prompt_kernel(q_ref, k_ref, g_ref, v_ref, r_ref, far_ref, near_ref, nw_ref, og_ref, sout_ref,
                       kp, bp, vp, st, oscr, ocar, *, nb):
    rows = CHUNK * nb
    pad = PAD_TOKENS * nb
    blk_rows = GLA_BAND * nb
    step = pl.program_id(0)

    @pl.when(step == 0)
    def _():
        st[...] = jnp.zeros_like(st)
        kp[:, 0:pad, :] = jnp.zeros((GLA_HEADS, pad, LANES), F32)
        bp[:, 0:pad, :] = jnp.zeros((GLA_HEADS, pad, LANES), F32)
        vp[:, 0:pad, :] = jnp.zeros((2 * GLA_HEADS, pad, LANES), F32)

    for h in range(GLA_HEADS):
        bp[h, pad:pad + rows, :] = g_ref[h]
        kp[h, pad:pad + rows, :] = k_ref[h]
        _cumsum_tokens(bp, h, pad, rows, nb)
    for s in range(2 * GLA_HEADS):
        vp[s, pad:pad + rows, :] = v_ref[s]

    nblk = CHUNK // GLA_BAND

    span = jnp.zeros((nb, LANES), F32)
    for h in range(GLA_HEADS):
        for blk in range(nblk):
            lo = pad + blk * blk_rows
            span = jnp.maximum(span, bp[h, lo - nb:lo, :] - bp[h, lo + blk_rows - nb:lo + blk_rows, :])
    safe = jnp.max(span) <= GLA_SAFE_SPAN

    def pairs_on_mxu(first_blk, mask_ref, accumulate):
        for h in range(GLA_HEADS):
            probs = []
            for blk in range(first_blk, nblk):
                lo = blk * blk_rows
                hi = lo + (1 - first_blk) * blk_rows
                ref_b = bp[h, pad + lo - nb:pad + lo, :]
                qs = q_ref[h, lo:lo + blk_rows, :] * jnp.exp(
                    bp[h, pad + lo:pad + lo + blk_rows, :] - jnp.concatenate([ref_b] * GLA_BAND, axis=0))
                ks = kp[h, pad:pad + hi, :] * jnp.exp(
                    jnp.concatenate([ref_b] * (hi // nb), axis=0) - bp[h, pad:pad + hi, :])
                probs.append((hi, _mm_nt(qs, ks) * mask_ref[lo:lo + blk_rows, 0:hi]))
            for blk in range(first_blk, nblk):
                lo = blk * blk_rows
                hi, p = probs[blk - first_blk]
                p = p.astype(BF16)
                for half in range(2):
                    sl = 2 * h + half
                    pv = jnp.dot(p, vp[sl, pad:pad + hi, :].astype(BF16), preferred_element_type=F32)
                    oscr[sl, lo:lo + blk_rows, :] = oscr[sl, lo:lo + blk_rows, :] + pv if accumulate else pv

    @pl.when(safe)
    def _():
        pairs_on_mxu(0, near_ref, False)

    def band_tile(ti, carry):
        r0 = pl.multiple_of(ti * 64, 64)
        for h in range(GLA_HEADS):
            qt = q_ref[h, pl.ds(r0, 64), :]
            bt = bp[h, pl.ds(pad + r0, 64), :]
            acc0 = jnp.zeros((64, LANES), F32)
            acc1 = jnp.zeros((64, LANES), F32)
            for d in range(GLA_BAND):
                off = pl.multiple_of(pad + r0 - d * nb, 8)
                ks = kp[h, pl.ds(off, 64), :]
                bs = bp[h, pl.ds(off, 64), :]
                w = jnp.sum(qt * ks * jnp.exp(bt - bs), axis=-1, keepdims=True)
                acc0 = acc0 + w * vp[2 * h, pl.ds(off, 64), :]
                acc1 = acc1 + w * vp[2 * h + 1, pl.ds(off, 64), :]
            oscr[2 * h, pl.ds(r0, 64), :] = acc0
            oscr[2 * h + 1, pl.ds(r0, 64), :] = acc1
        return carry

    @pl.when(jnp.logical_not(safe))
    def _():
        lax.fori_loop(0, rows // 64, band_tile, 0)
        pairs_on_mxu(1, far_ref, True)

    def per_batch(bi, carry):
        probs = [(bi * GLA_BATCH + jb, h) for jb in range(GLA_BATCH) for h in range(GLA_HEADS)]
        sels = [pl.ds(b, CHUNK, stride=nb) for b, _ in probs]
        loaded = [(q_ref[h, sel, :], kp[h, pl.ds(pad + b, CHUNK, stride=nb), :],
                   bp[h, pl.ds(pad + b, CHUNK, stride=nb), :], vp[2 * h, pl.ds(pad + b, CHUNK, stride=nb), :],
                   vp[2 * h + 1, pl.ds(pad + b, CHUNK, stride=nb), :], st[b, h])
                  for (b, h), sel in zip(probs, sels)]
        outs = [_mm_nt(qb * jnp.exp(bb), stt) for qb, _, bb, _, _, stt in loaded]
        news = []
        for _, kb, bb, v0, v1, stt in loaded:
            blast = bb[CHUNK - 1:CHUNK, :]
            kd = kb * jnp.exp(blast - bb)
            news.append(stt * jnp.exp(blast) + _mm_tn(jnp.concatenate([v0, v1], axis=1), kd))
        for (b, h), sel, out, new in zip(probs, sels, outs, news):
            ocar[2 * h, sel, :] = out[:, :LANES]
            ocar[2 * h + 1, sel, :] = out[:, LANES:]
            st[b, h] = new
        return carry

    lax.fori_loop(0, nb // GLA_BATCH, per_batch, 0)

    nw = nw_ref[...]

    def epi_tile(ti, carry):
        r0 = pl.multiple_of(ti * 64, 64)
        for h in range(GLA_HEADS):
            o0 = oscr[2 * h, pl.ds(r0, 64), :] + ocar[2 * h, pl.ds(r0, 64), :]
            o1 = oscr[2 * h + 1, pl.ds(r0, 64), :] + ocar[2 * h + 1, pl.ds(r0, 64), :]
            ms = (jnp.sum(o0 * o0, axis=-1, keepdims=True) + jnp.sum(o1 * o1, axis=-1, keepdims=True)) / GLA_DV
            inv = lax.rsqrt(ms + EPS)
            c0 = h * GLA_DV
            og_ref[pl.ds(r0, 64), c0:c0 + LANES] = (
                o0 * inv * nw[:, :LANES] * _silu(r_ref[pl.ds(r0, 64), c0:c0 + LANES])).astype(BF16)
            og_ref[pl.ds(r0, 64), c0 + LANES:c0 + 2 * LANES] = (
                o1 * inv * nw[:, LANES:] * _silu(r_ref[pl.ds(r0, 64), c0 + LANES:c0 + 2 * LANES])).astype(BF16)
        return carry

    lax.fori_loop(0, rows // 64, epi_tile, 0)

    @pl.when(step == pl.num_programs(0) - 1)
    def _():
        def wr(b, carry):
            for h in range(GLA_HEADS):
                sout_ref[b, h] = st[b, h].T
            return carry
        lax.fori_loop(0, nb, wr, 0)


def _pair_mask(nb, min_dist):
    n = CHUNK * nb
    r = jnp.arange(n)
    same = (r[:, None] % nb) == (r[None, :] % nb)
    far = (r[None, :] // nb) <= (r[:, None] // nb) - min_dist
    return (same & far).astype(F32)


def _gla_prompt(q, k, g, v, r, nw, nb):
    t = r.shape[0]
    rows = CHUNK * nb
    pad = PAD_TOKENS * nb
    slab = lambda n: pl.BlockSpec((n, rows, LANES), lambda i: (0, i, 0))
    row = lambda c: pl.BlockSpec((rows, c), lambda i: (i, 0))
    far = _pair_mask(nb, GLA_BAND)
    near = _pair_mask(nb, 0)
    return pl.pallas_call(
        functools.partial(_gla_prompt_kernel, nb=nb),
        grid=(t // rows,),
        in_specs=[slab(4), slab(4), slab(4), slab(8), row(1024), _const_spec(far.shape), _const_spec(near.shape),
                  _const_spec(nw.shape)],
        out_specs=(row(1024), _const_spec((nb, GLA_HEADS, GLA_DK, GLA_DV))),
        out_shape=(jax.ShapeDtypeStruct((t, 1024), BF16),
                   jax.ShapeDtypeStruct((nb, GLA_HEADS, GLA_DK, GLA_DV), F32)),
        scratch_shapes=[
            pltpu.VMEM((GLA_HEADS, pad + rows, LANES), F32),
            pltpu.VMEM((GLA_HEADS, pad + rows, LANES), F32),
            pltpu.VMEM((2 * GLA_HEADS, pad + rows, LANES), F32),
            pltpu.VMEM((nb, GLA_HEADS, GLA_DV, GLA_DK), F32),
            pltpu.VMEM((2 * GLA_HEADS, rows, LANES), F32),
            pltpu.VMEM((2 * GLA_HEADS, rows, LANES), F32),
        ],
        compiler_params=_params(),
        name="gla_prompt",
    )(q, k, g, v, r, far, near, nw)


def _gla_step_kernel(q_ref, k_ref, g_ref, v_ref, r_ref, s_ref, nw_ref, *rest, layer, first):
    og_ref, sout_ref = rest[-2:]
    sout_ref = _own_state_block(sout_ref, layer, first)
    nw = nw_ref[...]
    zeros = jnp.zeros((LANES - 8, LANES), F32)
    prow = lax.broadcasted_iota(jnp.int32, (16, 2 * GLA_DV), 0)
    plane = lax.broadcasted_iota(jnp.int32, (16, 2 * GLA_DV), 1)
    pick = [jnp.where(((prow == n) & (plane < GLA_DV)) | ((prow == 8 + n) & (plane >= GLA_DV)), 1.0, 0.0).astype(BF16)
            for n in range(8)]
    for h in range(GLA_HEADS):
        decay = jnp.concatenate([jnp.exp(g_ref[h]), zeros], axis=0).T
        kq = jnp.concatenate([k_ref[h], q_ref[h]], axis=0).astype(BF16)
        cols = [lax.dot_general(kq, pick[n], (((0,), (0,)), ((), ())), preferred_element_type=F32)
                for n in range(8)]
        outs = []
        for n in range(8):
            s_old = s_ref[n, h]
            vrow = jnp.concatenate([v_ref[2 * h, n:n + 1, :], v_ref[2 * h + 1, n:n + 1, :]], axis=1)
            s_new = s_old * decay[:, n:n + 1] + cols[n][:, :GLA_DV] * vrow
            sout_ref[n, h] = s_new
            outs.append(jnp.sum(cols[n][:, GLA_DV:] * s_new, axis=0, keepdims=True))
        o = jnp.concatenate(outs, axis=0)
        ms = jnp.mean(o * o, axis=-1, keepdims=True)
        c0 = h * GLA_DV
        og_ref[:, c0:c0 + GLA_DV] = o * lax.rsqrt(ms + EPS) * nw * _silu(r_ref[:, c0:c0 + GLA_DV])


def _stacked_state_args(states, layer, prev, n_inputs, out_index):
    tail = states.shape[2:]
    zeros = (0,) * len(tail)
    spec = pl.BlockSpec((None, 8) + tail, lambda i, _l=layer: (_l, i) + zeros)
    if prev is None:
        return spec, pl.BlockSpec((states.shape[0], 8) + tail, lambda i: (0, i) + zeros), [], [], {}
    return spec, spec, [pl.BlockSpec(memory_space=pl.ANY)], [prev], {n_inputs: out_index}


def _own_state_block(sout_ref, layer, first):
    if not first:
        return sout_ref
    for other in range(sout_ref.shape[0]):
        if other != layer:
            sout_ref[other] = jnp.zeros(sout_ref.shape[1:], F32)
    return sout_ref.at[layer]


def _gla_step(q, k, g, v, r, states, layer, prev, nw):
    n = r.shape[0]
    slab = lambda c: pl.BlockSpec((c, 8, LANES), lambda i: (0, i, 0))
    sspec, ospec, extra_specs, extra_args, aliases = _stacked_state_args(states, layer, prev, 7, 1)
    return pl.pallas_call(
        functools.partial(_gla_step_kernel, layer=layer, first=prev is None),
        grid=(n // 8,),
        in_specs=[slab(4), slab(4), slab(4), slab(8), pl.BlockSpec((8, 1024), lambda i: (i, 0)), sspec,
                  _const_spec(nw.shape)] + extra_specs,
        out_specs=(pl.BlockSpec((8, 1024), lambda i: (i, 0)), ospec),
        out_shape=(jax.ShapeDtypeStruct((n, 1024), F32), jax.ShapeDtypeStruct(states.shape, F32)),
        input_output_aliases=aliases,
        compiler_params=_params(),
        name="gla_step",
    )(q, k, g, v, r, states, nw, *extra_args)


def _s5_prep_kernel(lre_ref, lim_ref, ldt_ref, lre_r_ref, lim_r_ref, ldt_r_ref, bre_ref, bim_ref,
                    ar_ref, ai_ref, bbre_ref, bbim_ref):
    def disc(lre, lim, ldt):
        dt = jnp.exp(ldt)
        mag = jnp.exp(lre * dt)
        ar = mag * jnp.cos(lim * dt)
        ai = mag * jnp.sin(lim * dt)
        return ar, ai

    ar, ai = disc(lre_ref[...], lim_ref[...], ldt_ref[...])
    ar_ref[...] = ar
    ai_ref[...] = ai
    lre, lim = lre_r_ref[...], lim_r_ref[...]
    ar, ai = disc(lre, lim, ldt_r_ref[...])
    den = lre * lre + lim * lim
    wr = ((ar - 1.0) * lre + ai * lim) / den
    wi = (ai * lre - (ar - 1.0) * lim) / den
    bre, bim = bre_ref[...], bim_ref[...]
    bbre_ref[...] = wr * bre - wi * bim
    bbim_ref[...] = wr * bim + wi * bre


def _s5_prep(lam_re, lam_im, log_dt, b_re, b_im):
    n = lam_re.shape[0]
    rows = n * S5_GROUPS
    lre = lam_re.reshape(rows, S5_P)
    lim = lam_im.reshape(rows, S5_P)
    ldt = jnp.broadcast_to(log_dt.reshape(rows, 1), (rows, S5_P))
    rep = lambda a: jnp.repeat(a, S5_GROUP_CH, axis=1)
    args = (lre, lim, ldt, rep(lre), rep(lim), rep(ldt),
            b_re.reshape(rows, S5_P * S5_GROUP_CH), b_im.reshape(rows, S5_P * S5_GROUP_CH))
    wide = jax.ShapeDtypeStruct((rows, S5_P * S5_GROUP_CH), F32)
    narrow = jax.ShapeDtypeStruct((rows, S5_P), F32)
    ar, ai, bbre, bbim = pl.pallas_call(
        _s5_prep_kernel, out_shape=(narrow, narrow, wide, wide), name="s5_prep")(*args)
    shp = (n, S5_GROUPS, S5_P, S5_GROUP_CH)
    return ar.reshape(n, S5_GROUPS, S5_P), ai.reshape(n, S5_GROUPS, S5_P), bbre.reshape(shp), bbim.reshape(shp)


def _s5_kernel(u_ref, sg_ref, h0re_ref, h0im_ref, wbre_ref, wbim_ref, wc_ref, are_ref, aim_ref, d_ref,
               wglu_ref, bglu_ref, y_ref, hre_out, him_out, hre, him, cre, cim, ys, *, nb, tokens):
    step = pl.program_id(0)
    rows = nb * tokens

    @pl.when(step == 0)
    def _():
        cre[...] = h0re_ref[...]
        cim[...] = h0im_ref[...]

    for m in range(4):
        um = u_ref[:, m * LANES:(m + 1) * LANES].astype(BF16)
        hre[:, m * 512:(m + 1) * 512] = jnp.dot(um, wbre_ref[m], preferred_element_type=F32)
        him[:, m * 512:(m + 1) * 512] = jnp.dot(um, wbim_ref[m], preferred_element_type=F32)

    for qd in range(2):
        ql = slice(qd * 1024, (qd + 1) * 1024)
        ar = jnp.broadcast_to(are_ref[:, ql], (nb, 1024))
        ai = jnp.broadcast_to(aim_ref[:, ql], (nb, 1024))

        def tok(t, carry, ql=ql, ar=ar, ai=ai):
            hr, hi = carry
            sel = pl.ds(pl.multiple_of(t * nb, nb), nb)
            nr = ar * hr - ai * hi + hre[sel, ql]
            ni = ar * hi + ai * hr + him[sel, ql]
            hre[sel, ql] = nr
            him[sel, ql] = ni
            return nr, ni

        hr, hi = lax.fori_loop(0, tokens, tok, (cre[:, ql], cim[:, ql]))
        cre[:, ql] = hr
        cim[:, ql] = hi

    for m in range(4):
        hc = jnp.concatenate([hre[:, m * 512:(m + 1) * 512], him[:, m * 512:(m + 1) * 512]], axis=1)
        sl = slice(m * LANES, (m + 1) * LANES)
        y = _mm(hc, wc_ref[m]) + d_ref[:, sl] * u_ref[:, sl]
        ys[:, sl] = jax.nn.gelu(y)
    y = ys[...]
    gate = _sigmoid(_mm(y, wglu_ref[...]) + bglu_ref[...])
    y_ref[...] = (y * gate * _silu(sg_ref[...])).astype(BF16)

    @pl.when(step == pl.num_programs(0) - 1)
    def _():
        hre_out[...] = cre[...]
        him_out[...] = cim[...]


def _s5(u, sg, h0re, h0im, w, nb, tokens):
    t = u.shape[0]
    rows = nb * tokens
    row = lambda c: pl.BlockSpec((rows, c), lambda i: (i, 0))
    weights = (w["wbre"], w["wbim"], w["wc"], w["are"], w["aim"], w["d"], w["wglu"], w["bglu"])
    state = jax.ShapeDtypeStruct((nb, S5_STATE), F32)
    return pl.pallas_call(
        functools.partial(_s5_kernel, nb=nb, tokens=tokens),
        grid=(t // rows,),
        in_specs=[row(S5_WIDTH), row(S5_WIDTH), _const_spec(h0re.shape), _const_spec(h0im.shape)]
        + [_const_spec(a.shape) for a in weights],
        out_specs=(row(S5_WIDTH), _const_spec((nb, S5_STATE)), _const_spec((nb, S5_STATE))),
        out_shape=(jax.ShapeDtypeStruct((t, S5_WIDTH), BF16), state, state),
        scratch_shapes=[
            pltpu.VMEM((rows, S5_STATE), F32),
            pltpu.VMEM((rows, S5_STATE), F32),
            pltpu.VMEM((nb, S5_STATE), F32),
            pltpu.VMEM((nb, S5_STATE), F32),
            pltpu.VMEM((rows, S5_WIDTH), F32),
        ],
        compiler_params=_params(),
        name="s5",
    )(u, sg, h0re, h0im, *weights)


def _dn_solve(lms, rhss):
    nblk = CHUNK // SUB
    row = lax.broadcasted_iota(jnp.int32, (SUB, LANES), 0)
    lane = lax.broadcasted_iota(jnp.int32, (SUB, LANES), 1)
    seg = (lane // SUB) * SUB
    in_diag = [(lane >= SUB * blk) & (lane < SUB * (blk + 1)) for blk in range(nblk)]
    eye_pack = jnp.where((lane - seg == row) & (lane < CHUNK), 1.0, 0.0)
    zpad = jnp.zeros((CHUNK, LANES - CHUNK), F32)
    t_invs, lowers = [], []
    for lm in lms:
        wide = jnp.concatenate([lm, zpad], axis=1)
        tiles = [wide[SUB * blk:SUB * (blk + 1)] for blk in range(nblk)]
        dpack = jnp.zeros((SUB, LANES), F32)
        for blk in range(nblk):
            dpack = jnp.where(in_diag[blk], tiles[blk], dpack)
        inv = eye_pack
        for j in range(SUB - 1):
            col_j = jnp.take_along_axis(dpack, seg + j, axis=1)
            inv = inv - col_j * inv[j:j + 1, :]
        t_invs.append(jnp.concatenate(
            [jnp.where(in_diag[blk], inv, 0.0)[:, :CHUNK] for blk in range(nblk)], axis=0).astype(BF16))
        lowers.append(jnp.concatenate(
            [jnp.where(in_diag[blk], 0.0, tiles[blk])[:, :CHUNK] for blk in range(nblk)], axis=0).astype(BF16))
    d = lambda a, b: jnp.dot(a, b, preferred_element_type=F32)
    his = [rhs.astype(BF16) for rhs in rhss]
    los = [(rhs - hi.astype(F32)).astype(BF16) for rhs, hi in zip(rhss, his)]
    width = rhss[0].shape[1]
    applied = [d(t, jnp.concatenate([hi, lo_, lw], axis=1)) for t, hi, lo_, lw in zip(t_invs, his, los, lowers)]
    trs = [ap[:, :width] + ap[:, width:2 * width] for ap in applied]
    tns = [ap[:, 2 * width:] for ap in applied]
    xs = [[tr[0:SUB]] for tr in trs]
    for blk in range(1, nblk):
        lo = SUB * blk
        curs = [tr[lo:lo + SUB] - _mm(tn[lo:lo + SUB, 0:lo], jnp.concatenate(x, axis=0))
                for tr, tn, x in zip(trs, tns, xs)]
        for x, cur in zip(xs, curs):
            x.append(cur)
    return [jnp.concatenate(x, axis=0) for x in xs]


def _dn_prompt_kernel(x_ref, z_ref, beta_ref, gd_ref, cw_ref, nw_ref, c0_ref,
                      og_ref, sout_ref, cout_ref,
                      xs, qs, ks, vs, bcol, gcol, gc, gt, oscr, *, nb):
    rows = CHUNK * nb
    hist = (CONV_W - 1) * nb
    pad = PAD_TOKENS * nb
    step = pl.program_id(0)
    st = sout_ref

    @pl.when(step == 0)
    def _():
        st[...] = jnp.zeros_like(st)
        xs[0:hist, :] = c0_ref[...]
        gc[0, 0:pad, :] = jnp.zeros((pad, LANES), F32)

    xs[hist:hist + 64, :] = x_ref[0:64, :]

    def conv_rows(src, base, r0):
        tap = lambda i: base + r0 + i * nb if isinstance(r0, int) else pl.multiple_of(base + r0 + i * nb, 8)
        for c in range(DN_CONV_DIM // LANES):
            cl = slice(c * LANES, (c + 1) * LANES)
            y = src[pl.ds(tap(0), 64), cl] * cw_ref[0:1, cl]
            for i in range(1, CONV_W):
                y = y + src[pl.ds(tap(i), 64), cl] * cw_ref[i:i + 1, cl]
            y = _silu(y)
            if c < 2 * DN_HEADS:
                y = y * lax.rsqrt(jnp.sum(y * y, axis=-1, keepdims=True) + EPS)
            if c < DN_HEADS:
                qs[c, pl.ds(r0, 64), :] = y * (DN_DK ** -0.5)
            elif c < 2 * DN_HEADS:
                ks[c - DN_HEADS, pl.ds(r0, 64), :] = y
            else:
                vs[c - 2 * DN_HEADS, pl.ds(r0, 64), :] = y

    conv_rows(xs, 0, 0)

    def conv_tile(ti, carry):
        conv_rows(x_ref, -hist, pl.multiple_of(ti * 64, 64))
        return carry

    lax.fori_loop(1, rows // 64, conv_tile, 0)

    xs[0:hist, :] = x_ref[rows - hist:rows, :]

    gc[0, pad:pad + rows, :] = gd_ref[...]
    _cumsum_tokens(gc, 0, pad, rows, nb)

    def bc_tile(ti, carry):
        r0 = pl.multiple_of(ti * 64, 64)
        gtile = gc[0, pl.ds(pad + r0, 64), :]
        btile = beta_ref[pl.ds(r0, 64), :]
        for h in range(DN_HEADS):
            gcol[h, pl.ds(r0, 64), :] = jnp.broadcast_to(gtile[:, h:h + 1], (64, LANES))
            bcol[h, pl.ds(r0, 64), :] = jnp.broadcast_to(btile[:, h:h + 1], (64, LANES))
        return carry

    lax.fori_loop(0, rows // 64, bc_tile, 0)

    ii = lax.broadcasted_iota(jnp.int32, (CHUNK, CHUNK), 0)
    jj = lax.broadcasted_iota(jnp.int32, (CHUNK, CHUNK), 1)
    eye = (ii == jj).astype(BF16)

    parts = [_split3(gc[0, pl.ds(pad + b, CHUNK, stride=nb), :]) for b in range(nb)]
    tdot = lambda x: lax.dot_general(x, eye, (((0,), (0,)), ((), ())), preferred_element_type=F32)
    firsts = [tdot(p[0]) for p in parts]
    seconds = [tdot(p[1]) for p in parts]
    thirds = [tdot(p[2]) for p in parts]
    for b in range(nb):
        gt[b] = (firsts[b] + seconds[b]) + thirds[b]

    def per_batch(bi, carry):
        def head_group(gi, carry2):
            probs = [(bi * DN_BATCH + jb, gi * DN_GROUP + jh) for jb in range(DN_BATCH) for jh in range(DN_GROUP)]
            sels = [pl.ds(b, CHUNK, stride=nb) for b, _ in probs]
            loaded = [(qs[h, sel, :], ks[h, sel, :], vs[h, sel, :], bcol[h, sel, :], gcol[h, sel, :],
                       gt[b, pl.ds(h, 1), :], st[b, h]) for (b, h), sel in zip(probs, sels)]
            qbs, kbs, vbs, bcs, gcls, grows, s_olds = zip(*loaded)
            decs = [jnp.where(jj <= ii, jnp.exp(jnp.where(jj <= ii, gcl[:, :CHUNK] - grow, 0.0)), 0.0)
                    for gcl, grow in zip(gcls, grows)]
            kqs = [_mm_nt(jnp.concatenate([kb, qb], axis=0), kb) for qb, kb in zip(qbs, kbs)]
            kks = [kq[:CHUNK] for kq in kqs]
            qks = [kq[CHUNK:] for kq in kqs]
            lms = [jnp.where(jj < ii, bc[:, :CHUNK] * kk * dec, 0.0) for bc, kk, dec in zip(bcs, kks, decs)]
            egs = [jnp.exp(gcl) for gcl in gcls]
            rhss = [jnp.concatenate([vb * bc, kb * (bc * eg)], axis=1)
                    for vb, kb, bc, eg in zip(vbs, kbs, bcs, egs)]
            sols = _dn_solve(lms, rhss)
            v_news = [sol[:, :DN_DV] - _mm(sol[:, DN_DV:], s_old) for sol, s_old in zip(sols, s_olds)]
            os_ = [_mm(qb * eg, s_old) + _mm(qk * dec, v_new)
                   for qb, eg, s_old, qk, dec, v_new in zip(qbs, egs, s_olds, qks, decs, v_news)]
            glasts = [gcl[CHUNK - 1:CHUNK, :] for gcl in gcls]
            s_news = [s_old * jnp.exp(glast) + _mm_tn(kb * jnp.exp(glast - gcl), v_new)
                      for s_old, glast, kb, gcl, v_new in zip(s_olds, glasts, kbs, gcls, v_news)]
            for (b, h), sel, o, s_new in zip(probs, sels, os_, s_news):
                st[b, h] = s_new
                oscr[h, sel, :] = o
            return carry2

        lax.fori_loop(0, DN_HEADS // DN_GROUP, head_group, 0)
        return carry

    lax.fori_loop(0, nb // DN_BATCH, per_batch, 0)

    nw = nw_ref[...]

    def epi_tile(ti, carry):
        r0 = pl.multiple_of(ti * 64, 64)
        for h in range(DN_HEADS):
            o = oscr[h, pl.ds(r0, 64), :]
            inv = lax.rsqrt(jnp.mean(o * o, axis=-1, keepdims=True) + EPS)
            cl = slice(h * LANES, (h + 1) * LANES)
            og_ref[pl.ds(r0, 64), cl] = (o * inv * nw * _silu(z_ref[pl.ds(r0, 64), cl])).astype(BF16)
        return carry

    lax.fori_loop(0, rows // 64, epi_tile, 0)

    @pl.when(step == pl.num_programs(0) - 1)
    def _():
        cout_ref[...] = xs[0:hist, :]


def _dn_prompt(xqkv, z, beta, gd, cw, nw, c0, nb):
    t = z.shape[0]
    rows = CHUNK * nb
    hist = (CONV_W - 1) * nb
    row = lambda c: pl.BlockSpec((rows, c), lambda i: (i, 0))
    slab = pltpu.VMEM((DN_HEADS, rows, LANES), F32)
    return pl.pallas_call(
        functools.partial(_dn_prompt_kernel, nb=nb),
        grid=(t // rows,),
        in_specs=[row(DN_CONV_DIM), row(1024), row(LANES), row(LANES), _const_spec(cw.shape),
                  _const_spec(nw.shape), _const_spec(c0.shape)],
        out_specs=(row(1024), _const_spec((nb, DN_HEADS, DN_DK, DN_DV)), _const_spec((hist, DN_CONV_DIM))),
        out_shape=(jax.ShapeDtypeStruct((t, 1024), BF16),
                   jax.ShapeDtypeStruct((nb, DN_HEADS, DN_DK, DN_DV), F32),
                   jax.ShapeDtypeStruct((hist, DN_CONV_DIM), F32)),
        scratch_shapes=[
            pltpu.VMEM((hist + 64, DN_CONV_DIM), F32),
            slab, slab, slab, slab, slab,
            pltpu.VMEM((1, PAD_TOKENS * nb + rows, LANES), F32),
            pltpu.VMEM((nb, LANES, CHUNK), F32),
            slab,
        ],
        compiler_params=_params(),
        name="dn_prompt",
    )(xqkv, z, beta, gd, cw, nw, c0)


def _dn_step_kernel(x_ref, cb_ref, z_ref, beta_ref, gd_ref, s_ref, cw_ref, nw_ref, *rest, layer, first):
    og_ref, sout_ref, cn_ref = rest[-3:]
    sout_ref = _own_state_block(sout_ref, layer, first)
    nw = nw_ref[...]
    cn_ref[0] = cb_ref[1]
    cn_ref[1] = cb_ref[2]
    cn_ref[2] = x_ref[...]
    beta = beta_ref[...]
    eg_all = jnp.exp(gd_ref[...])

    def conv(c):
        cl = slice(c * LANES, (c + 1) * LANES)
        y = cb_ref[0, :, cl] * cw_ref[0:1, cl]
        y = y + cb_ref[1, :, cl] * cw_ref[1:2, cl]
        y = y + cb_ref[2, :, cl] * cw_ref[2:3, cl]
        y = y + x_ref[:, cl] * cw_ref[3:4, cl]
        return _silu(y)

    def l2(y):
        return y * lax.rsqrt(jnp.sum(y * y, axis=-1, keepdims=True) + EPS)

    prow = lax.broadcasted_iota(jnp.int32, (16, 2 * LANES), 0)
    plane = lax.broadcasted_iota(jnp.int32, (16, 2 * LANES), 1)
    pick = [jnp.where(((prow == n) & (plane < LANES)) | ((prow == 8 + n) & (plane >= LANES)), 1.0, 0.0).astype(BF16)
            for n in range(8)]

    for h in range(DN_HEADS):
        qh = l2(conv(h)) * (DN_DK ** -0.5)
        kh = l2(conv(DN_HEADS + h))
        vh = conv(2 * DN_HEADS + h)
        kq = jnp.concatenate([kh, qh], axis=0).astype(BF16)
        cols = [lax.dot_general(kq, pick[n], (((0,), (0,)), ((), ())), preferred_element_type=F32)
                for n in range(8)]
        outs = []
        for n in range(8):
            s_old = s_ref[n, h]
            kcol = cols[n][:, :LANES]
            eg = eg_all[n:n + 1, h:h + 1]
            ks_row = jnp.sum(kcol * s_old, axis=0, keepdims=True)
            v_new = beta[n:n + 1, h:h + 1] * (vh[n:n + 1, :] - eg * ks_row)
            s_new = s_old * eg + kcol * v_new
            sout_ref[n, h] = s_new
            outs.append(jnp.sum(cols[n][:, LANES:] * s_new, axis=0, keepdims=True))
        o = jnp.concatenate(outs, axis=0)
        cl = slice(h * LANES, (h + 1) * LANES)
        og_ref[:, cl] = o * lax.rsqrt(jnp.mean(o * o, axis=-1, keepdims=True) + EPS) * nw * _silu(z_ref[:, cl])


def _dn_step(xqkv, cbuf, z, beta, gd, states, layer, prev, cw, nw):
    n = z.shape[0]
    row = lambda c: pl.BlockSpec((8, c), lambda i: (i, 0))
    cspec = pl.BlockSpec((CONV_W - 1, 8, DN_CONV_DIM), lambda i: (0, i, 0))
    sspec, ospec, extra_specs, extra_args, aliases = _stacked_state_args(states, layer, prev, 8, 1)
    return pl.pallas_call(
        functools.partial(_dn_step_kernel, layer=layer, first=prev is None),
        grid=(n // 8,),
        in_specs=[row(DN_CONV_DIM), cspec, row(1024), row(LANES), row(LANES), sspec,
                  _const_spec(cw.shape), _const_spec(nw.shape)] + extra_specs,
        out_specs=(row(1024), ospec, cspec),
        out_shape=(jax.ShapeDtypeStruct((n, 1024), F32), jax.ShapeDtypeStruct(states.shape, F32),
                   jax.ShapeDtypeStruct(cbuf.shape, F32)),
        input_output_aliases=aliases,
        compiler_params=_params(),
        name="dn_step",
    )(xqkv, cbuf, z, beta, gd, states, cw, nw, *extra_args)


def _pad_cols(a, n):
    return jnp.pad(a, ((0, 0), (0, n - a.shape[1])))


def _even_weights(w_in, w_gate_up, b_gate, w_out):
    return {
        "w": _pad_cols(w_in.astype(BF16), W_IN_PAD),
        "wg": jnp.pad(w_gate_up, ((0, LANES - GLA_RANK), (0, 0))).astype(BF16),
        "bg": b_gate.reshape(1, -1),
        "wo": w_out.astype(BF16),
    }


def _s5_weights(ar, ai, bbre, bbim, c_re, c_im, d, w_glu, b_glu):
    eye = jnp.eye(8, dtype=F32)

    def pack_b(bb):
        t = bb.reshape(4, 8, S5_P, S5_GROUP_CH).transpose(0, 1, 3, 2)
        return (t[:, :, :, None, :] * eye[None, :, None, :, None]).reshape(4, 128, 512).astype(BF16)

    def pack_c(c):
        t = c.reshape(4, 8, S5_GROUP_CH, S5_P).transpose(0, 1, 3, 2)
        return (t[:, :, :, None, :] * eye[None, :, None, :, None]).reshape(4, 512, 128)

    return {
        "wbre": pack_b(bbre), "wbim": pack_b(bbim),
        "wc": jnp.concatenate([pack_c(c_re), -pack_c(c_im)], axis=1).astype(BF16),
        "are": ar.reshape(1, S5_STATE), "aim": ai.reshape(1, S5_STATE),
        "d": d.reshape(1, -1), "wglu": w_glu.astype(BF16), "bglu": b_glu.reshape(1, -1),
    }


def _odd_weights(w_in, conv_w, a_log, dt_bias, w_out):
    return {
        "w": _pad_cols(w_in.astype(BF16), W_IN_PAD),
        "alog": _pad_cols(a_log.reshape(1, -1), LANES), "dtb": _pad_cols(dt_bias.reshape(1, -1), LANES),
        "cw": jnp.pad(conv_w, ((0, 8 - CONV_W), (0, 0))),
        "wo": w_out.astype(BF16),
    }


def kernel(x_prompt, x_sample, state_gla, state_s5_re, state_s5_im, state_delta, state_conv, norm_w, final_norm_w, w_in_even, gla_w_gate_up, gla_b_gate, gla_norm_w, s5_lambda_re, s5_lambda_im, s5_log_dt, s5_b_re, s5_b_im, s5_c_re, s5_c_im, s5_d, s5_w_glu, s5_b_glu, w_out_even, w_in_odd, dn_conv_w, dn_a_log, dn_dt_bias, dn_norm_w, w_out_odd):
    nb, seq, _ = x_prompt.shape
    ns = x_sample.shape[0]
    assert seq % CHUNK == 0 and nb % 8 == 0 and ns % 8 == 0 and x_sample.shape[1] == 1
    depth = norm_w.shape[0]

    assert depth % 2 == 0 and state_gla.shape[0] == depth // 2 and state_delta.shape[0] == depth // 2
    xp = None
    xs = x_sample.reshape(ns, D_MODEL)
    fw = final_norm_w.reshape(1, -1)

    ar, ai, bbre, bbim = _s5_prep(s5_lambda_re, s5_lambda_im, s5_log_dt, s5_b_re, s5_b_im)

    gla_p, s5r_p, s5i_p, s5r_s, s5i_s, dn_p, cv_p, cv_s = ([] for _ in range(8))
    gla_s = dn_s = None
    for layer in range(depth):
        i = layer // 2
        nw = norm_w[layer].reshape(1, -1)
        if layer % 2 == 0:
            w = _even_weights(w_in_even[i], gla_w_gate_up[i], gla_b_gate[i], w_out_even[i])
            w5 = _s5_weights(ar[i], ai[i], bbre[i], bbim[i], s5_c_re[i], s5_c_im[i], s5_d[i],
                             s5_w_glu[i], s5_b_glu[i])
            gnw = gla_norm_w[i].reshape(1, -1)
            if layer == 0:
                q, k, g, v, r, u, sg, xp = _proj_even(x_prompt, nw, w, batch_major=True)
            else:
                q, k, g, v, r, u, sg, xp = _proj_even(xp, nw, w, pending=pending)
            og, s_new = _gla_prompt(q, k, g, v, r, gnw, nb)
            zero = jnp.zeros((nb, S5_STATE), F32)
            y5, hre, him = _s5(u, sg, zero, zero, w5, nb, CHUNK)
            pending = (og, y5, w["wo"])
            gla_p.append(s_new)
            s5r_p.append(hre.reshape(nb, S5_GROUPS, S5_P))
            s5i_p.append(him.reshape(nb, S5_GROUPS, S5_P))
            q, k, g, v, r, u, sg = _proj_even(xs, nw, w)
            og, gla_s = _gla_step(q, k, g, v, r, state_gla, i, gla_s, gnw)
            y5, hre, him = _s5(u, sg, state_s5_re[i].reshape(ns, S5_STATE), state_s5_im[i].reshape(ns, S5_STATE),
                               w5, ns, 1)
            xs = _out_even(xs, og, y5, w["wo"])
            s5r_s.append(hre.reshape(ns, S5_GROUPS, S5_P))
            s5i_s.append(him.reshape(ns, S5_GROUPS, S5_P))
        else:
            w = _odd_weights(w_in_odd[i], dn_conv_w[i], dn_a_log[i], dn_dt_bias[i], w_out_odd[i])
            dnw = dn_norm_w[i].reshape(1, -1)
            final = layer == depth - 1
            xqkv, z, beta, gd, xp = _proj_odd(xp, nw, w, pending=pending)
            c0 = jnp.zeros(((CONV_W - 1) * nb, DN_CONV_DIM), F32)
            og, s_new, c_new = _dn_prompt(xqkv, z, beta, gd, w["cw"], dnw, c0, nb)
            if final:
                xp = _out_odd(xp, og, w["wo"], fw, final, nb_out=nb)
            else:
                pending = (og, w["wo"])
            dn_p.append(s_new)
            cv_p.append(c_new.reshape(CONV_W - 1, nb, DN_CONV_DIM).transpose(1, 0, 2))
            xqkv, z, beta, gd = _proj_odd(xs, nw, w)
            og, dn_s, c_new = _dn_step(xqkv, state_conv[i].transpose(1, 0, 2), z, beta, gd, state_delta, i, dn_s,
                                       w["cw"], dnw)
            xs = _out_odd(xs, og, w["wo"], fw, final)
            cv_s.append(c_new.transpose(1, 0, 2))

    y_prompt = xp
    y_sample = xs.reshape(ns, 1, D_MODEL)
    st = jnp.stack
    return (y_prompt, y_sample, st(gla_p), gla_s, st(s5r_p), st(s5i_p), st(s5r_s), st(s5i_s),
            st(dn_p), dn_s, st(cv_p), st(cv_s))
```

```python
import functools
import math

import jax
import jax.numpy as jnp
from jax import lax
from jax.experimental import pallas as pl
from jax.experimental.pallas import tpu as pltpu

F32 = jnp.float32
BF16 = jnp.bfloat16
EPS = 1e-6

D_MODEL = 1024
GLA_HEADS, GLA_DK, GLA_DV, GLA_RANK = 4, 128, 256, 16
GLA_GATE_NORM = 16.0
S5_GROUPS, S5_GROUP_CH, S5_P, S5_WIDTH = 32, 16, 64, 512
S5_STATE = S5_GROUPS * S5_P
DN_HEADS, DN_DK, DN_DV = 8, 128, 128
DN_CONV_DIM, CONV_W = 3072, 4
CHUNK = 64
SUB = 8
GLA_BAND = 8
GLA_BATCH = 4
GLA_SAFE_SPAN = 60.0
PAD_TOKENS = 16
DN_BATCH = 8
DN_GROUP = 8
LANES = 128
W_IN_PAD = 4224
VMEM_LIMIT = 56 * 1024 * 1024


def _mm(a, b):
    return jnp.dot(a.astype(BF16), b.astype(BF16), preferred_element_type=F32)


def _mm_nt(a, b):
    return lax.dot_general(a.astype(BF16), b.astype(BF16), (((1,), (1,)), ((), ())),
                           preferred_element_type=F32)


def _mm_tn(a, b):
    return lax.dot_general(a.astype(BF16), b.astype(BF16), (((0,), (0,)), ((), ())),
                           preferred_element_type=F32)


def _split3(a):
    a1 = a.astype(BF16)
    r1 = a - a1.astype(F32)
    a2 = r1.astype(BF16)
    a3 = (r1 - a2.astype(F32)).astype(BF16)
    return a1, a2, a3


def _mm_tn_exact(a, m_bf16):
    a1, a2, a3 = _split3(a)
    d = lambda x: lax.dot_general(x, m_bf16, (((0,), (0,)), ((), ())), preferred_element_type=F32)
    return (d(a1) + d(a2)) + d(a3)


def _sigmoid(x):
    return 1.0 / (1.0 + jnp.exp(-x))


def _silu(x):
    return x * _sigmoid(x)


def _softplus(x):
    return jnp.maximum(x, 0.0) + jnp.log1p(jnp.exp(-jnp.abs(x)))


def _rms_rows(x, w):
    ms = jnp.mean(x * x, axis=-1, keepdims=True)
    return x * lax.rsqrt(ms + EPS) * w


def _const_spec(shape):
    nd = len(shape)
    return pl.BlockSpec(shape, lambda i, _nd=nd: (0,) * _nd)


def _params(**flags):
    return pltpu.CompilerParams(dimension_semantics=("arbitrary",), vmem_limit_bytes=VMEM_LIMIT,
                                flags=flags or None)


def _row_tile(t):
    return min(512, t)


def _add_pending(x, pending):
    if len(pending) == 2:
        og_ref, wo_ref = pending
        return x + _mm(og_ref[...], wo_ref[...])
    og_ref, y5_ref, wo_ref = pending
    return x + (_mm(og_ref[...], wo_ref[0:1024, :]) + _mm(y5_ref[...], wo_ref[1024:, :]))


def _proj_even_kernel(x_ref, nw_ref, w_ref, wg_ref, bg_ref, *rest, nb_in, n_pending):
    pending = rest[:n_pending]
    q_ref, k_ref, g_ref, v_ref, r_ref, u_ref, sg_ref = rest[n_pending:n_pending + 7]
    tail = rest[n_pending + 7:]
    if nb_in:
        xo_ref, xt = tail
        for b in range(nb_in):
            for c in range(D_MODEL // LANES):
                xt[c, pl.ds(b, x_ref.shape[1], stride=nb_in), :] = x_ref[b, :, c * LANES:(c + 1) * LANES]
        x = jnp.concatenate([xt[c] for c in range(D_MODEL // LANES)], axis=1)
        xo_ref[...] = x
    elif n_pending:
        xo_ref, = tail
        x = _add_pending(x_ref[...], pending)
        xo_ref[...] = x
    else:
        x = x_ref[...]
    hb = _rms_rows(x, nw_ref[...]).astype(BF16)
    d = lambda lo, hi: jnp.dot(hb, w_ref[:, lo:hi], preferred_element_type=F32)
    q = d(0, 512) * (GLA_DK ** -0.5)
    k = d(512, 1024)
    lr_u = d(3072, 3712)
    u_sg = d(3584, 4224)
    logit = jnp.dot(lr_u[:, :LANES].astype(BF16), wg_ref[...], preferred_element_type=F32) + bg_ref[...]
    g = -_softplus(-logit) / GLA_GATE_NORM
    for h in range(GLA_HEADS):
        sl = slice(h * LANES, (h + 1) * LANES)
        q_ref[h] = q[:, sl]
        k_ref[h] = k[:, sl]
        g_ref[h] = g[:, sl]
    v = d(1024, 2048)
    for s in range(2 * GLA_HEADS):
        v_ref[s] = v[:, s * LANES:(s + 1) * LANES]
    r_ref[...] = d(2048, 3072)
    u_ref[...] = lr_u[:, GLA_RANK:GLA_RANK + S5_WIDTH]
    sg_ref[...] = u_sg[:, GLA_RANK:GLA_RANK + S5_WIDTH]


def _pending_specs(pending, tm):
    *acts, wo = pending
    return [pl.BlockSpec((tm, a.shape[1]), lambda i: (i, 0)) for a in acts] + [_const_spec(wo.shape)]


def _proj_even(x, nw, w, batch_major=False, pending=()):
    nb_in = x.shape[0] if batch_major else 0
    t = x.shape[0] * x.shape[1] if batch_major else x.shape[0]
    tm = _row_tile(t)
    row = lambda c: pl.BlockSpec((tm, c), lambda i: (i, 0))
    slab = lambda n: pl.BlockSpec((n, tm, LANES), lambda i: (0, i, 0))
    x_spec = pl.BlockSpec((nb_in, tm // nb_in, D_MODEL), lambda i: (0, i, 0)) if batch_major else row(D_MODEL)
    new_x = batch_major or bool(pending)
    extra_out_shape = (jax.ShapeDtypeStruct((t, D_MODEL), F32),) if new_x else ()
    extra_out_specs = (row(D_MODEL),) if new_x else ()
    scratch = [pltpu.VMEM((D_MODEL // LANES, tm, LANES), F32)] if batch_major else []
    pend_specs = _pending_specs(pending, tm) if pending else []
    out_shape = (
        jax.ShapeDtypeStruct((GLA_HEADS, t, LANES), F32),
        jax.ShapeDtypeStruct((GLA_HEADS, t, LANES), F32),
        jax.ShapeDtypeStruct((GLA_HEADS, t, LANES), F32),
        jax.ShapeDtypeStruct((2 * GLA_HEADS, t, LANES), F32),
        jax.ShapeDtypeStruct((t, 1024), F32),
        jax.ShapeDtypeStruct((t, S5_WIDTH), F32),
        jax.ShapeDtypeStruct((t, S5_WIDTH), F32),
    )
    weights = (w["w"], w["wg"], w["bg"])
    return pl.pallas_call(
        functools.partial(_proj_even_kernel, nb_in=nb_in, n_pending=len(pending)),
        grid=(t // tm,),
        in_specs=[x_spec, _const_spec(nw.shape)] + [_const_spec(a.shape) for a in weights] + pend_specs,
        out_specs=(slab(4), slab(4), slab(4), slab(8), row(1024), row(S5_WIDTH), row(S5_WIDTH)) + extra_out_specs,
        out_shape=out_shape + extra_out_shape,
        scratch_shapes=scratch,
        compiler_params=_params(),
        name="proj_even",
    )(x, nw, *weights, *pending)


def _proj_odd_kernel(x_ref, nw_ref, w_ref, alog_ref, dtb_ref, *rest, n_pending):
    pending = rest[:n_pending]
    xqkv_ref, z_ref, beta_ref, gd_ref = rest[n_pending:n_pending + 4]
    x = x_ref[...]
    if n_pending:
        xo_ref, = rest[n_pending + 4:]
        x = _add_pending(x, pending)
        xo_ref[...] = x
    hb = _rms_rows(x, nw_ref[...]).astype(BF16)
    d = lambda lo, hi: jnp.dot(hb, w_ref[:, lo:hi], preferred_element_type=F32)
    xqkv_ref[...] = d(0, DN_CONV_DIM)
    z_ref[...] = d(DN_CONV_DIM, DN_CONV_DIM + 1024)
    tail = d(DN_CONV_DIM + 1024, DN_CONV_DIM + 1024 + LANES)
    beta_ref[...] = _sigmoid(tail)
    a = pltpu.roll(tail, LANES - DN_HEADS, 1)
    gd_ref[...] = -jnp.exp(alog_ref[...]) * _softplus(a + dtb_ref[...])


def _proj_odd(x, nw, w, pending=()):
    t = x.shape[0]
    tm = _row_tile(t)
    row = lambda c: pl.BlockSpec((tm, c), lambda i: (i, 0))
    weights = (w["w"], w["alog"], w["dtb"])
    pend_specs = _pending_specs(pending, tm) if pending else []
    extra_out_shape = (jax.ShapeDtypeStruct((t, D_MODEL), F32),) if pending else ()
    extra_out_specs = (row(D_MODEL),) if pending else ()
    return pl.pallas_call(
        functools.partial(_proj_odd_kernel, n_pending=len(pending)),
        grid=(t // tm,),
        in_specs=[row(D_MODEL), _const_spec(nw.shape)] + [_const_spec(a.shape) for a in weights] + pend_specs,
        out_specs=(row(DN_CONV_DIM), row(1024), row(LANES), row(LANES)) + extra_out_specs,
        out_shape=(
            jax.ShapeDtypeStruct((t, DN_CONV_DIM), F32),
            jax.ShapeDtypeStruct((t, 1024), F32),
            jax.ShapeDtypeStruct((t, LANES), F32),
            jax.ShapeDtypeStruct((t, LANES), F32),
        ) + extra_out_shape,
        compiler_params=_params(),
        name="proj_odd",
    )(x, nw, *weights, *pending)


def _out_even_kernel(x_ref, og_ref, y5_ref, w_ref, o_ref):
    o_ref[...] = x_ref[...] + (_mm(og_ref[...], w_ref[0:1024, :]) + _mm(y5_ref[...], w_ref[1024:, :]))


def _out_even(x, og, y5, w):
    t = x.shape[0]
    tm = _row_tile(t)
    row = lambda c: pl.BlockSpec((tm, c), lambda i: (i, 0))
    return pl.pallas_call(
        _out_even_kernel,
        grid=(t // tm,),
        in_specs=[row(D_MODEL), row(1024), row(S5_WIDTH), _const_spec(w.shape)],
        out_specs=row(D_MODEL),
        out_shape=jax.ShapeDtypeStruct((t, D_MODEL), F32),
        compiler_params=_params(),
        name="out_even",
    )(x, og, y5, w)


def _out_odd_kernel(x_ref, og_ref, w_ref, fw_ref, o_ref, *slabs, final, nb_out):
    y = x_ref[...] + _mm(og_ref[...], w_ref[...])
    if final:
        y = _rms_rows(y, fw_ref[...])
    if nb_out:
        ys, = slabs
        for c in range(D_MODEL // LANES):
            ys[c] = y[:, c * LANES:(c + 1) * LANES]
        for b in range(nb_out):
            for c in range(D_MODEL // LANES):
                o_ref[b, :, c * LANES:(c + 1) * LANES] = ys[c, pl.ds(b, o_ref.shape[1], stride=nb_out), :]
    else:
        o_ref[...] = y


def _out_odd(x, og, w, fw, final, nb_out=0):
    t = x.shape[0]
    tm = _row_tile(t)
    row = lambda c: pl.BlockSpec((tm, c), lambda i: (i, 0))
    if nb_out:
        out_spec = pl.BlockSpec((nb_out, tm // nb_out, D_MODEL), lambda i: (0, i, 0))
        out_shape = jax.ShapeDtypeStruct((nb_out, t // nb_out, D_MODEL), F32)
        scratch = [pltpu.VMEM((D_MODEL // LANES, tm, LANES), F32)]
    else:
        out_spec, out_shape, scratch = row(D_MODEL), jax.ShapeDtypeStruct((t, D_MODEL), F32), []
    return pl.pallas_call(
        functools.partial(_out_odd_kernel, final=final, nb_out=nb_out),
        grid=(t // tm,),
        in_specs=[row(D_MODEL), row(1024), _const_spec(w.shape), _const_spec(fw.shape)],
        out_specs=out_spec,
        out_shape=out_shape,
        scratch_shapes=scratch,
        compiler_params=_params(),
        name="out_odd",
    )(x, og, w, fw)


def _cumsum_tokens(ref, lead, pad, rows, nb):
    shift = nb
    while shift <= pad:
        ref[lead, pad:pad + rows, :] = ref[lead, pad:pad + rows, :] + ref[lead, pad - shift:pad + rows - shift, :]
        shift *= 2
    while shift < rows:
        ref[lead, pad + shift:pad + rows, :] = (ref[lead, pad + shift:pad + rows, :]
                                                + ref[lead, pad:pad + rows - shift, :])
        shift *= 2


def _gla_prompt_kernel(q_ref, k_ref, g_ref, v_ref, r_ref, far_ref, near_ref, nw_ref, og_ref, sout_ref,
                       kp, bp, vp, st, oscr, ocar, *, nb):
    rows = CHUNK * nb
    pad = PAD_TOKENS * nb
    blk_rows = GLA_BAND * nb
    step = pl.program_id(0)

    @pl.when(step == 0)
    def _():
        st[...] = jnp.zeros_like(st)
        kp[:, 0:pad, :] = jnp.zeros((GLA_HEADS, pad, LANES), F32)
        bp[:, 0:pad, :] = jnp.zeros((GLA_HEADS, pad, LANES), F32)
        vp[:, 0:pad, :] = jnp.zeros((2 * GLA_HEADS, pad, LANES), F32)

    for h in range(GLA_HEADS):
        bp[h, pad:pad + rows, :] = g_ref[h]
        kp[h, pad:pad + rows, :] = k_ref[h]
        _cumsum_tokens(bp, h, pad, rows, nb)
    for s in range(2 * GLA_HEADS):
        vp[s, pad:pad + rows, :] = v_ref[s]

    nblk = CHUNK // GLA_BAND

    span = jnp.zeros((nb, LANES), F32)
    for h in range(GLA_HEADS):
        for blk in range(nblk):
            lo = pad + blk * blk_rows
            span = jnp.maximum(span, bp[h, lo - nb:lo, :] - bp[h, lo + blk_rows - nb:lo + blk_rows, :])
    safe = jnp.max(span) <= GLA_SAFE_SPAN

    def pairs_on_mxu(first_blk, mask_ref, accumulate):
        for h in range(GLA_HEADS):
            probs = []
            for blk in range(first_blk, nblk):
                lo = blk * blk_rows
                hi = lo + (1 - first_blk) * blk_rows
                ref_b = bp[h, pad + lo - nb:pad + lo, :]
                qs = q_ref[h, lo:lo + blk_rows, :] * jnp.exp(
                    bp[h, pad + lo:pad + lo + blk_rows, :] - jnp.concatenate([ref_b] * GLA_BAND, axis=0))
                ks = kp[h, pad:pad + hi, :] * jnp.exp(
                    jnp.concatenate([ref_b] * (hi // nb), axis=0) - bp[h, pad:pad + hi, :])
                probs.append((hi, _mm_nt(qs, ks) * mask_ref[lo:lo + blk_rows, 0:hi]))
            for blk in range(first_blk, nblk):
                lo = blk * blk_rows
                hi, p = probs[blk - first_blk]
                p = p.astype(BF16)
                for half in range(2):
                    sl = 2 * h + half
                    pv = jnp.dot(p, vp[sl, pad:pad + hi, :].astype(BF16), preferred_element_type=F32)
                    oscr[sl, lo:lo + blk_rows, :] = oscr[sl, lo:lo + blk_rows, :] + pv if accumulate else pv

    @pl.when(safe)
    def _():
        pairs_on_mxu(0, near_ref, False)

    def band_tile(ti, carry):
        r0 = pl.multiple_of(ti * 64, 64)
        for h in range(GLA_HEADS):
            qt = q_ref[h, pl.ds(r0, 64), :]
            bt = bp[h, pl.ds(pad + r0, 64), :]
            acc0 = jnp.zeros((64, LANES), F32)
            acc1 = jnp.zeros((64, LANES), F32)
            for d in range(GLA_BAND):
                off = pl.multiple_of(pad + r0 - d * nb, 8)
                ks = kp[h, pl.ds(off, 64), :]
                bs = bp[h, pl.ds(off, 64), :]
                w = jnp.sum(qt * ks * jnp.exp(bt - bs), axis=-1, keepdims=True)
                acc0 = acc0 + w * vp[2 * h, pl.ds(off, 64), :]
                acc1 = acc1 + w * vp[2 * h + 1, pl.ds(off, 64), :]
            oscr[2 * h, pl.ds(r0, 64), :] = acc0
            oscr[2 * h + 1, pl.ds(r0, 64), :] = acc1
        return carry

    @pl.when(jnp.logical_not(safe))
    def _():
        lax.fori_loop(0, rows // 64, band_tile, 0)
        pairs_on_mxu(1, far_ref, True)

    def per_batch(bi, carry):
        probs = [(bi * GLA_BATCH + jb, h) for jb in range(GLA_BATCH) for h in range(GLA_HEADS)]
        sels = [pl.ds(b, CHUNK, stride=nb) for b, _ in probs]
        loaded = [(q_ref[h, sel, :], kp[h, pl.ds(pad + b, CHUNK, stride=nb), :],
                   bp[h, pl.ds(pad + b, CHUNK, stride=nb), :], vp[2 * h, pl.ds(pad + b, CHUNK, stride=nb), :],
                   vp[2 * h + 1, pl.ds(pad + b, CHUNK, stride=nb), :], st[b, h])
                  for (b, h), sel in zip(probs, sels)]
        outs = [_mm_nt(qb * jnp.exp(bb), stt) for qb, _, bb, _, _, stt in loaded]
        news = []
        for _, kb, bb, v0, v1, stt in loaded:
            blast = bb[CHUNK - 1:CHUNK, :]
            kd = kb * jnp.exp(blast - bb)
            news.append(stt * jnp.exp(blast) + _mm_tn(jnp.concatenate([v0, v1], axis=1), kd))
        for (b, h), sel, out, new in zip(probs, sels, outs, news):
            ocar[2 * h, sel, :] = out[:, :LANES]
            ocar[2 * h + 1, sel, :] = out[:, LANES:]
            st[b, h] = new
        return carry

    lax.fori_loop(0, nb // GLA_BATCH, per_batch, 0)

    nw = nw_ref[...]

    def epi_tile(ti, carry):
        r0 = pl.multiple_of(ti * 64, 64)
        for h in range(GLA_HEADS):
            o0 = oscr[2 * h, pl.ds(r0, 64), :] + ocar[2 * h, pl.ds(r0, 64), :]
            o1 = oscr[2 * h + 1, pl.ds(r0, 64), :] + ocar[2 * h + 1, pl.ds(r0, 64), :]
            ms = (jnp.sum(o0 * o0, axis=-1, keepdims=True) + jnp.sum(o1 * o1, axis=-1, keepdims=True)) / GLA_DV
            inv = lax.rsqrt(ms + EPS)
            c0 = h * GLA_DV
            og_ref[pl.ds(r0, 64), c0:c0 + LANES] = (
                o0 * inv * nw[:, :LANES] * _silu(r_ref[pl.ds(r0, 64), c0:c0 + LANES])).astype(BF16)
            og_ref[pl.ds(r0, 64), c0 + LANES:c0 + 2 * LANES] = (
                o1 * inv * nw[:, LANES:] * _silu(r_ref[pl.ds(r0, 64), c0 + LANES:c0 + 2 * LANES])).astype(BF16)
        return carry

    lax.fori_loop(0, rows // 64, epi_tile, 0)

    @pl.when(step == pl.num_programs(0) - 1)
    def _():
        def wr(b, carry):
            for h in range(GLA_HEADS):
                sout_ref[b, h] = st[b, h].T
            return carry
        lax.fori_loop(0, nb, wr, 0)


def _pair_mask(nb, min_dist):
    n = CHUNK * nb
    r = jnp.arange(n)
    same = (r[:, None] % nb) == (r[None, :] % nb)
    far = (r[None, :] // nb) <= (r[:, None] // nb) - min_dist
    return (same & far).astype(F32)


def _gla_prompt(q, k, g, v, r, nw, nb):
    t = r.shape[0]
    rows = CHUNK * nb
    pad = PAD_TOKENS * nb
    slab = lambda n: pl.BlockSpec((n, rows, LANES), lambda i: (0, i, 0))
    row = lambda c: pl.BlockSpec((rows, c), lambda i: (i, 0))
    far = _pair_mask(nb, GLA_BAND)
    near = _pair_mask(nb, 0)
    return pl.pallas_call(
        functools.partial(_gla_prompt_kernel, nb=nb),
        grid=(t // rows,),
        in_specs=[slab(4), slab(4), slab(4), slab(8), row(1024), _const_spec(far.shape), _const_spec(near.shape),
                  _const_spec(nw.shape)],
        out_specs=(row(1024), _const_spec((nb, GLA_HEADS, GLA_DK, GLA_DV))),
        out_shape=(jax.ShapeDtypeStruct((t, 1024), BF16),
                   jax.ShapeDtypeStruct((nb, GLA_HEADS, GLA_DK, GLA_DV), F32)),
        scratch_shapes=[
            pltpu.VMEM((GLA_HEADS, pad + rows, LANES), F32),
            pltpu.VMEM((GLA_HEADS, pad + rows, LANES), F32),
            pltpu.VMEM((2 * GLA_HEADS, pad + rows, LANES), F32),
            pltpu.VMEM((nb, GLA_HEADS, GLA_DV, GLA_DK), F32),
            pltpu.VMEM((2 * GLA_HEADS, rows, LANES), F32),
            pltpu.VMEM((2 * GLA_HEADS, rows, LANES), F32),
        ],
        compiler_params=_params(),
        name="gla_prompt",
    )(q, k, g, v, r, far, near, nw)


def _gla_step_kernel(q_ref, k_ref, g_ref, v_ref, r_ref, s_ref, nw_ref, *rest, layer, first):
    og_ref, sout_ref = rest[-2:]
    sout_ref = _own_state_block(sout_ref, layer, first)
    nw = nw_ref[...]
    zeros = jnp.zeros((LANES - 8, LANES), F32)
    prow = lax.broadcasted_iota(jnp.int32, (16, 2 * GLA_DV), 0)
    plane = lax.broadcasted_iota(jnp.int32, (16, 2 * GLA_DV), 1)
    pick = [jnp.where(((prow == n) & (plane < GLA_DV)) | ((prow == 8 + n) & (plane >= GLA_DV)), 1.0, 0.0).astype(BF16)
            for n in range(8)]
    for h in range(GLA_HEADS):
        decay = jnp.concatenate([jnp.exp(g_ref[h]), zeros], axis=0).T
        kq = jnp.concatenate([k_ref[h], q_ref[h]], axis=0).astype(BF16)
        cols = [lax.dot_general(kq, pick[n], (((0,), (0,)), ((), ())), preferred_element_type=F32)
                for n in range(8)]
        outs = []
        for n in range(8):
            s_old = s_ref[n, h]
            vrow = jnp.concatenate([v_ref[2 * h, n:n + 1, :], v_ref[2 * h + 1, n:n + 1, :]], axis=1)
            s_new = s_old * decay[:, n:n + 1] + cols[n][:, :GLA_DV] * vrow
            sout_ref[n, h] = s_new
            outs.append(jnp.sum(cols[n][:, GLA_DV:] * s_new, axis=0, keepdims=True))
        o = jnp.concatenate(outs, axis=0)
        ms = jnp.mean(o * o, axis=-1, keepdims=True)
        c0 = h * GLA_DV
        og_ref[:, c0:c0 + GLA_DV] = o * lax.rsqrt(ms + EPS) * nw * _silu(r_ref[:, c0:c0 + GLA_DV])


def _stacked_state_args(states, layer, prev, n_inputs, out_index):
    tail = states.shape[2:]
    zeros = (0,) * len(tail)
    spec = pl.BlockSpec((None, 8) + tail, lambda i, _l=layer: (_l, i) + zeros)
    if prev is None:
        return spec, pl.BlockSpec((states.shape[0], 8) + tail, lambda i: (0, i) + zeros), [], [], {}
    return spec, spec, [pl.BlockSpec(memory_space=pl.ANY)], [prev], {n_inputs: out_index}


def _own_state_block(sout_ref, layer, first):
    if not first:
        return sout_ref
    for other in range(sout_ref.shape[0]):
        if other != layer:
            sout_ref[other] = jnp.zeros(sout_ref.shape[1:], F32)
    return sout_ref.at[layer]


def _gla_step(q, k, g, v, r, states, layer, prev, nw):
    n = r.shape[0]
    slab = lambda c: pl.BlockSpec((c, 8, LANES), lambda i: (0, i, 0))
    sspec, ospec, extra_specs, extra_args, aliases = _stacked_state_args(states, layer, prev, 7, 1)
    return pl.pallas_call(
        functools.partial(_gla_step_kernel, layer=layer, first=prev is None),
        grid=(n // 8,),
        in_specs=[slab(4), slab(4), slab(4), slab(8), pl.BlockSpec((8, 1024), lambda i: (i, 0)), sspec,
                  _const_spec(nw.shape)] + extra_specs,
        out_specs=(pl.BlockSpec((8, 1024), lambda i: (i, 0)), ospec),
        out_shape=(jax.ShapeDtypeStruct((n, 1024), F32), jax.ShapeDtypeStruct(states.shape, F32)),
        input_output_aliases=aliases,
        compiler_params=_params(),
        name="gla_step",
    )(q, k, g, v, r, states, nw, *extra_args)


def _s5_prep_kernel(lre_ref, lim_ref, ldt_ref, lre_r_ref, lim_r_ref, ldt_r_ref, bre_ref, bim_ref,
                    ar_ref, ai_ref, bbre_ref, bbim_ref):
    def disc(lre, lim, ldt):
        dt = jnp.exp(ldt)
        mag = jnp.exp(lre * dt)
        ar = mag * jnp.cos(lim * dt)
        ai = mag * jnp.sin(lim * dt)
        return ar, ai

    ar, ai = disc(lre_ref[...], lim_ref[...], ldt_ref[...])
    ar_ref[...] = ar
    ai_ref[...] = ai
    lre, lim = lre_r_ref[...], lim_r_ref[...]
    ar, ai = disc(lre, lim, ldt_r_ref[...])
    den = lre * lre + lim * lim
    wr = ((ar - 1.0) * lre + ai * lim) / den
    wi = (ai * lre - (ar - 1.0) * lim) / den
    bre, bim = bre_ref[...], bim_ref[...]
    bbre_ref[...] = wr * bre - wi * bim
    bbim_ref[...] = wr * bim + wi * bre


def _s5_prep(lam_re, lam_im, log_dt, b_re, b_im):
    n = lam_re.shape[0]
    rows = n * S5_GROUPS
    lre = lam_re.reshape(rows, S5_P)
    lim = lam_im.reshape(rows, S5_P)
    ldt = jnp.broadcast_to(log_dt.reshape(rows, 1), (rows, S5_P))
    rep = lambda a: jnp.repeat(a, S5_GROUP_CH, axis=1)
    args = (lre, lim, ldt, rep(lre), rep(lim), rep(ldt),
            b_re.reshape(rows, S5_P * S5_GROUP_CH), b_im.reshape(rows, S5_P * S5_GROUP_CH))
    wide = jax.ShapeDtypeStruct((rows, S5_P * S5_GROUP_CH), F32)
    narrow = jax.ShapeDtypeStruct((rows, S5_P), F32)
    ar, ai, bbre, bbim = pl.pallas_call(
        _s5_prep_kernel, out_shape=(narrow, narrow, wide, wide), name="s5_prep")(*args)
    shp = (n, S5_GROUPS, S5_P, S5_GROUP_CH)
    return ar.reshape(n, S5_GROUPS, S5_P), ai.reshape(n, S5_GROUPS, S5_P), bbre.reshape(shp), bbim.reshape(shp)


def _s5_kernel(u_ref, sg_ref, h0re_ref, h0im_ref, wbre_ref, wbim_ref, wc_ref, are_ref, aim_ref, d_ref,
               wglu_ref, bglu_ref, y_ref, hre_out, him_out, hre, him, cre, cim, ys, *, nb, tokens):
    step = pl.program_id(0)
    rows = nb * tokens

    @pl.when(step == 0)
    def _():
        cre[...] = h0re_ref[...]
        cim[...] = h0im_ref[...]

    for m in range(4):
        um = u_ref[:, m * LANES:(m + 1) * LANES].astype(BF16)
        hre[:, m * 512:(m + 1) * 512] = jnp.dot(um, wbre_ref[m], preferred_element_type=F32)
        him[:, m * 512:(m + 1) * 512] = jnp.dot(um, wbim_ref[m], preferred_element_type=F32)

    for qd in range(2):
        ql = slice(qd * 1024, (qd + 1) * 1024)
        ar = jnp.broadcast_to(are_ref[:, ql], (nb, 1024))
        ai = jnp.broadcast_to(aim_ref[:, ql], (nb, 1024))

        def tok(t, carry, ql=ql, ar=ar, ai=ai):
            hr, hi = carry
            sel = pl.ds(pl.multiple_of(t * nb, nb), nb)
            nr = ar * hr - ai * hi + hre[sel, ql]
            ni = ar * hi + ai * hr + him[sel, ql]
            hre[sel, ql] = nr
            him[sel, ql] = ni
            return nr, ni

        hr, hi = lax.fori_loop(0, tokens, tok, (cre[:, ql], cim[:, ql]))
        cre[:, ql] = hr
        cim[:, ql] = hi

    for m in range(4):
        hc = jnp.concatenate([hre[:, m * 512:(m + 1) * 512], him[:, m * 512:(m + 1) * 512]], axis=1)
        sl = slice(m * LANES, (m + 1) * LANES)
        y = _mm(hc, wc_ref[m]) + d_ref[:, sl] * u_ref[:, sl]
        ys[:, sl] = jax.nn.gelu(y)
    y = ys[...]
    gate = _sigmoid(_mm(y, wglu_ref[...]) + bglu_ref[...])
    y_ref[...] = (y * gate * _silu(sg_ref[...])).astype(BF16)

    @pl.when(step == pl.num_programs(0) - 1)
    def _():
        hre_out[...] = cre[...]
        him_out[...] = cim[...]


def _s5(u, sg, h0re, h0im, w, nb, tokens):
    t = u.shape[0]
    rows = nb * tokens
    row = lambda c: pl.BlockSpec((rows, c), lambda i: (i, 0))
    weights = (w["wbre"], w["wbim"], w["wc"], w["are"], w["aim"], w["d"], w["wglu"], w["bglu"])
    state = jax.ShapeDtypeStruct((nb, S5_STATE), F32)
    return pl.pallas_call(
        functools.partial(_s5_kernel, nb=nb, tokens=tokens),
        grid=(t // rows,),
        in_specs=[row(S5_WIDTH), row(S5_WIDTH), _const_spec(h0re.shape), _const_spec(h0im.shape)]
        + [_const_spec(a.shape) for a in weights],
        out_specs=(row(S5_WIDTH), _const_spec((nb, S5_STATE)), _const_spec((nb, S5_STATE))),
        out_shape=(jax.ShapeDtypeStruct((t, S5_WIDTH), BF16), state, state),
        scratch_shapes=[
            pltpu.VMEM((rows, S5_STATE), F32),
            pltpu.VMEM((rows, S5_STATE), F32),
            pltpu.VMEM((nb, S5_STATE), F32),
            pltpu.VMEM((nb, S5_STATE), F32),
            pltpu.VMEM((rows, S5_WIDTH), F32),
        ],
        compiler_params=_params(),
        name="s5",
    )(u, sg, h0re, h0im, *weights)


def _dn_solve(lms, rhss):
    nblk = CHUNK // SUB
    row = lax.broadcasted_iota(jnp.int32, (SUB, LANES), 0)
    lane = lax.broadcasted_iota(jnp.int32, (SUB, LANES), 1)
    seg = (lane // SUB) * SUB
    in_diag = [(lane >= SUB * blk) & (lane < SUB * (blk + 1)) for blk in range(nblk)]
    eye_pack = jnp.where((lane - seg == row) & (lane < CHUNK), 1.0, 0.0)
    zpad = jnp.zeros((CHUNK, LANES - CHUNK), F32)
    t_invs, lowers = [], []
    for lm in lms:
        wide = jnp.concatenate([lm, zpad], axis=1)
        tiles = [wide[SUB * blk:SUB * (blk + 1)] for blk in range(nblk)]
        dpack = jnp.zeros((SUB, LANES), F32)
        for blk in range(nblk):
            dpack = jnp.where(in_diag[blk], tiles[blk], dpack)
        inv = eye_pack
        for j in range(SUB - 1):
            col_j = jnp.take_along_axis(dpack, seg + j, axis=1)
            inv = inv - col_j * inv[j:j + 1, :]
        t_invs.append(jnp.concatenate(
            [jnp.where(in_diag[blk], inv, 0.0)[:, :CHUNK] for blk in range(nblk)], axis=0).astype(BF16))
        lowers.append(jnp.concatenate(
            [jnp.where(in_diag[blk], 0.0, tiles[blk])[:, :CHUNK] for blk in range(nblk)], axis=0).astype(BF16))
    d = lambda a, b: jnp.dot(a, b, preferred_element_type=F32)
    his = [rhs.astype(BF16) for rhs in rhss]
    los = [(rhs - hi.astype(F32)).astype(BF16) for rhs, hi in zip(rhss, his)]
    width = rhss[0].shape[1]
    applied = [d(t, jnp.concatenate([hi, lo_, lw], axis=1)) for t, hi, lo_, lw in zip(t_invs, his, los, lowers)]
    trs = [ap[:, :width] + ap[:, width:2 * width] for ap in applied]
    tns = [ap[:, 2 * width:] for ap in applied]
    xs = [[tr[0:SUB]] for tr in trs]
    for blk in range(1, nblk):
        lo = SUB * blk
        curs = [tr[lo:lo + SUB] - _mm(tn[lo:lo + SUB, 0:lo], jnp.concatenate(x, axis=0))
                for tr, tn, x in zip(trs, tns, xs)]
        for x, cur in zip(xs, curs):
            x.append(cur)
    return [jnp.concatenate(x, axis=0) for x in xs]


def _dn_prompt_kernel(x_ref, z_ref, beta_ref, gd_ref, cw_ref, nw_ref, c0_ref,
                      og_ref, sout_ref, cout_ref,
                      xs, qs, ks, vs, bcol, gcol, gc, gt, oscr, *, nb):
    rows = CHUNK * nb
    hist = (CONV_W - 1) * nb
    pad = PAD_TOKENS * nb
    step = pl.program_id(0)
    st = sout_ref

    @pl.when(step == 0)
    def _():
        st[...] = jnp.zeros_like(st)
        xs[0:hist, :] = c0_ref[...]
        gc[0, 0:pad, :] = jnp.zeros((pad, LANES), F32)

    xs[hist:hist + 64, :] = x_ref[0:64, :]

    def conv_rows(src, base, r0):
        tap = lambda i: base + r0 + i * nb if isinstance(r0, int) else pl.multiple_of(base + r0 + i * nb, 8)
        for c in range(DN_CONV_DIM // LANES):
            cl = slice(c * LANES, (c + 1) * LANES)
            y = src[pl.ds(tap(0), 64), cl] * cw_ref[0:1, cl]
            for i in range(1, CONV_W):
                y = y + src[pl.ds(tap(i), 64), cl] * cw_ref[i:i + 1, cl]
            y = _silu(y)
            if c < 2 * DN_HEADS:
                y = y * lax.rsqrt(jnp.sum(y * y, axis=-1, keepdims=True) + EPS)
            if c < DN_HEADS:
                qs[c, pl.ds(r0, 64), :] = y * (DN_DK ** -0.5)
            elif c < 2 * DN_HEADS:
                ks[c - DN_HEADS, pl.ds(r0, 64), :] = y
            else:
                vs[c - 2 * DN_HEADS, pl.ds(r0, 64), :] = y

    gc[0, pad:pad + rows, :] = gd_ref[...]
    _cumsum_tokens(gc, 0, pad, rows, nb)

    def bcast_rows(r0):
        gtile = gc[0, pl.ds(pad + r0, 64), :]
        btile = beta_ref[pl.ds(r0, 64), :]
        for h in range(DN_HEADS):
            gcol[h, pl.ds(r0, 64), :] = jnp.broadcast_to(gtile[:, h:h + 1], (64, LANES))
            bcol[h, pl.ds(r0, 64), :] = jnp.broadcast_to(btile[:, h:h + 1], (64, LANES))

    conv_rows(xs, 0, 0)
    bcast_rows(0)

    def conv_tile(ti, carry):
        r0 = pl.multiple_of(ti * 64, 64)
        conv_rows(x_ref, -hist, r0)
        bcast_rows(r0)
        return carry

    lax.fori_loop(1, rows // 64, conv_tile, 0)

    xs[0:hist, :] = x_ref[rows - hist:rows, :]

    ii = lax.broadcasted_iota(jnp.int32, (CHUNK, CHUNK), 0)
    jj = lax.broadcasted_iota(jnp.int32, (CHUNK, CHUNK), 1)
    eye = (ii == jj).astype(BF16)

    parts = [_split3(gc[0, pl.ds(pad + b, CHUNK, stride=nb), :]) for b in range(nb)]
    tdot = lambda x: lax.dot_general(x, eye, (((0,), (0,)), ((), ())), preferred_element_type=F32)
    firsts = [tdot(p[0]) for p in parts]
    seconds = [tdot(p[1]) for p in parts]
    thirds = [tdot(p[2]) for p in parts]
    for b in range(nb):
        gt[b] = (firsts[b] + seconds[b]) + thirds[b]

    def per_batch(bi, carry):
        def head_group(gi, carry2):
            probs = [(bi * DN_BATCH + jb, gi * DN_GROUP + jh) for jb in range(DN_BATCH) for jh in range(DN_GROUP)]
            sels = [pl.ds(b, CHUNK, stride=nb) for b, _ in probs]
            loaded = [(qs[h, sel, :], ks[h, sel, :], vs[h, sel, :], bcol[h, sel, :], gcol[h, sel, :],
                       gt[b, pl.ds(h, 1), :], st[b, h]) for (b, h), sel in zip(probs, sels)]
            qbs, kbs, vbs, bcs, gcls, grows, s_olds = zip(*loaded)
            decs = [jnp.where(jj <= ii, jnp.exp(jnp.where(jj <= ii, gcl[:, :CHUNK] - grow, 0.0)), 0.0)
                    for gcl, grow in zip(gcls, grows)]
            kqs = [_mm_nt(jnp.concatenate([kb, qb], axis=0), kb) for qb, kb in zip(qbs, kbs)]
            kks = [kq[:CHUNK] for kq in kqs]
            qks = [kq[CHUNK:] for kq in kqs]
            lms = [jnp.where(jj < ii, bc[:, :CHUNK] * kk * dec, 0.0) for bc, kk, dec in zip(bcs, kks, decs)]
            egs = [jnp.exp(gcl) for gcl in gcls]
            rhss = [jnp.concatenate([vb * bc, kb * (bc * eg)], axis=1)
                    for vb, kb, bc, eg in zip(vbs, kbs, bcs, egs)]
            sols = _dn_solve(lms, rhss)
            v_news = [sol[:, :DN_DV] - _mm(sol[:, DN_DV:], s_old) for sol, s_old in zip(sols, s_olds)]
            os_ = [_mm(qb * eg, s_old) + _mm(qk * dec, v_new)
                   for qb, eg, s_old, qk, dec, v_new in zip(qbs, egs, s_olds, qks, decs, v_news)]
            glasts = [gcl[CHUNK - 1:CHUNK, :] for gcl in gcls]
            s_news = [s_old * jnp.exp(glast) + _mm_tn(kb * jnp.exp(glast - gcl), v_new)
                      for s_old, glast, kb, gcl, v_new in zip(s_olds, glasts, kbs, gcls, v_news)]
            for (b, h), sel, o, s_new in zip(probs, sels, os_, s_news):
                st[b, h] = s_new
                oscr[h, sel, :] = o
            return carry2

        lax.fori_loop(0, DN_HEADS // DN_GROUP, head_group, 0)
        return carry

    lax.fori_loop(0, nb // DN_BATCH, per_batch, 0)

    nw = nw_ref[...]

    def epi_tile(ti, carry):
        r0 = pl.multiple_of(ti * 64, 64)
        for h in range(DN_HEADS):
            o = oscr[h, pl.ds(r0, 64), :]
            inv = lax.rsqrt(jnp.mean(o * o, axis=-1, keepdims=True) + EPS)
            cl = slice(h * LANES, (h + 1) * LANES)
            og_ref[pl.ds(r0, 64), cl] = (o * inv * nw * _silu(z_ref[pl.ds(r0, 64), cl])).astype(BF16)
        return carry

    lax.fori_loop(0, rows // 64, epi_tile, 0)

    @pl.when(step == pl.num_programs(0) - 1)
    def _():
        cout_ref[...] = xs[0:hist, :]


def _dn_prompt(xqkv, z, beta, gd, cw, nw, c0, nb):
    t = z.shape[0]
    rows = CHUNK * nb
    hist = (CONV_W - 1) * nb
    row = lambda c: pl.BlockSpec((rows, c), lambda i: (i, 0))
    slab = pltpu.VMEM((DN_HEADS, rows, LANES), F32)
    return pl.pallas_call(
        functools.partial(_dn_prompt_kernel, nb=nb),
        grid=(t // rows,),
        in_specs=[row(DN_CONV_DIM), row(1024), row(LANES), row(LANES), _const_spec(cw.shape),
                  _const_spec(nw.shape), _const_spec(c0.shape)],
        out_specs=(row(1024), _const_spec((nb, DN_HEADS, DN_DK, DN_DV)), _const_spec((hist, DN_CONV_DIM))),
        out_shape=(jax.ShapeDtypeStruct((t, 1024), BF16),
                   jax.ShapeDtypeStruct((nb, DN_HEADS, DN_DK, DN_DV), F32),
                   jax.ShapeDtypeStruct((hist, DN_CONV_DIM), F32)),
        scratch_shapes=[
            pltpu.VMEM((hist + 64, DN_CONV_DIM), F32),
            slab, slab, slab, slab, slab,
            pltpu.VMEM((1, PAD_TOKENS * nb + rows, LANES), F32),
            pltpu.VMEM((nb, LANES, CHUNK), F32),
            slab,
        ],
        compiler_params=_params(),
        name="dn_prompt",
    )(xqkv, z, beta, gd, cw, nw, c0)


def _dn_step_kernel(x_ref, cb_ref, z_ref, beta_ref, gd_ref, s_ref, cw_ref, nw_ref, *rest, layer, first):
    og_ref, sout_ref, cn_ref = rest[-3:]
    sout_ref = _own_state_block(sout_ref, layer, first)
    nw = nw_ref[...]
    cn_ref[0] = cb_ref[1]
    cn_ref[1] = cb_ref[2]
    cn_ref[2] = x_ref[...]
    beta = beta_ref[...]
    eg_all = jnp.exp(gd_ref[...])

    def conv(c):
        cl = slice(c * LANES, (c + 1) * LANES)
        y = cb_ref[0, :, cl] * cw_ref[0:1, cl]
        y = y + cb_ref[1, :, cl] * cw_ref[1:2, cl]
        y = y + cb_ref[2, :, cl] * cw_ref[2:3, cl]
        y = y + x_ref[:, cl] * cw_ref[3:4, cl]
        return _silu(y)

    def l2(y):
        return y * lax.rsqrt(jnp.sum(y * y, axis=-1, keepdims=True) + EPS)

    prow = lax.broadcasted_iota(jnp.int32, (16, 2 * LANES), 0)
    plane = lax.broadcasted_iota(jnp.int32, (16, 2 * LANES), 1)
    pick = [jnp.where(((prow == n) & (plane < LANES)) | ((prow == 8 + n) & (plane >= LANES)), 1.0, 0.0).astype(BF16)
            for n in range(8)]

    for h in range(DN_HEADS):
        qh = l2(conv(h)) * (DN_DK ** -0.5)
        kh = l2(conv(DN_HEADS + h))
        vh = conv(2 * DN_HEADS + h)
        kq = jnp.concatenate([kh, qh], axis=0).astype(BF16)
        cols = [lax.dot_general(kq, pick[n], (((0,), (0,)), ((), ())), preferred_element_type=F32)
                for n in range(8)]
        outs = []
        for n in range(8):
            s_old = s_ref[n, h]
            kcol = cols[n][:, :LANES]
            eg = eg_all[n:n + 1, h:h + 1]
            ks_row = jnp.sum(kcol * s_old, axis=0, keepdims=True)
            v_new = beta[n:n + 1, h:h + 1] * (vh[n:n + 1, :] - eg * ks_row)
            s_new = s_old * eg + kcol * v_new
            sout_ref[n, h] = s_new
            outs.append(jnp.sum(cols[n][:, LANES:] * s_new, axis=0, keepdims=True))
        o = jnp.concatenate(outs, axis=0)
        cl = slice(h * LANES, (h + 1) * LANES)
        og_ref[:, cl] = o * lax.rsqrt(jnp.mean(o * o, axis=-1, keepdims=True) + EPS) * nw * _silu(z_ref[:, cl])


def _dn_step(xqkv, cbuf, z, beta, gd, states, layer, prev, cw, nw):
    n = z.shape[0]
    row = lambda c: pl.BlockSpec((8, c), lambda i: (i, 0))
    cspec = pl.BlockSpec((CONV_W - 1, 8, DN_CONV_DIM), lambda i: (0, i, 0))
    sspec, ospec, extra_specs, extra_args, aliases = _stacked_state_args(states, layer, prev, 8, 1)
    return pl.pallas_call(
        functools.partial(_dn_step_kernel, layer=layer, first=prev is None),
        grid=(n // 8,),
        in_specs=[row(DN_CONV_DIM), cspec, row(1024), row(LANES), row(LANES), sspec,
                  _const_spec(cw.shape), _const_spec(nw.shape)] + extra_specs,
        out_specs=(row(1024), ospec, cspec),
        out_shape=(jax.ShapeDtypeStruct((n, 1024), F32), jax.ShapeDtypeStruct(states.shape, F32),
                   jax.ShapeDtypeStruct(cbuf.shape, F32)),
        input_output_aliases=aliases,
        compiler_params=_params(),
        name="dn_step",
    )(xqkv, cbuf, z, beta, gd, states, cw, nw, *extra_args)


def _pad_cols(a, n):
    return jnp.pad(a, ((0, 0), (0, n - a.shape[1])))


def _even_weights(w_in, w_gate_up, b_gate, w_out):
    return {
        "w": _pad_cols(w_in.astype(BF16), W_IN_PAD),
        "wg": jnp.pad(w_gate_up, ((0, LANES - GLA_RANK), (0, 0))).astype(BF16),
        "bg": b_gate.reshape(1, -1),
        "wo": w_out.astype(BF16),
    }


def _s5_weights(ar, ai, bbre, bbim, c_re, c_im, d, w_glu, b_glu):
    eye = jnp.eye(8, dtype=F32)

    def pack_b(bb):
        t = bb.reshape(4, 8, S5_P, S5_GROUP_CH).transpose(0, 1, 3, 2)
        return (t[:, :, :, None, :] * eye[None, :, None, :, None]).reshape(4, 128, 512).astype(BF16)

    def pack_c(c):
        t = c.reshape(4, 8, S5_GROUP_CH, S5_P).transpose(0, 1, 3, 2)
        return (t[:, :, :, None, :] * eye[None, :, None, :, None]).reshape(4, 512, 128)

    return {
        "wbre": pack_b(bbre), "wbim": pack_b(bbim),
        "wc": jnp.concatenate([pack_c(c_re), -pack_c(c_im)], axis=1).astype(BF16),
        "are": ar.reshape(1, S5_STATE), "aim": ai.reshape(1, S5_STATE),
        "d": d.reshape(1, -1), "wglu": w_glu.astype(BF16), "bglu": b_glu.reshape(1, -1),
    }


def _odd_weights(w_in, conv_w, a_log, dt_bias, w_out):
    return {
        "w": _pad_cols(w_in.astype(BF16), W_IN_PAD),
        "alog": _pad_cols(a_log.reshape(1, -1), LANES), "dtb": _pad_cols(dt_bias.reshape(1, -1), LANES),
        "cw": jnp.pad(conv_w, ((0, 8 - CONV_W), (0, 0))),
        "wo": w_out.astype(BF16),
    }


def kernel(x_prompt, x_sample, state_gla, state_s5_re, state_s5_im, state_delta, state_conv, norm_w, final_norm_w, w_in_even, gla_w_gate_up, gla_b_gate, gla_norm_w, s5_lambda_re, s5_lambda_im, s5_log_dt, s5_b_re, s5_b_im, s5_c_re, s5_c_im, s5_d, s5_w_glu, s5_b_glu, w_out_even, w_in_odd, dn_conv_w, dn_a_log, dn_dt_bias, dn_norm_w, w_out_odd):
    nb, seq, _ = x_prompt.shape
    ns = x_sample.shape[0]
    assert seq % CHUNK == 0 and nb % 8 == 0 and ns % 8 == 0 and x_sample.shape[1] == 1
    depth = norm_w.shape[0]

    assert depth % 2 == 0 and state_gla.shape[0] == depth // 2 and state_delta.shape[0] == depth // 2
    xp = None
    xs = x_sample.reshape(ns, D_MODEL)
    fw = final_norm_w.reshape(1, -1)

    ar, ai, bbre, bbim = _s5_prep(s5_lambda_re, s5_lambda_im, s5_log_dt, s5_b_re, s5_b_im)

    gla_p, s5r_p, s5i_p, s5r_s, s5i_s, dn_p, cv_p, cv_s = ([] for _ in range(8))
    gla_s = dn_s = None
    for layer in range(depth):
        i = layer // 2
        nw = norm_w[layer].reshape(1, -1)
        if layer % 2 == 0:
            w = _even_weights(w_in_even[i], gla_w_gate_up[i], gla_b_gate[i], w_out_even[i])
            w5 = _s5_weights(ar[i], ai[i], bbre[i], bbim[i], s5_c_re[i], s5_c_im[i], s5_d[i],
                             s5_w_glu[i], s5_b_glu[i])
            gnw = gla_norm_w[i].reshape(1, -1)
            if layer == 0:
                q, k, g, v, r, u, sg, xp = _proj_even(x_prompt, nw, w, batch_major=True)
            else:
                q, k, g, v, r, u, sg, xp = _proj_even(xp, nw, w, pending=pending)
            og, s_new = _gla_prompt(q, k, g, v, r, gnw, nb)
            zero = jnp.zeros((nb, S5_STATE), F32)
            y5, hre, him = _s5(u, sg, zero, zero, w5, nb, CHUNK)
            pending = (og, y5, w["wo"])
            gla_p.append(s_new)
            s5r_p.append(hre.reshape(nb, S5_GROUPS, S5_P))
            s5i_p.append(him.reshape(nb, S5_GROUPS, S5_P))
            q, k, g, v, r, u, sg = _proj_even(xs, nw, w)
            og, gla_s = _gla_step(q, k, g, v, r, state_gla, i, gla_s, gnw)
            y5, hre, him = _s5(u, sg, state_s5_re[i].reshape(ns, S5_STATE), state_s5_im[i].reshape(ns, S5_STATE),
                               w5, ns, 1)
            xs = _out_even(xs, og, y5, w["wo"])
            s5r_s.append(hre.reshape(ns, S5_GROUPS, S5_P))
            s5i_s.append(him.reshape(ns, S5_GROUPS, S5_P))
        else:
            w = _odd_weights(w_in_odd[i], dn_conv_w[i], dn_a_log[i], dn_dt_bias[i], w_out_odd[i])
            dnw = dn_norm_w[i].reshape(1, -1)
            final = layer == depth - 1
            xqkv, z, beta, gd, xp = _proj_odd(xp, nw, w, pending=pending)
            c0 = jnp.zeros(((CONV_W - 1) * nb, DN_CONV_DIM), F32)
            og, s_new, c_new = _dn_prompt(xqkv, z, beta, gd, w["cw"], dnw, c0, nb)
            if final:
                xp = _out_odd(xp, og, w["wo"], fw, final, nb_out=nb)
            else:
                pending = (og, w["wo"])
            dn_p.append(s_new)
            cv_p.append(c_new.reshape(CONV_W - 1, nb, DN_CONV_DIM).transpose(1, 0, 2))
            xqkv, z, beta, gd = _proj_odd(xs, nw, w)
            og, dn_s, c_new = _dn_step(xqkv, state_conv[i].transpose(1, 0, 2), z, beta, gd, state_delta, i, dn_s,
                                       w["cw"], dnw)
            xs = _out_odd(xs, og, w["wo"], fw, final)
            cv_s.append(c_new.transpose(1, 0, 2))

    y_prompt = xp
    y_sample = xs.reshape(ns, 1, D_MODEL)
    st = jnp.stack
    return (y_prompt, y_sample, st(gla_p), gla_s, st(s5r_p), st(s5i_p), st(s5r_s), st(s5i_s),
            st(dn_p), dn_s, st(cv_p), st(cv_s))
```

```python
import functools
import math

import jax
import jax.numpy as jnp
from jax import lax
from jax.experimental import pallas as pl
from jax.experimental.pallas import tpu as pltpu

F32 = jnp.float32
BF16 = jnp.bfloat16
EPS = 1e-6

D_MODEL = 1024
GLA_HEADS, GLA_DK, GLA_DV, GLA_RANK = 4, 128, 256, 16
GLA_GATE_NORM = 16.0
S5_GROUPS, S5_GROUP_CH, S5_P, S5_WIDTH = 32, 16, 64, 512
S5_STATE = S5_GROUPS * S5_P
DN_HEADS, DN_DK, DN_DV = 8, 128, 128
DN_CONV_DIM, CONV_W = 3072, 4
CHUNK = 64
SUB = 8
GLA_BAND = 8
GLA_BATCH = 4
GLA_SAFE_SPAN = 60.0
PAD_TOKENS = 16
DN_BATCH = 8
DN_GROUP = 8
LANES = 128
W_IN_PAD = 4224
VMEM_LIMIT = 56 * 1024 * 1024


def _mm(a, b):
    return jnp.dot(a.astype(BF16), b.astype(BF16), preferred_element_type=F32)


def _mm_nt(a, b):
    return lax.dot_general(a.astype(BF16), b.astype(BF16), (((1,), (1,)), ((), ())),
                           preferred_element_type=F32)


def _mm_tn(a, b):
    return lax.dot_general(a.astype(BF16), b.astype(BF16), (((0,), (0,)), ((), ())),
                           preferred_element_type=F32)


def _split3(a):
    a1 = a.astype(BF16)
    r1 = a - a1.astype(F32)
    a2 = r1.astype(BF16)
    a3 = (r1 - a2.astype(F32)).astype(BF16)
    return a1, a2, a3


def _mm_tn_exact(a, m_bf16):
    a1, a2, a3 = _split3(a)
    d = lambda x: lax.dot_general(x, m_bf16, (((0,), (0,)), ((), ())), preferred_element_type=F32)
    return (d(a1) + d(a2)) + d(a3)


def _sigmoid(x):
    return 1.0 / (1.0 + jnp.exp(-x))


def _silu(x):
    half = 0.5 * x
    return half + half * jnp.tanh(half)


def _softplus(x):
    return jnp.maximum(x, 0.0) + jnp.log1p(jnp.exp(-jnp.abs(x)))


def _rms_rows(x, w):
    ms = jnp.mean(x * x, axis=-1, keepdims=True)
    return x * lax.rsqrt(ms + EPS) * w


def _const_spec(shape):
    nd = len(shape)
    return pl.BlockSpec(shape, lambda i, _nd=nd: (0,) * _nd)


def _params(**flags):
    return pltpu.CompilerParams(dimension_semantics=("arbitrary",), vmem_limit_bytes=VMEM_LIMIT,
                                flags=flags or None)


def _row_tile(t):
    return min(512, t)


def _add_pending(x, pending):
    if len(pending) == 2:
        og_ref, wo_ref = pending
        return x + _mm(og_ref[...], wo_ref[...])
    og_ref, y5_ref, wo_ref = pending
    return x + (_mm(og_ref[...], wo_ref[0:1024, :]) + _mm(y5_ref[...], wo_ref[1024:, :]))


def _proj_even_kernel(x_ref, nw_ref, w_ref, wg_ref, bg_ref, *rest, nb_in, n_pending):
    pending = rest[:n_pending]
    q_ref, k_ref, g_ref, v_ref, r_ref, u_ref, sg_ref = rest[n_pending:n_pending + 7]
    tail = rest[n_pending + 7:]
    if nb_in:
        xo_ref, xt = tail
        for b in range(nb_in):
            for c in range(D_MODEL // LANES):
                xt[c, pl.ds(b, x_ref.shape[1], stride=nb_in), :] = x_ref[b, :, c * LANES:(c + 1) * LANES]
        x = jnp.concatenate([xt[c] for c in range(D_MODEL // LANES)], axis=1)
        xo_ref[...] = x
    elif n_pending:
        xo_ref, = tail
        x = _add_pending(x_ref[...], pending)
        xo_ref[...] = x
    else:
        x = x_ref[...]
    hb = _rms_rows(x, nw_ref[...]).astype(BF16)
    d = lambda lo, hi: jnp.dot(hb, w_ref[:, lo:hi], preferred_element_type=F32)
    q = d(0, 512) * (GLA_DK ** -0.5)
    k = d(512, 1024)
    tail = d(3072, W_IN_PAD)
    logit = jnp.dot(tail[:, :LANES].astype(BF16), wg_ref[...], preferred_element_type=F32) + bg_ref[...]
    g = -_softplus(-logit) / GLA_GATE_NORM
    for h in range(GLA_HEADS):
        sl = slice(h * LANES, (h + 1) * LANES)
        q_ref[h] = q[:, sl]
        k_ref[h] = k[:, sl]
        g_ref[h] = g[:, sl]
    v = d(1024, 2048)
    for s in range(2 * GLA_HEADS):
        v_ref[s] = v[:, s * LANES:(s + 1) * LANES]
    r_ref[...] = d(2048, 3072)
    u_ref[...] = tail[:, GLA_RANK:GLA_RANK + S5_WIDTH]
    sg_ref[...] = tail[:, GLA_RANK + S5_WIDTH:GLA_RANK + 2 * S5_WIDTH]


def _pending_specs(pending, tm):
    *acts, wo = pending
    return [pl.BlockSpec((tm, a.shape[1]), lambda i: (i, 0)) for a in acts] + [_const_spec(wo.shape)]


def _proj_even(x, nw, w, batch_major=False, pending=()):
    nb_in = x.shape[0] if batch_major else 0
    t = x.shape[0] * x.shape[1] if batch_major else x.shape[0]
    tm = _row_tile(t)
    row = lambda c: pl.BlockSpec((tm, c), lambda i: (i, 0))
    slab = lambda n: pl.BlockSpec((n, tm, LANES), lambda i: (0, i, 0))
    x_spec = pl.BlockSpec((nb_in, tm // nb_in, D_MODEL), lambda i: (0, i, 0)) if batch_major else row(D_MODEL)
    new_x = batch_major or bool(pending)
    extra_out_shape = (jax.ShapeDtypeStruct((t, D_MODEL), F32),) if new_x else ()
    extra_out_specs = (row(D_MODEL),) if new_x else ()
    scratch = [pltpu.VMEM((D_MODEL // LANES, tm, LANES), F32)] if batch_major else []
    pend_specs = _pending_specs(pending, tm) if pending else []
    out_shape = (
        jax.ShapeDtypeStruct((GLA_HEADS, t, LANES), F32),
        jax.ShapeDtypeStruct((GLA_HEADS, t, LANES), F32),
        jax.ShapeDtypeStruct((GLA_HEADS, t, LANES), F32),
        jax.ShapeDtypeStruct((2 * GLA_HEADS, t, LANES), F32),
        jax.ShapeDtypeStruct((t, 1024), F32),
        jax.ShapeDtypeStruct((t, S5_WIDTH), F32),
        jax.ShapeDtypeStruct((t, S5_WIDTH), F32),
    )
    weights = (w["w"], w["wg"], w["bg"])
    return pl.pallas_call(
        functools.partial(_proj_even_kernel, nb_in=nb_in, n_pending=len(pending)),
        grid=(t // tm,),
        in_specs=[x_spec, _const_spec(nw.shape)] + [_const_spec(a.shape) for a in weights] + pend_specs,
        out_specs=(slab(4), slab(4), slab(4), slab(8), row(1024), row(S5_WIDTH), row(S5_WIDTH)) + extra_out_specs,
        out_shape=out_shape + extra_out_shape,
        scratch_shapes=scratch,
        compiler_params=_params(),
        name="proj_even",
    )(x, nw, *weights, *pending)


def _proj_odd_kernel(x_ref, nw_ref, w_ref, alog_ref, dtb_ref, *rest, n_pending):
    pending = rest[:n_pending]
    xqkv_ref, z_ref, beta_ref, gd_ref = rest[n_pending:n_pending + 4]
    x = x_ref[...]
    if n_pending:
        xo_ref, = rest[n_pending + 4:]
        x = _add_pending(x, pending)
        xo_ref[...] = x
    hb = _rms_rows(x, nw_ref[...]).astype(BF16)
    d = lambda lo, hi: jnp.dot(hb, w_ref[:, lo:hi], preferred_element_type=F32)
    xqkv_ref[...] = d(0, DN_CONV_DIM)
    z_ref[...] = d(DN_CONV_DIM, DN_CONV_DIM + 1024)
    tail = d(DN_CONV_DIM + 1024, DN_CONV_DIM + 1024 + LANES)
    beta_ref[...] = _sigmoid(tail)
    a = pltpu.roll(tail, LANES - DN_HEADS, 1)
    gd_ref[...] = -jnp.exp(alog_ref[...]) * _softplus(a + dtb_ref[...])


def _proj_odd(x, nw, w, pending=()):
    t = x.shape[0]
    tm = _row_tile(t)
    row = lambda c: pl.BlockSpec((tm, c), lambda i: (i, 0))
    weights = (w["w"], w["alog"], w["dtb"])
    pend_specs = _pending_specs(pending, tm) if pending else []
    extra_out_shape = (jax.ShapeDtypeStruct((t, D_MODEL), F32),) if pending else ()
    extra_out_specs = (row(D_MODEL),) if pending else ()
    return pl.pallas_call(
        functools.partial(_proj_odd_kernel, n_pending=len(pending)),
        grid=(t // tm,),
        in_specs=[row(D_MODEL), _const_spec(nw.shape)] + [_const_spec(a.shape) for a in weights] + pend_specs,
        out_specs=(row(DN_CONV_DIM), row(1024), row(LANES), row(LANES)) + extra_out_specs,
        out_shape=(
            jax.ShapeDtypeStruct((t, DN_CONV_DIM), F32),
            jax.ShapeDtypeStruct((t, 1024), F32),
            jax.ShapeDtypeStruct((t, LANES), F32),
            jax.ShapeDtypeStruct((t, LANES), F32),
        ) + extra_out_shape,
        compiler_params=_params(),
        name="proj_odd",
    )(x, nw, *weights, *pending)


def _out_even_kernel(x_ref, og_ref, y5_ref, w_ref, o_ref):
    o_ref[...] = x_ref[...] + (_mm(og_ref[...], w_ref[0:1024, :]) + _mm(y5_ref[...], w_ref[1024:, :]))


def _out_even(x, og, y5, w):
    t = x.shape[0]
    tm = _row_tile(t)
    row = lambda c: pl.BlockSpec((tm, c), lambda i: (i, 0))
    return pl.pallas_call(
        _out_even_kernel,
        grid=(t // tm,),
        in_specs=[row(D_MODEL), row(1024), row(S5_WIDTH), _const_spec(w.shape)],
        out_specs=row(D_MODEL),
        out_shape=jax.ShapeDtypeStruct((t, D_MODEL), F32),
        compiler_params=_params(),
        name="out_even",
    )(x, og, y5, w)


def _out_odd_kernel(x_ref, og_ref, w_ref, fw_ref, o_ref, *slabs, final, nb_out):
    y = x_ref[...] + _mm(og_ref[...], w_ref[...])
    if final:
        y = _rms_rows(y, fw_ref[...])
    if nb_out:
        ys, = slabs
        for c in range(D_MODEL // LANES):
            ys[c] = y[:, c * LANES:(c + 1) * LANES]
        for b in range(nb_out):
            for c in range(D_MODEL // LANES):
                o_ref[b, :, c * LANES:(c + 1) * LANES] = ys[c, pl.ds(b, o_ref.shape[1], stride=nb_out), :]
    else:
        o_ref[...] = y


def _out_odd(x, og, w, fw, final, nb_out=0):
    t = x.shape[0]
    tm = _row_tile(t)
    row = lambda c: pl.BlockSpec((tm, c), lambda i: (i, 0))
    if nb_out:
        out_spec = pl.BlockSpec((nb_out, tm // nb_out, D_MODEL), lambda i: (0, i, 0))
        out_shape = jax.ShapeDtypeStruct((nb_out, t // nb_out, D_MODEL), F32)
        scratch = [pltpu.VMEM((D_MODEL // LANES, tm, LANES), F32)]
    else:
        out_spec, out_shape, scratch = row(D_MODEL), jax.ShapeDtypeStruct((t, D_MODEL), F32), []
    return pl.pallas_call(
        functools.partial(_out_odd_kernel, final=final, nb_out=nb_out),
        grid=(t // tm,),
        in_specs=[row(D_MODEL), row(1024), _const_spec(w.shape), _const_spec(fw.shape)],
        out_specs=out_spec,
        out_shape=out_shape,
        scratch_shapes=scratch,
        compiler_params=_params(),
        name="out_odd",
    )(x, og, w, fw)


def _cumsum_tokens(ref, lead, pad, rows, nb):
    shift = nb
    while shift <= pad:
        ref[lead, pad:pad + rows, :] = ref[lead, pad:pad + rows, :] + ref[lead, pad - shift:pad + rows - shift, :]
        shift *= 2
    while shift < rows:
        ref[lead, pad + shift:pad + rows, :] = (ref[lead, pad + shift:pad + rows, :]
                                                + ref[lead, pad:pad + rows - shift, :])
        shift *= 2


def _gla_prompt_kernel(q_ref, k_ref, g_ref, v_ref, r_ref, far_ref, near_ref, nw_ref, og_ref, sout_ref,
                       kp, bp, vp, st, oscr, ocar, *, nb):
    rows = CHUNK * nb
    pad = PAD_TOKENS * nb
    blk_rows = GLA_BAND * nb
    step = pl.program_id(0)

    @pl.when(step == 0)
    def _():
        st[...] = jnp.zeros_like(st)
        kp[:, 0:pad, :] = jnp.zeros((GLA_HEADS, pad, LANES), F32)
        bp[:, 0:pad, :] = jnp.zeros((GLA_HEADS, pad, LANES), F32)
        vp[:, 0:pad, :] = jnp.zeros((2 * GLA_HEADS, pad, LANES), F32)

    for h in range(GLA_HEADS):
        bp[h, pad:pad + rows, :] = g_ref[h]
        kp[h, pad:pad + rows, :] = k_ref[h]
        _cumsum_tokens(bp, h, pad, rows, nb)
    for s in range(2 * GLA_HEADS):
        vp[s, pad:pad + rows, :] = v_ref[s]

    nblk = CHUNK // GLA_BAND

    span = jnp.zeros((nb, LANES), F32)
    for h in range(GLA_HEADS):
        for blk in range(nblk):
            lo = pad + blk * blk_rows
            span = jnp.maximum(span, bp[h, lo - nb:lo, :] - bp[h, lo + blk_rows - nb:lo + blk_rows, :])
    safe = jnp.max(span) <= GLA_SAFE_SPAN

    def pairs_on_mxu(first_blk, mask_ref, accumulate):
        for h in range(GLA_HEADS):
            probs = []
            for blk in range(first_blk, nblk):
                lo = blk * blk_rows
                hi = lo + (1 - first_blk) * blk_rows
                ref_b = bp[h, pad + lo - nb:pad + lo, :]
                qs = q_ref[h, lo:lo + blk_rows, :] * jnp.exp(
                    bp[h, pad + lo:pad + lo + blk_rows, :] - jnp.concatenate([ref_b] * GLA_BAND, axis=0))
                ks = kp[h, pad:pad + hi, :] * jnp.exp(
                    jnp.concatenate([ref_b] * (hi // nb), axis=0) - bp[h, pad:pad + hi, :])
                probs.append((hi, _mm_nt(qs, ks) * mask_ref[lo:lo + blk_rows, 0:hi]))
            for blk in range(first_blk, nblk):
                lo = blk * blk_rows
                hi, p = probs[blk - first_blk]
                p = p.astype(BF16)
                for half in range(2):
                    sl = 2 * h + half
                    pv = jnp.dot(p, vp[sl, pad:pad + hi, :].astype(BF16), preferred_element_type=F32)
                    oscr[sl, lo:lo + blk_rows, :] = oscr[sl, lo:lo + blk_rows, :] + pv if accumulate else pv

    def self_tile(ti, carry):
        r0 = pl.multiple_of(ti * 64, 64)
        for h in range(GLA_HEADS):
            w = jnp.sum(q_ref[h, pl.ds(r0, 64), :] * k_ref[h, pl.ds(r0, 64), :], axis=-1, keepdims=True)
            oscr[2 * h, pl.ds(r0, 64), :] = w * v_ref[2 * h, pl.ds(r0, 64), :]
            oscr[2 * h + 1, pl.ds(r0, 64), :] = w * v_ref[2 * h + 1, pl.ds(r0, 64), :]
        return carry

    @pl.when(safe)
    def _():
        lax.fori_loop(0, rows // 64, self_tile, 0)
        pairs_on_mxu(0, near_ref, True)

    def band_tile(ti, carry):
        r0 = pl.multiple_of(ti * 64, 64)
        for h in range(GLA_HEADS):
            qt = q_ref[h, pl.ds(r0, 64), :]
            bt = bp[h, pl.ds(pad + r0, 64), :]
            acc0 = jnp.zeros((64, LANES), F32)
            acc1 = jnp.zeros((64, LANES), F32)
            for d in range(GLA_BAND):
                off = pl.multiple_of(pad + r0 - d * nb, 8)
                ks = kp[h, pl.ds(off, 64), :]
                bs = bp[h, pl.ds(off, 64), :]
                w = jnp.sum(qt * ks * jnp.exp(bt - bs), axis=-1, keepdims=True)
                acc0 = acc0 + w * vp[2 * h, pl.ds(off, 64), :]
                acc1 = acc1 + w * vp[2 * h + 1, pl.ds(off, 64), :]
            oscr[2 * h, pl.ds(r0, 64), :] = acc0
            oscr[2 * h + 1, pl.ds(r0, 64), :] = acc1
        return carry

    @pl.when(jnp.logical_not(safe))
    def _():
        lax.fori_loop(0, rows // 64, band_tile, 0)
        pairs_on_mxu(1, far_ref, True)

    def per_batch(bi, carry):
        probs = [(bi * GLA_BATCH + jb, h) for jb in range(GLA_BATCH) for h in range(GLA_HEADS)]
        sels = [pl.ds(b, CHUNK, stride=nb) for b, _ in probs]
        loaded = [(q_ref[h, sel, :], kp[h, pl.ds(pad + b, CHUNK, stride=nb), :],
                   bp[h, pl.ds(pad + b, CHUNK, stride=nb), :], vp[2 * h, pl.ds(pad + b, CHUNK, stride=nb), :],
                   vp[2 * h + 1, pl.ds(pad + b, CHUNK, stride=nb), :], st[b, h])
                  for (b, h), sel in zip(probs, sels)]
        outs = [_mm_nt(qb * jnp.exp(bb), stt) for qb, _, bb, _, _, stt in loaded]
        news = []
        for _, kb, bb, v0, v1, stt in loaded:
            blast = bb[CHUNK - 1:CHUNK, :]
            kd = kb * jnp.exp(blast - bb)
            news.append(stt * jnp.exp(blast) + _mm_tn(jnp.concatenate([v0, v1], axis=1), kd))
        for (b, h), sel, out, new in zip(probs, sels, outs, news):
            ocar[2 * h, sel, :] = out[:, :LANES]
            ocar[2 * h + 1, sel, :] = out[:, LANES:]
            st[b, h] = new
        return carry

    lax.fori_loop(0, nb // GLA_BATCH, per_batch, 0)

    nw = nw_ref[...]

    def epi_tile(ti, carry):
        r0 = pl.multiple_of(ti * 64, 64)
        for h in range(GLA_HEADS):
            o0 = oscr[2 * h, pl.ds(r0, 64), :] + ocar[2 * h, pl.ds(r0, 64), :]
            o1 = oscr[2 * h + 1, pl.ds(r0, 64), :] + ocar[2 * h + 1, pl.ds(r0, 64), :]
            ms = (jnp.sum(o0 * o0, axis=-1, keepdims=True) + jnp.sum(o1 * o1, axis=-1, keepdims=True)) / GLA_DV
            inv = lax.rsqrt(ms + EPS)
            c0 = h * GLA_DV
            og_ref[pl.ds(r0, 64), c0:c0 + LANES] = (
                o0 * inv * nw[:, :LANES] * _silu(r_ref[pl.ds(r0, 64), c0:c0 + LANES])).astype(BF16)
            og_ref[pl.ds(r0, 64), c0 + LANES:c0 + 2 * LANES] = (
                o1 * inv * nw[:, LANES:] * _silu(r_ref[pl.ds(r0, 64), c0 + LANES:c0 + 2 * LANES])).astype(BF16)
        return carry

    lax.fori_loop(0, rows // 64, epi_tile, 0)

    @pl.when(step == pl.num_programs(0) - 1)
    def _():
        def wr(b, carry):
            for h in range(GLA_HEADS):
                sout_ref[b, h] = st[b, h].T
            return carry
        lax.fori_loop(0, nb, wr, 0)


def _pair_mask(nb, min_dist):
    n = CHUNK * nb
    r = jnp.arange(n)
    same = (r[:, None] % nb) == (r[None, :] % nb)
    far = (r[None, :] // nb) <= (r[:, None] // nb) - min_dist
    return (same & far).astype(F32)


def _gla_prompt(q, k, g, v, r, nw, nb):
    t = r.shape[0]
    rows = CHUNK * nb
    pad = PAD_TOKENS * nb
    slab = lambda n: pl.BlockSpec((n, rows, LANES), lambda i: (0, i, 0))
    row = lambda c: pl.BlockSpec((rows, c), lambda i: (i, 0))
    far = _pair_mask(nb, GLA_BAND)
    near = _pair_mask(nb, 1)
    return pl.pallas_call(
        functools.partial(_gla_prompt_kernel, nb=nb),
        grid=(t // rows,),
        in_specs=[slab(4), slab(4), slab(4), slab(8), row(1024), _const_spec(far.shape), _const_spec(near.shape),
                  _const_spec(nw.shape)],
        out_specs=(row(1024), _const_spec((nb, GLA_HEADS, GLA_DK, GLA_DV))),
        out_shape=(jax.ShapeDtypeStruct((t, 1024), BF16),
                   jax.ShapeDtypeStruct((nb, GLA_HEADS, GLA_DK, GLA_DV), F32)),
        scratch_shapes=[
            pltpu.VMEM((GLA_HEADS, pad + rows, LANES), F32),
            pltpu.VMEM((GLA_HEADS, pad + rows, LANES), F32),
            pltpu.VMEM((2 * GLA_HEADS, pad + rows, LANES), F32),
            pltpu.VMEM((nb, GLA_HEADS, GLA_DV, GLA_DK), F32),
            pltpu.VMEM((2 * GLA_HEADS, rows, LANES), F32),
            pltpu.VMEM((2 * GLA_HEADS, rows, LANES), F32),
        ],
        compiler_params=_params(),
        name="gla_prompt",
    )(q, k, g, v, r, far, near, nw)


def _gla_step_kernel(q_ref, k_ref, g_ref, v_ref, r_ref, s_ref, nw_ref, *rest, layer, first):
    og_ref, sout_ref = rest[-2:]
    sout_ref = _own_state_block(sout_ref, layer, first)
    nw = nw_ref[...]
    zeros = jnp.zeros((LANES - 8, LANES), F32)
    prow = lax.broadcasted_iota(jnp.int32, (16, 2 * GLA_DV), 0)
    plane = lax.broadcasted_iota(jnp.int32, (16, 2 * GLA_DV), 1)
    pick = [jnp.where(((prow == n) & (plane < GLA_DV)) | ((prow == 8 + n) & (plane >= GLA_DV)), 1.0, 0.0).astype(BF16)
            for n in range(8)]
    for h in range(GLA_HEADS):
        decay = jnp.concatenate([jnp.exp(g_ref[h]), zeros], axis=0).T
        kq = jnp.concatenate([k_ref[h], q_ref[h]], axis=0).astype(BF16)
        cols = [lax.dot_general(kq, pick[n], (((0,), (0,)), ((), ())), preferred_element_type=F32)
                for n in range(8)]
        outs = []
        for n in range(8):
            s_old = s_ref[n, h]
            vrow = jnp.concatenate([v_ref[2 * h, n:n + 1, :], v_ref[2 * h + 1, n:n + 1, :]], axis=1)
            s_new = s_old * decay[:, n:n + 1] + cols[n][:, :GLA_DV] * vrow
            sout_ref[n, h] = s_new
            outs.append(jnp.sum(cols[n][:, GLA_DV:] * s_new, axis=0, keepdims=True))
        o = jnp.concatenate(outs, axis=0)
        ms = jnp.mean(o * o, axis=-1, keepdims=True)
        c0 = h * GLA_DV
        og_ref[:, c0:c0 + GLA_DV] = o * lax.rsqrt(ms + EPS) * nw * _silu(r_ref[:, c0:c0 + GLA_DV])


def _stacked_state_args(states, layer, prev, n_inputs, out_index):
    tail = states.shape[2:]
    zeros = (0,) * len(tail)
    spec = pl.BlockSpec((None, 8) + tail, lambda i, _l=layer: (_l, i) + zeros)
    if prev is None:
        return spec, pl.BlockSpec((states.shape[0], 8) + tail, lambda i: (0, i) + zeros), [], [], {}
    return spec, spec, [pl.BlockSpec(memory_space=pl.ANY)], [prev], {n_inputs: out_index}


def _own_state_block(sout_ref, layer, first):
    if not first:
        return sout_ref
    for other in range(sout_ref.shape[0]):
        if other != layer:
            sout_ref[other] = jnp.zeros(sout_ref.shape[1:], F32)
    return sout_ref.at[layer]


def _gla_step(q, k, g, v, r, states, layer, prev, nw):
    n = r.shape[0]
    slab = lambda c: pl.BlockSpec((c, 8, LANES), lambda i: (0, i, 0))
    sspec, ospec, extra_specs, extra_args, aliases = _stacked_state_args(states, layer, prev, 7, 1)
    return pl.pallas_call(
        functools.partial(_gla_step_kernel, layer=layer, first=prev is None),
        grid=(n // 8,),
        in_specs=[slab(4), slab(4), slab(4), slab(8), pl.BlockSpec((8, 1024), lambda i: (i, 0)), sspec,
                  _const_spec(nw.shape)] + extra_specs,
        out_specs=(pl.BlockSpec((8, 1024), lambda i: (i, 0)), ospec),
        out_shape=(jax.ShapeDtypeStruct((n, 1024), F32), jax.ShapeDtypeStruct(states.shape, F32)),
        input_output_aliases=aliases,
        compiler_params=_params(),
        name="gla_step",
    )(q, k, g, v, r, states, nw, *extra_args)


def _s5_prep_kernel(lre_ref, lim_ref, ldt_ref, lre_r_ref, lim_r_ref, ldt_r_ref, bre_ref, bim_ref,
                    ar_ref, ai_ref, bbre_ref, bbim_ref):
    def disc(lre, lim, ldt):
        dt = jnp.exp(ldt)
        mag = jnp.exp(lre * dt)
        ar = mag * jnp.cos(lim * dt)
        ai = mag * jnp.sin(lim * dt)
        return ar, ai

    ar, ai = disc(lre_ref[...], lim_ref[...], ldt_ref[...])
    ar_ref[...] = ar
    ai_ref[...] = ai
    lre, lim = lre_r_ref[...], lim_r_ref[...]
    ar, ai = disc(lre, lim, ldt_r_ref[...])
    den = lre * lre + lim * lim
    wr = ((ar - 1.0) * lre + ai * lim) / den
    wi = (ai * lre - (ar - 1.0) * lim) / den
    bre, bim = bre_ref[...], bim_ref[...]
    bbre_ref[...] = wr * bre - wi * bim
    bbim_ref[...] = wr * bim + wi * bre


def _s5_prep(lam_re, lam_im, log_dt, b_re, b_im):
    n = lam_re.shape[0]
    rows = n * S5_GROUPS
    lre = lam_re.reshape(rows, S5_P)
    lim = lam_im.reshape(rows, S5_P)
    ldt = jnp.broadcast_to(log_dt.reshape(rows, 1), (rows, S5_P))
    rep = lambda a: jnp.repeat(a, S5_GROUP_CH, axis=1)
    args = (lre, lim, ldt, rep(lre), rep(lim), rep(ldt),
            b_re.reshape(rows, S5_P * S5_GROUP_CH), b_im.reshape(rows, S5_P * S5_GROUP_CH))
    wide = jax.ShapeDtypeStruct((rows, S5_P * S5_GROUP_CH), F32)
    narrow = jax.ShapeDtypeStruct((rows, S5_P), F32)
    ar, ai, bbre, bbim = pl.pallas_call(
        _s5_prep_kernel, out_shape=(narrow, narrow, wide, wide), name="s5_prep")(*args)
    shp = (n, S5_GROUPS, S5_P, S5_GROUP_CH)
    return ar.reshape(n, S5_GROUPS, S5_P), ai.reshape(n, S5_GROUPS, S5_P), bbre.reshape(shp), bbim.reshape(shp)


def _s5_kernel(u_ref, sg_ref, h0re_ref, h0im_ref, wbre_ref, wbim_ref, wc_ref, are_ref, aim_ref, d_ref,
               wglu_ref, bglu_ref, y_ref, hre_out, him_out, hre, him, cre, cim, ys, *, nb, tokens):
    step = pl.program_id(0)
    rows = nb * tokens

    @pl.when(step == 0)
    def _():
        cre[...] = h0re_ref[...]
        cim[...] = h0im_ref[...]

    for m in range(4):
        um = u_ref[:, m * LANES:(m + 1) * LANES].astype(BF16)
        hre[:, m * 512:(m + 1) * 512] = jnp.dot(um, wbre_ref[m], preferred_element_type=F32)
        him[:, m * 512:(m + 1) * 512] = jnp.dot(um, wbim_ref[m], preferred_element_type=F32)

    for qd in range(2):
        ql = slice(qd * 1024, (qd + 1) * 1024)
        ar = jnp.broadcast_to(are_ref[:, ql], (nb, 1024))
        ai = jnp.broadcast_to(aim_ref[:, ql], (nb, 1024))

        def tok(t, carry, ql=ql, ar=ar, ai=ai):
            hr, hi = carry
            sel = pl.ds(pl.multiple_of(t * nb, nb), nb)
            nr = ar * hr - ai * hi + hre[sel, ql]
            ni = ar * hi + ai * hr + him[sel, ql]
            hre[sel, ql] = nr
            him[sel, ql] = ni
            return nr, ni

        hr, hi = lax.fori_loop(0, tokens, tok, (cre[:, ql], cim[:, ql]))
        cre[:, ql] = hr
        cim[:, ql] = hi

    for m in range(4):
        hc = jnp.concatenate([hre[:, m * 512:(m + 1) * 512], him[:, m * 512:(m + 1) * 512]], axis=1)
        sl = slice(m * LANES, (m + 1) * LANES)
        y = _mm(hc, wc_ref[m]) + d_ref[:, sl] * u_ref[:, sl]
        ys[:, sl] = jax.nn.gelu(y)
    y = ys[...]
    gate = _sigmoid(_mm(y, wglu_ref[...]) + bglu_ref[...])
    y_ref[...] = (y * gate * _silu(sg_ref[...])).astype(BF16)

    @pl.when(step == pl.num_programs(0) - 1)
    def _():
        hre_out[...] = cre[...]
        him_out[...] = cim[...]


def _s5(u, sg, h0re, h0im, w, nb, tokens):
    t = u.shape[0]
    rows = nb * tokens
    row = lambda c: pl.BlockSpec((rows, c), lambda i: (i, 0))
    weights = (w["wbre"], w["wbim"], w["wc"], w["are"], w["aim"], w["d"], w["wglu"], w["bglu"])
    state = jax.ShapeDtypeStruct((nb, S5_STATE), F32)
    return pl.pallas_call(
        functools.partial(_s5_kernel, nb=nb, tokens=tokens),
        grid=(t // rows,),
        in_specs=[row(S5_WIDTH), row(S5_WIDTH), _const_spec(h0re.shape), _const_spec(h0im.shape)]
        + [_const_spec(a.shape) for a in weights],
        out_specs=(row(S5_WIDTH), _const_spec((nb, S5_STATE)), _const_spec((nb, S5_STATE))),
        out_shape=(jax.ShapeDtypeStruct((t, S5_WIDTH), BF16), state, state),
        scratch_shapes=[
            pltpu.VMEM((rows, S5_STATE), F32),
            pltpu.VMEM((rows, S5_STATE), F32),
            pltpu.VMEM((nb, S5_STATE), F32),
            pltpu.VMEM((nb, S5_STATE), F32),
            pltpu.VMEM((rows, S5_WIDTH), F32),
        ],
        compiler_params=_params(),
        name="s5",
    )(u, sg, h0re, h0im, *weights)


def _dn_solve(lms, rhss):
    nblk = CHUNK // SUB
    row = lax.broadcasted_iota(jnp.int32, (SUB, LANES), 0)
    lane = lax.broadcasted_iota(jnp.int32, (SUB, LANES), 1)
    seg = (lane // SUB) * SUB
    in_diag = [(lane >= SUB * blk) & (lane < SUB * (blk + 1)) for blk in range(nblk)]
    eye_pack = jnp.where((lane - seg == row) & (lane < CHUNK), 1.0, 0.0)
    zpad = jnp.zeros((CHUNK, LANES - CHUNK), F32)
    t_invs, lowers = [], []
    for lm in lms:
        wide = jnp.concatenate([lm, zpad], axis=1)
        tiles = [wide[SUB * blk:SUB * (blk + 1)] for blk in range(nblk)]
        dpack = jnp.zeros((SUB, LANES), F32)
        for blk in range(nblk):
            dpack = jnp.where(in_diag[blk], tiles[blk], dpack)
        inv = eye_pack
        for j in range(SUB - 1):
            col_j = jnp.take_along_axis(dpack, seg + j, axis=1)
            inv = inv - col_j * inv[j:j + 1, :]
        t_invs.append(jnp.concatenate(
            [jnp.where(in_diag[blk], inv, 0.0)[:, :CHUNK] for blk in range(nblk)], axis=0).astype(BF16))
        lowers.append(jnp.concatenate(
            [jnp.where(in_diag[blk], 0.0, tiles[blk])[:, :CHUNK] for blk in range(nblk)], axis=0).astype(BF16))
    d = lambda a, b: jnp.dot(a, b, preferred_element_type=F32)
    his = [rhs.astype(BF16) for rhs in rhss]
    los = [(rhs - hi.astype(F32)).astype(BF16) for rhs, hi in zip(rhss, his)]
    width = rhss[0].shape[1]
    applied = [d(t, jnp.concatenate([hi, lo_, lw], axis=1)) for t, hi, lo_, lw in zip(t_invs, his, los, lowers)]
    trs = [ap[:, :width] + ap[:, width:2 * width] for ap in applied]
    tns = [ap[:, 2 * width:] for ap in applied]
    xs = [[tr[0:SUB]] for tr in trs]
    for blk in range(1, nblk):
        lo = SUB * blk
        curs = [tr[lo:lo + SUB] - _mm(tn[lo:lo + SUB, 0:lo], jnp.concatenate(x, axis=0))
                for tr, tn, x in zip(trs, tns, xs)]
        for x, cur in zip(xs, curs):
            x.append(cur)
    return [jnp.concatenate(x, axis=0) for x in xs]


def _dn_prompt_kernel(x_ref, z_ref, beta_ref, gd_ref, cw_ref, nw_ref, c0_ref,
                      og_ref, sout_ref, cout_ref,
                      xs, qs, ks, vs, bcol, gcol, gc, gt, oscr, *, nb):
    rows = CHUNK * nb
    hist = (CONV_W - 1) * nb
    pad = PAD_TOKENS * nb
    step = pl.program_id(0)
    st = sout_ref

    @pl.when(step == 0)
    def _():
        st[...] = jnp.zeros_like(st)
        xs[0:hist, :] = c0_ref[...]
        gc[0, 0:pad, :] = jnp.zeros((pad, LANES), F32)

    xs[hist:hist + 64, :] = x_ref[0:64, :]

    def conv_rows(src, base, r0):
        tap = lambda i: base + r0 + i * nb if isinstance(r0, int) else pl.multiple_of(base + r0 + i * nb, 8)
        for c in range(DN_CONV_DIM // LANES):
            cl = slice(c * LANES, (c + 1) * LANES)
            y = src[pl.ds(tap(0), 64), cl] * cw_ref[0:1, cl]
            for i in range(1, CONV_W):
                y = y + src[pl.ds(tap(i), 64), cl] * cw_ref[i:i + 1, cl]
            y = _silu(y)
            if c < 2 * DN_HEADS:
                y = y * lax.rsqrt(jnp.sum(y * y, axis=-1, keepdims=True) + EPS)
            if c < DN_HEADS:
                qs[c, pl.ds(r0, 64), :] = y * (DN_DK ** -0.5)
            elif c < 2 * DN_HEADS:
                ks[c - DN_HEADS, pl.ds(r0, 64), :] = y
            else:
                vs[c - 2 * DN_HEADS, pl.ds(r0, 64), :] = y

    gc[0, pad:pad + rows, :] = gd_ref[...]
    _cumsum_tokens(gc, 0, pad, rows, nb)

    def bcast_rows(r0):
        gtile = gc[0, pl.ds(pad + r0, 64), :]
        btile = beta_ref[pl.ds(r0, 64), :]
        for h in range(DN_HEADS):
            gcol[h, pl.ds(r0, 64), :] = jnp.broadcast_to(gtile[:, h:h + 1], (64, LANES))
            bcol[h, pl.ds(r0, 64), :] = jnp.broadcast_to(btile[:, h:h + 1], (64, LANES))

    conv_rows(xs, 0, 0)
    bcast_rows(0)

    def conv_tile(ti, carry):
        r0 = pl.multiple_of(ti * 64, 64)
        conv_rows(x_ref, -hist, r0)
        bcast_rows(r0)
        return carry

    lax.fori_loop(1, rows // 64, conv_tile, 0)

    xs[0:hist, :] = x_ref[rows - hist:rows, :]

    ii = lax.broadcasted_iota(jnp.int32, (CHUNK, CHUNK), 0)
    jj = lax.broadcasted_iota(jnp.int32, (CHUNK, CHUNK), 1)
    eye = (ii == jj).astype(BF16)

    parts = [_split3(gc[0, pl.ds(pad + b, CHUNK, stride=nb), :]) for b in range(nb)]
    tdot = lambda x: lax.dot_general(x, eye, (((0,), (0,)), ((), ())), preferred_element_type=F32)
    firsts = [tdot(p[0]) for p in parts]
    seconds = [tdot(p[1]) for p in parts]
    thirds = [tdot(p[2]) for p in parts]
    for b in range(nb):
        gt[b] = (firsts[b] + seconds[b]) + thirds[b]

    def per_batch(bi, carry):
        def head_group(gi, carry2):
            probs = [(bi * DN_BATCH + jb, gi * DN_GROUP + jh) for jb in range(DN_BATCH) for jh in range(DN_GROUP)]
            sels = [pl.ds(b, CHUNK, stride=nb) for b, _ in probs]
            loaded = [(qs[h, sel, :], ks[h, sel, :], vs[h, sel, :], bcol[h, sel, :], gcol[h, sel, :],
                       gt[b, pl.ds(h, 1), :], st[b, h]) for (b, h), sel in zip(probs, sels)]
            qbs, kbs, vbs, bcs, gcls, grows, s_olds = zip(*loaded)
            decs = [jnp.where(jj <= ii, jnp.exp(jnp.where(jj <= ii, gcl[:, :CHUNK] - grow, 0.0)), 0.0)
                    for gcl, grow in zip(gcls, grows)]
            kqs = [_mm_nt(jnp.concatenate([kb, qb], axis=0), kb) for qb, kb in zip(qbs, kbs)]
            kks = [kq[:CHUNK] for kq in kqs]
            qks = [kq[CHUNK:] for kq in kqs]
            lms = [jnp.where(jj < ii, bc[:, :CHUNK] * kk * dec, 0.0) for bc, kk, dec in zip(bcs, kks, decs)]
            egs = [jnp.exp(gcl) for gcl in gcls]
            rhss = [jnp.concatenate([vb * bc, kb * (bc * eg)], axis=1)
                    for vb, kb, bc, eg in zip(vbs, kbs, bcs, egs)]
            sols = _dn_solve(lms, rhss)
            v_news = [sol[:, :DN_DV] - _mm(sol[:, DN_DV:], s_old) for sol, s_old in zip(sols, s_olds)]
            os_ = [_mm(qb * eg, s_old) + _mm(qk * dec, v_new)
                   for qb, eg, s_old, qk, dec, v_new in zip(qbs, egs, s_olds, qks, decs, v_news)]
            glasts = [gcl[CHUNK - 1:CHUNK, :] for gcl in gcls]
            s_news = [s_old * jnp.exp(glast) + _mm_tn(kb * jnp.exp(glast - gcl), v_new)
                      for s_old, glast, kb, gcl, v_new in zip(s_olds, glasts, kbs, gcls, v_news)]
            for (b, h), sel, o, s_new in zip(probs, sels, os_, s_news):
                st[b, h] = s_new
                oscr[h, sel, :] = o
            return carry2

        lax.fori_loop(0, DN_HEADS // DN_GROUP, head_group, 0)
        return carry

    lax.fori_loop(0, nb // DN_BATCH, per_batch, 0)

    nw = nw_ref[...]

    def epi_tile(ti, carry):
        r0 = pl.multiple_of(ti * 64, 64)
        for h in range(DN_HEADS):
            o = oscr[h, pl.ds(r0, 64), :]
            inv = lax.rsqrt(jnp.mean(o * o, axis=-1, keepdims=True) + EPS)
            cl = slice(h * LANES, (h + 1) * LANES)
            og_ref[pl.ds(r0, 64), cl] = (o * inv * nw * _silu(z_ref[pl.ds(r0, 64), cl])).astype(BF16)
        return carry

    lax.fori_loop(0, rows // 64, epi_tile, 0)

    @pl.when(step == pl.num_programs(0) - 1)
    def _():
        cout_ref[...] = xs[0:hist, :]


def _dn_prompt(xqkv, z, beta, gd, cw, nw, c0, nb):
    t = z.shape[0]
    rows = CHUNK * nb
    hist = (CONV_W - 1) * nb
    row = lambda c: pl.BlockSpec((rows, c), lambda i: (i, 0))
    slab = pltpu.VMEM((DN_HEADS, rows, LANES), F32)
    return pl.pallas_call(
        functools.partial(_dn_prompt_kernel, nb=nb),
        grid=(t // rows,),
        in_specs=[row(DN_CONV_DIM), row(1024), row(LANES), row(LANES), _const_spec(cw.shape),
                  _const_spec(nw.shape), _const_spec(c0.shape)],
        out_specs=(row(1024), _const_spec((nb, DN_HEADS, DN_DK, DN_DV)), _const_spec((hist, DN_CONV_DIM))),
        out_shape=(jax.ShapeDtypeStruct((t, 1024), BF16),
                   jax.ShapeDtypeStruct((nb, DN_HEADS, DN_DK, DN_DV), F32),
                   jax.ShapeDtypeStruct((hist, DN_CONV_DIM), F32)),
        scratch_shapes=[
            pltpu.VMEM((hist + 64, DN_CONV_DIM), F32),
            slab, slab, slab, slab, slab,
            pltpu.VMEM((1, PAD_TOKENS * nb + rows, LANES), F32),
            pltpu.VMEM((nb, LANES, CHUNK), F32),
            slab,
        ],
        compiler_params=_params(),
        name="dn_prompt",
    )(xqkv, z, beta, gd, cw, nw, c0)


def _dn_step_kernel(x_ref, cb_ref, z_ref, beta_ref, gd_ref, s_ref, cw_ref, nw_ref, *rest, layer, first):
    og_ref, sout_ref, cn_ref = rest[-3:]
    sout_ref = _own_state_block(sout_ref, layer, first)
    nw = nw_ref[...]
    cn_ref[0] = cb_ref[1]
    cn_ref[1] = cb_ref[2]
    cn_ref[2] = x_ref[...]
    beta = beta_ref[...]
    eg_all = jnp.exp(gd_ref[...])

    def conv(c):
        cl = slice(c * LANES, (c + 1) * LANES)
        y = cb_ref[0, :, cl] * cw_ref[0:1, cl]
        y = y + cb_ref[1, :, cl] * cw_ref[1:2, cl]
        y = y + cb_ref[2, :, cl] * cw_ref[2:3, cl]
        y = y + x_ref[:, cl] * cw_ref[3:4, cl]
        return _silu(y)

    def l2(y):
        return y * lax.rsqrt(jnp.sum(y * y, axis=-1, keepdims=True) + EPS)

    prow = lax.broadcasted_iota(jnp.int32, (16, 2 * LANES), 0)
    plane = lax.broadcasted_iota(jnp.int32, (16, 2 * LANES), 1)
    pick = [jnp.where(((prow == n) & (plane < LANES)) | ((prow == 8 + n) & (plane >= LANES)), 1.0, 0.0).astype(BF16)
            for n in range(8)]

    for h in range(DN_HEADS):
        qh = l2(conv(h)) * (DN_DK ** -0.5)
        kh = l2(conv(DN_HEADS + h))
        vh = conv(2 * DN_HEADS + h)
        kq = jnp.concatenate([kh, qh], axis=0).astype(BF16)
        cols = [lax.dot_general(kq, pick[n], (((0,), (0,)), ((), ())), preferred_element_type=F32)
                for n in range(8)]
        outs = []
        for n in range(8):
            s_old = s_ref[n, h]
            kcol = cols[n][:, :LANES]
            eg = eg_all[n:n + 1, h:h + 1]
            ks_row = jnp.sum(kcol * s_old, axis=0, keepdims=True)
            v_new = beta[n:n + 1, h:h + 1] * (vh[n:n + 1, :] - eg * ks_row)
            s_new = s_old * eg + kcol * v_new
            sout_ref[n, h] = s_new
            outs.append(jnp.sum(cols[n][:, LANES:] * s_new, axis=0, keepdims=True))
        o = jnp.concatenate(outs, axis=0)
        cl = slice(h * LANES, (h + 1) * LANES)
        og_ref[:, cl] = o * lax.rsqrt(jnp.mean(o * o, axis=-1, keepdims=True) + EPS) * nw * _silu(z_ref[:, cl])


def _dn_step(xqkv, cbuf, z, beta, gd, states, layer, prev, cw, nw):
    n = z.shape[0]
    row = lambda c: pl.BlockSpec((8, c), lambda i: (i, 0))
    cspec = pl.BlockSpec((CONV_W - 1, 8, DN_CONV_DIM), lambda i: (0, i, 0))
    sspec, ospec, extra_specs, extra_args, aliases = _stacked_state_args(states, layer, prev, 8, 1)
    return pl.pallas_call(
        functools.partial(_dn_step_kernel, layer=layer, first=prev is None),
        grid=(n // 8,),
        in_specs=[row(DN_CONV_DIM), cspec, row(1024), row(LANES), row(LANES), sspec,
                  _const_spec(cw.shape), _const_spec(nw.shape)] + extra_specs,
        out_specs=(row(1024), ospec, cspec),
        out_shape=(jax.ShapeDtypeStruct((n, 1024), F32), jax.ShapeDtypeStruct(states.shape, F32),
                   jax.ShapeDtypeStruct(cbuf.shape, F32)),
        input_output_aliases=aliases,
        compiler_params=_params(),
        name="dn_step",
    )(xqkv, cbuf, z, beta, gd, states, cw, nw, *extra_args)


def _pad_cols(a, n):
    return jnp.pad(a, ((0, 0), (0, n - a.shape[1])))


def _even_weights(w_in, w_gate_up, b_gate, w_out):
    return {
        "w": _pad_cols(w_in.astype(BF16), W_IN_PAD),
        "wg": jnp.pad(w_gate_up, ((0, LANES - GLA_RANK), (0, 0))).astype(BF16),
        "bg": b_gate.reshape(1, -1),
        "wo": w_out.astype(BF16),
    }


def _s5_weights(ar, ai, bbre, bbim, c_re, c_im, d, w_glu, b_glu):
    eye = jnp.eye(8, dtype=F32)

    def pack_b(bb):
        t = bb.reshape(4, 8, S5_P, S5_GROUP_CH).transpose(0, 1, 3, 2)
        return (t[:, :, :, None, :] * eye[None, :, None, :, None]).reshape(4, 128, 512).astype(BF16)

    def pack_c(c):
        t = c.reshape(4, 8, S5_GROUP_CH, S5_P).transpose(0, 1, 3, 2)
        return (t[:, :, :, None, :] * eye[None, :, None, :, None]).reshape(4, 512, 128)

    return {
        "wbre": pack_b(bbre), "wbim": pack_b(bbim),
        "wc": jnp.concatenate([pack_c(c_re), -pack_c(c_im)], axis=1).astype(BF16),
        "are": ar.reshape(1, S5_STATE), "aim": ai.reshape(1, S5_STATE),
        "d": d.reshape(1, -1), "wglu": w_glu.astype(BF16), "bglu": b_glu.reshape(1, -1),
    }


def _odd_weights(w_in, conv_w, a_log, dt_bias, w_out):
    return {
        "w": _pad_cols(w_in.astype(BF16), W_IN_PAD),
        "alog": _pad_cols(a_log.reshape(1, -1), LANES), "dtb": _pad_cols(dt_bias.reshape(1, -1), LANES),
        "cw": jnp.pad(conv_w, ((0, 8 - CONV_W), (0, 0))),
        "wo": w_out.astype(BF16),
    }


def kernel(x_prompt, x_sample, state_gla, state_s5_re, state_s5_im, state_delta, state_conv, norm_w, final_norm_w, w_in_even, gla_w_gate_up, gla_b_gate, gla_norm_w, s5_lambda_re, s5_lambda_im, s5_log_dt, s5_b_re, s5_b_im, s5_c_re, s5_c_im, s5_d, s5_w_glu, s5_b_glu, w_out_even, w_in_odd, dn_conv_w, dn_a_log, dn_dt_bias, dn_norm_w, w_out_odd):
    nb, seq, _ = x_prompt.shape
    ns = x_sample.shape[0]
    assert seq % CHUNK == 0 and nb % 8 == 0 and ns % 8 == 0 and x_sample.shape[1] == 1
    depth = norm_w.shape[0]

    assert depth % 2 == 0 and state_gla.shape[0] == depth // 2 and state_delta.shape[0] == depth // 2
    xp = None
    xs = x_sample.reshape(ns, D_MODEL)
    fw = final_norm_w.reshape(1, -1)

    ar, ai, bbre, bbim = _s5_prep(s5_lambda_re, s5_lambda_im, s5_log_dt, s5_b_re, s5_b_im)

    gla_p, s5r_p, s5i_p, s5r_s, s5i_s, dn_p, cv_p, cv_s = ([] for _ in range(8))
    gla_s = dn_s = None
    for layer in range(depth):
        i = layer // 2
        nw = norm_w[layer].reshape(1, -1)
        if layer % 2 == 0:
            w = _even_weights(w_in_even[i], gla_w_gate_up[i], gla_b_gate[i], w_out_even[i])
            w5 = _s5_weights(ar[i], ai[i], bbre[i], bbim[i], s5_c_re[i], s5_c_im[i], s5_d[i],
                             s5_w_glu[i], s5_b_glu[i])
            gnw = gla_norm_w[i].reshape(1, -1)
            if layer == 0:
                q, k, g, v, r, u, sg, xp = _proj_even(x_prompt, nw, w, batch_major=True)
            else:
                q, k, g, v, r, u, sg, xp = _proj_even(xp, nw, w, pending=pending)
            og, s_new = _gla_prompt(q, k, g, v, r, gnw, nb)
            zero = jnp.zeros((nb, S5_STATE), F32)
            y5, hre, him = _s5(u, sg, zero, zero, w5, nb, CHUNK)
            pending = (og, y5, w["wo"])
            gla_p.append(s_new)
            s5r_p.append(hre.reshape(nb, S5_GROUPS, S5_P))
            s5i_p.append(him.reshape(nb, S5_GROUPS, S5_P))
            q, k, g, v, r, u, sg = _proj_even(xs, nw, w)
            og, gla_s = _gla_step(q, k, g, v, r, state_gla, i, gla_s, gnw)
            y5, hre, him = _s5(u, sg, state_s5_re[i].reshape(ns, S5_STATE), state_s5_im[i].reshape(ns, S5_STATE),
                               w5, ns, 1)
            xs = _out_even(xs, og, y5, w["wo"])
            s5r_s.append(hre.reshape(ns, S5_GROUPS, S5_P))
            s5i_s.append(him.reshape(ns, S5_GROUPS, S5_P))
        else:
            w = _odd_weights(w_in_odd[i], dn_conv_w[i], dn_a_log[i], dn_dt_bias[i], w_out_odd[i])
            dnw = dn_norm_w[i].reshape(1, -1)
            final = layer == depth - 1
            xqkv, z, beta, gd, xp = _proj_odd(xp, nw, w, pending=pending)
            c0 = jnp.zeros(((CONV_W - 1) * nb, DN_CONV_DIM), F32)
            og, s_new, c_new = _dn_prompt(xqkv, z, beta, gd, w["cw"], dnw, c0, nb)
            if final:
                xp = _out_odd(xp, og, w["wo"], fw, final, nb_out=nb)
            else:
                pending = (og, w["wo"])
            dn_p.append(s_new)
            cv_p.append(c_new.reshape(CONV_W - 1, nb, DN_CONV_DIM).transpose(1, 0, 2))
            xqkv, z, beta, gd = _proj_odd(xs, nw, w)
            og, dn_s, c_new = _dn_step(xqkv, state_conv[i].transpose(1, 0, 2), z, beta, gd, state_delta, i, dn_s,
                                       w["cw"], dnw)
            xs = _out_odd(xs, og, w["wo"], fw, final)
            cv_s.append(c_new.transpose(1, 0, 2))

    y_prompt = xp
    y_sample = xs.reshape(ns, 1, D_MODEL)
    st = jnp.stack
    return (y_prompt, y_sample, st(gla_p), gla_s, st(s5r_p), st(s5i_p), st(s5r_s), st(s5i_s),
            st(dn_p), dn_s, st(cv_p), st(cv_s))
```

```python
import functools
import math

import jax
import jax.numpy as jnp
from jax import lax
from jax.experimental import pallas as pl
from jax.experimental.pallas import tpu as pltpu

F32 = jnp.float32
BF16 = jnp.bfloat16
EPS = 1e-6

D_MODEL = 1024
GLA_HEADS, GLA_DK, GLA_DV, GLA_RANK = 4, 128, 256, 16
GLA_GATE_NORM = 16.0
S5_GROUPS, S5_GROUP_CH, S5_P, S5_WIDTH = 32, 16, 64, 512
S5_STATE = S5_GROUPS * S5_P
DN_HEADS, DN_DK, DN_DV = 8, 128, 128
DN_CONV_DIM, CONV_W = 3072, 4
CHUNK = 64
SUB = 8
GLA_BAND = 8
GLA_BATCH = 4
GLA_SAFE_SPAN = 60.0
PAD_TOKENS = 16
DN_BATCH = 8
DN_GROUP = 8
LANES = 128
W_IN_PAD = 4224
VMEM_LIMIT = 56 * 1024 * 1024


def _mm(a, b):
    return jnp.dot(a.astype(BF16), b.astype(BF16), preferred_element_type=F32)


def _mm_nt(a, b):
    return lax.dot_general(a.astype(BF16), b.astype(BF16), (((1,), (1,)), ((), ())),
                           preferred_element_type=F32)


def _mm_tn(a, b):
    return lax.dot_general(a.astype(BF16), b.astype(BF16), (((0,), (0,)), ((), ())),
                           preferred_element_type=F32)


def _split3(a):
    a1 = a.astype(BF16)
    r1 = a - a1.astype(F32)
    a2 = r1.astype(BF16)
    a3 = (r1 - a2.astype(F32)).astype(BF16)
    return a1, a2, a3


def _mm_tn_exact(a, m_bf16):
    a1, a2, a3 = _split3(a)
    d = lambda x: lax.dot_general(x, m_bf16, (((0,), (0,)), ((), ())), preferred_element_type=F32)
    return (d(a1) + d(a2)) + d(a3)


def _sigmoid(x):
    return 1.0 / (1.0 + jnp.exp(-x))


def _silu(x):
    half = 0.5 * x
    return half + half * jnp.tanh(half)


def _softplus(x):
    return jnp.maximum(x, 0.0) + jnp.log1p(jnp.exp(-jnp.abs(x)))


def _rms_rows(x, w):
    ms = jnp.mean(x * x, axis=-1, keepdims=True)
    return x * lax.rsqrt(ms + EPS) * w


def _const_spec(shape):
    nd = len(shape)
    return pl.BlockSpec(shape, lambda i, _nd=nd: (0,) * _nd)


def _params(**flags):
    return pltpu.CompilerParams(dimension_semantics=("arbitrary",), vmem_limit_bytes=VMEM_LIMIT,
                                flags=flags or None)


def _row_tile(t):
    return min(512, t)


def _add_pending(x, pending):
    if len(pending) == 2:
        og_ref, wo_ref = pending
        return x + _mm(og_ref[...], wo_ref[...])
    og_ref, y5_ref, wo_ref = pending
    return x + (_mm(og_ref[...], wo_ref[0:1024, :]) + _mm(y5_ref[...], wo_ref[1024:, :]))


def _proj_even_kernel(x_ref, nw_ref, w_ref, wg_ref, bg_ref, *rest, nb_in, n_pending):
    pending = rest[:n_pending]
    q_ref, k_ref, g_ref, v_ref, r_ref, u_ref, sg_ref = rest[n_pending:n_pending + 7]
    tail = rest[n_pending + 7:]
    if nb_in:
        xo_ref, xt = tail
        for b in range(nb_in):
            for c in range(D_MODEL // LANES):
                xt[c, pl.ds(b, x_ref.shape[1], stride=nb_in), :] = x_ref[b, :, c * LANES:(c + 1) * LANES]
        x = jnp.concatenate([xt[c] for c in range(D_MODEL // LANES)], axis=1)
        xo_ref[...] = x
    elif n_pending:
        xo_ref, = tail
        x = _add_pending(x_ref[...], pending)
        xo_ref[...] = x
    else:
        x = x_ref[...]
    hb = _rms_rows(x, nw_ref[...]).astype(BF16)
    d = lambda lo, hi: jnp.dot(hb, w_ref[:, lo:hi], preferred_element_type=F32)
    q = d(0, 512) * (GLA_DK ** -0.5)
    k = d(512, 1024)
    tail = d(3072, W_IN_PAD)
    logit = jnp.dot(tail[:, :LANES].astype(BF16), wg_ref[...], preferred_element_type=F32) + bg_ref[...]
    g = -_softplus(-logit) / GLA_GATE_NORM
    for h in range(GLA_HEADS):
        sl = slice(h * LANES, (h + 1) * LANES)
        q_ref[h] = q[:, sl]
        k_ref[h] = k[:, sl]
        g_ref[h] = g[:, sl]
    v = d(1024, 2048)
    for s in range(2 * GLA_HEADS):
        v_ref[s] = v[:, s * LANES:(s + 1) * LANES]
    r_ref[...] = d(2048, 3072)
    u_ref[...] = tail[:, GLA_RANK:GLA_RANK + S5_WIDTH]
    sg_ref[...] = tail[:, GLA_RANK + S5_WIDTH:GLA_RANK + 2 * S5_WIDTH]


def _pending_specs(pending, tm):
    *acts, wo = pending
    return [pl.BlockSpec((tm, a.shape[1]), lambda i: (i, 0)) for a in acts] + [_const_spec(wo.shape)]


def _proj_even(x, nw, w, batch_major=False, pending=()):
    nb_in = x.shape[0] if batch_major else 0
    t = x.shape[0] * x.shape[1] if batch_major else x.shape[0]
    tm = _row_tile(t)
    row = lambda c: pl.BlockSpec((tm, c), lambda i: (i, 0))
    slab = lambda n: pl.BlockSpec((n, tm, LANES), lambda i: (0, i, 0))
    x_spec = pl.BlockSpec((nb_in, tm // nb_in, D_MODEL), lambda i: (0, i, 0)) if batch_major else row(D_MODEL)
    new_x = batch_major or bool(pending)
    extra_out_shape = (jax.ShapeDtypeStruct((t, D_MODEL), F32),) if new_x else ()
    extra_out_specs = (row(D_MODEL),) if new_x else ()
    scratch = [pltpu.VMEM((D_MODEL // LANES, tm, LANES), F32)] if batch_major else []
    pend_specs = _pending_specs(pending, tm) if pending else []
    out_shape = (
        jax.ShapeDtypeStruct((GLA_HEADS, t, LANES), F32),
        jax.ShapeDtypeStruct((GLA_HEADS, t, LANES), F32),
        jax.ShapeDtypeStruct((GLA_HEADS, t, LANES), F32),
        jax.ShapeDtypeStruct((2 * GLA_HEADS, t, LANES), F32),
        jax.ShapeDtypeStruct((t, 1024), F32),
        jax.ShapeDtypeStruct((t, S5_WIDTH), F32),
        jax.ShapeDtypeStruct((t, S5_WIDTH), F32),
    )
    weights = (w["w"], w["wg"], w["bg"])
    return pl.pallas_call(
        functools.partial(_proj_even_kernel, nb_in=nb_in, n_pending=len(pending)),
        grid=(t // tm,),
        in_specs=[x_spec, _const_spec(nw.shape)] + [_const_spec(a.shape) for a in weights] + pend_specs,
        out_specs=(slab(4), slab(4), slab(4), slab(8), row(1024), row(S5_WIDTH), row(S5_WIDTH)) + extra_out_specs,
        out_shape=out_shape + extra_out_shape,
        scratch_shapes=scratch,
        compiler_params=_params(),
        name="proj_even",
    )(x, nw, *weights, *pending)


def _proj_odd_kernel(x_ref, nw_ref, w_ref, alog_ref, dtb_ref, *rest, n_pending):
    pending = rest[:n_pending]
    xqkv_ref, z_ref, beta_ref, gd_ref = rest[n_pending:n_pending + 4]
    x = x_ref[...]
    if n_pending:
        xo_ref, = rest[n_pending + 4:]
        x = _add_pending(x, pending)
        xo_ref[...] = x
    hb = _rms_rows(x, nw_ref[...]).astype(BF16)
    d = lambda lo, hi: jnp.dot(hb, w_ref[:, lo:hi], preferred_element_type=F32)
    xqkv_ref[...] = d(0, DN_CONV_DIM)
    z_ref[...] = d(DN_CONV_DIM, DN_CONV_DIM + 1024)
    tail = d(DN_CONV_DIM + 1024, DN_CONV_DIM + 1024 + LANES)
    beta_ref[...] = _sigmoid(tail)
    a = pltpu.roll(tail, LANES - DN_HEADS, 1)
    gd_ref[...] = -jnp.exp(alog_ref[...]) * _softplus(a + dtb_ref[...])


def _proj_odd(x, nw, w, pending=()):
    t = x.shape[0]
    tm = _row_tile(t)
    row = lambda c: pl.BlockSpec((tm, c), lambda i: (i, 0))
    weights = (w["w"], w["alog"], w["dtb"])
    pend_specs = _pending_specs(pending, tm) if pending else []
    extra_out_shape = (jax.ShapeDtypeStruct((t, D_MODEL), F32),) if pending else ()
    extra_out_specs = (row(D_MODEL),) if pending else ()
    return pl.pallas_call(
        functools.partial(_proj_odd_kernel, n_pending=len(pending)),
        grid=(t // tm,),
        in_specs=[row(D_MODEL), _const_spec(nw.shape)] + [_const_spec(a.shape) for a in weights] + pend_specs,
        out_specs=(row(DN_CONV_DIM), row(1024), row(LANES), row(LANES)) + extra_out_specs,
        out_shape=(
            jax.ShapeDtypeStruct((t, DN_CONV_DIM), F32),
            jax.ShapeDtypeStruct((t, 1024), F32),
            jax.ShapeDtypeStruct((t, LANES), F32),
            jax.ShapeDtypeStruct((t, LANES), F32),
        ) + extra_out_shape,
        compiler_params=_params(),
        name="proj_odd",
    )(x, nw, *weights, *pending)


def _out_even_kernel(x_ref, og_ref, y5_ref, w_ref, o_ref):
    o_ref[...] = x_ref[...] + (_mm(og_ref[...], w_ref[0:1024, :]) + _mm(y5_ref[...], w_ref[1024:, :]))


def _out_even(x, og, y5, w):
    t = x.shape[0]
    tm = _row_tile(t)
    row = lambda c: pl.BlockSpec((tm, c), lambda i: (i, 0))
    return pl.pallas_call(
        _out_even_kernel,
        grid=(t // tm,),
        in_specs=[row(D_MODEL), row(1024), row(S5_WIDTH), _const_spec(w.shape)],
        out_specs=row(D_MODEL),
        out_shape=jax.ShapeDtypeStruct((t, D_MODEL), F32),
        compiler_params=_params(),
        name="out_even",
    )(x, og, y5, w)


def _out_odd_kernel(x_ref, og_ref, w_ref, fw_ref, o_ref, *slabs, final, nb_out):
    y = x_ref[...] + _mm(og_ref[...], w_ref[...])
    if final:
        y = _rms_rows(y, fw_ref[...])
    if nb_out:
        ys, = slabs
        for c in range(D_MODEL // LANES):
            ys[c] = y[:, c * LANES:(c + 1) * LANES]
        for b in range(nb_out):
            for c in range(D_MODEL // LANES):
                o_ref[b, :, c * LANES:(c + 1) * LANES] = ys[c, pl.ds(b, o_ref.shape[1], stride=nb_out), :]
    else:
        o_ref[...] = y


def _out_odd(x, og, w, fw, final, nb_out=0):
    t = x.shape[0]
    tm = _row_tile(t)
    row = lambda c: pl.BlockSpec((tm, c), lambda i: (i, 0))
    if nb_out:
        out_spec = pl.BlockSpec((nb_out, tm // nb_out, D_MODEL), lambda i: (0, i, 0))
        out_shape = jax.ShapeDtypeStruct((nb_out, t // nb_out, D_MODEL), F32)
        scratch = [pltpu.VMEM((D_MODEL // LANES, tm, LANES), F32)]
    else:
        out_spec, out_shape, scratch = row(D_MODEL), jax.ShapeDtypeStruct((t, D_MODEL), F32), []
    return pl.pallas_call(
        functools.partial(_out_odd_kernel, final=final, nb_out=nb_out),
        grid=(t // tm,),
        in_specs=[row(D_MODEL), row(1024), _const_spec(w.shape), _const_spec(fw.shape)],
        out_specs=out_spec,
        out_shape=out_shape,
        scratch_shapes=scratch,
        compiler_params=_params(),
        name="out_odd",
    )(x, og, w, fw)


def _cumsum_tokens(ref, lead, pad, rows, nb):
    shift = nb
    while shift <= pad:
        ref[lead, pad:pad + rows, :] = ref[lead, pad:pad + rows, :] + ref[lead, pad - shift:pad + rows - shift, :]
        shift *= 2
    while shift < rows:
        ref[lead, pad + shift:pad + rows, :] = (ref[lead, pad + shift:pad + rows, :]
                                                + ref[lead, pad:pad + rows - shift, :])
        shift *= 2


def _gla_prompt_kernel(q_ref, k_ref, g_ref, v_ref, r_ref, far_ref, near_ref, nw_ref, og_ref, sout_ref,
                       kp, bp, vp, st, oscr, ocar, *, nb):
    rows = CHUNK * nb
    pad = PAD_TOKENS * nb
    blk_rows = GLA_BAND * nb
    step = pl.program_id(0)

    @pl.when(step == 0)
    def _():
        st[...] = jnp.zeros_like(st)
        kp[:, 0:pad, :] = jnp.zeros((GLA_HEADS, pad, LANES), F32)
        bp[:, 0:pad, :] = jnp.zeros((GLA_HEADS, pad, LANES), F32)
        vp[:, 0:pad, :] = jnp.zeros((2 * GLA_HEADS, pad, LANES), F32)

    for h in range(GLA_HEADS):
        bp[h, pad:pad + rows, :] = g_ref[h]
        kp[h, pad:pad + rows, :] = k_ref[h]
        _cumsum_tokens(bp, h, pad, rows, nb)
    for s in range(2 * GLA_HEADS):
        vp[s, pad:pad + rows, :] = v_ref[s]

    nblk = CHUNK // GLA_BAND

    span = jnp.zeros((nb, LANES), F32)
    for h in range(GLA_HEADS):
        for blk in range(nblk):
            lo = pad + blk * blk_rows
            span = jnp.maximum(span, bp[h, lo - nb:lo, :] - bp[h, lo + blk_rows - nb:lo + blk_rows, :])
    safe = jnp.max(span) <= GLA_SAFE_SPAN

    def pairs_on_mxu(first_blk, mask_ref, accumulate):
        for h in range(GLA_HEADS):
            probs = []
            for blk in range(first_blk, nblk):
                lo = blk * blk_rows
                hi = lo + (1 - first_blk) * blk_rows
                ref_b = bp[h, pad + lo - nb:pad + lo, :]
                qs = q_ref[h, lo:lo + blk_rows, :] * jnp.exp(
                    bp[h, pad + lo:pad + lo + blk_rows, :] - jnp.concatenate([ref_b] * GLA_BAND, axis=0))
                ks = kp[h, pad:pad + hi, :] * jnp.exp(
                    jnp.concatenate([ref_b] * (hi // nb), axis=0) - bp[h, pad:pad + hi, :])
                probs.append((hi, _mm_nt(qs, ks) * mask_ref[lo:lo + blk_rows, 0:hi]))
            for blk in range(first_blk, nblk):
                lo = blk * blk_rows
                hi, p = probs[blk - first_blk]
                p = p.astype(BF16)
                for half in range(2):
                    sl = 2 * h + half
                    pv = jnp.dot(p, vp[sl, pad:pad + hi, :].astype(BF16), preferred_element_type=F32)
                    oscr[sl, lo:lo + blk_rows, :] = oscr[sl, lo:lo + blk_rows, :] + pv if accumulate else pv

    self_pair = jnp.where(safe, 1.0, 0.0)

    @pl.when(safe)
    def _():
        pairs_on_mxu(0, near_ref, False)

    def band_tile(ti, carry):
        r0 = pl.multiple_of(ti * 64, 64)
        for h in range(GLA_HEADS):
            qt = q_ref[h, pl.ds(r0, 64), :]
            bt = bp[h, pl.ds(pad + r0, 64), :]
            acc0 = jnp.zeros((64, LANES), F32)
            acc1 = jnp.zeros((64, LANES), F32)
            for d in range(GLA_BAND):
                off = pl.multiple_of(pad + r0 - d * nb, 8)
                ks = kp[h, pl.ds(off, 64), :]
                bs = bp[h, pl.ds(off, 64), :]
                w = jnp.sum(qt * ks * jnp.exp(bt - bs), axis=-1, keepdims=True)
                acc0 = acc0 + w * vp[2 * h, pl.ds(off, 64), :]
                acc1 = acc1 + w * vp[2 * h + 1, pl.ds(off, 64), :]
            oscr[2 * h, pl.ds(r0, 64), :] = acc0
            oscr[2 * h + 1, pl.ds(r0, 64), :] = acc1
        return carry

    @pl.when(jnp.logical_not(safe))
    def _():
        lax.fori_loop(0, rows // 64, band_tile, 0)
        pairs_on_mxu(1, far_ref, True)

    def per_batch(bi, carry):
        probs = [(bi * GLA_BATCH + jb, h) for jb in range(GLA_BATCH) for h in range(GLA_HEADS)]
        sels = [pl.ds(b, CHUNK, stride=nb) for b, _ in probs]
        loaded = [(q_ref[h, sel, :], kp[h, pl.ds(pad + b, CHUNK, stride=nb), :],
                   bp[h, pl.ds(pad + b, CHUNK, stride=nb), :], vp[2 * h, pl.ds(pad + b, CHUNK, stride=nb), :],
                   vp[2 * h + 1, pl.ds(pad + b, CHUNK, stride=nb), :], st[b, h])
                  for (b, h), sel in zip(probs, sels)]
        outs = [_mm_nt(qb * jnp.exp(bb), stt) for qb, _, bb, _, _, stt in loaded]
        news = []
        for _, kb, bb, v0, v1, stt in loaded:
            blast = bb[CHUNK - 1:CHUNK, :]
            kd = kb * jnp.exp(blast - bb)
            news.append(stt * jnp.exp(blast) + _mm_tn(jnp.concatenate([v0, v1], axis=1), kd))
        for (b, h), sel, out, new in zip(probs, sels, outs, news):
            ocar[2 * h, sel, :] = out[:, :LANES]
            ocar[2 * h + 1, sel, :] = out[:, LANES:]
            st[b, h] = new
        return carry

    lax.fori_loop(0, nb // GLA_BATCH, per_batch, 0)

    nw = nw_ref[...]

    def epi_tile(ti, carry):
        r0 = pl.multiple_of(ti * 64, 64)
        for h in range(GLA_HEADS):
            w = self_pair * jnp.sum(q_ref[h, pl.ds(r0, 64), :] * k_ref[h, pl.ds(r0, 64), :], axis=-1, keepdims=True)
            o0 = (oscr[2 * h, pl.ds(r0, 64), :] + ocar[2 * h, pl.ds(r0, 64), :]
                  + w * v_ref[2 * h, pl.ds(r0, 64), :])
            o1 = (oscr[2 * h + 1, pl.ds(r0, 64), :] + ocar[2 * h + 1, pl.ds(r0, 64), :]
                  + w * v_ref[2 * h + 1, pl.ds(r0, 64), :])
            ms = (jnp.sum(o0 * o0, axis=-1, keepdims=True) + jnp.sum(o1 * o1, axis=-1, keepdims=True)) / GLA_DV
            inv = lax.rsqrt(ms + EPS)
            c0 = h * GLA_DV
            og_ref[pl.ds(r0, 64), c0:c0 + LANES] = (
                o0 * inv * nw[:, :LANES] * _silu(r_ref[pl.ds(r0, 64), c0:c0 + LANES])).astype(BF16)
            og_ref[pl.ds(r0, 64), c0 + LANES:c0 + 2 * LANES] = (
                o1 * inv * nw[:, LANES:] * _silu(r_ref[pl.ds(r0, 64), c0 + LANES:c0 + 2 * LANES])).astype(BF16)
        return carry

    lax.fori_loop(0, rows // 64, epi_tile, 0)

    @pl.when(step == pl.num_programs(0) - 1)
    def _():
        def wr(b, carry):
            for h in range(GLA_HEADS):
                sout_ref[b, h] = st[b, h].T
            return carry
        lax.fori_loop(0, nb, wr, 0)


def _pair_mask(nb, min_dist):
    n = CHUNK * nb
    r = jnp.arange(n)
    same = (r[:, None] % nb) == (r[None, :] % nb)
    far = (r[None, :] // nb) <= (r[:, None] // nb) - min_dist
    return (same & far).astype(F32)


def _gla_prompt(q, k, g, v, r, nw, nb):
    t = r.shape[0]
    rows = CHUNK * nb
    pad = PAD_TOKENS * nb
    slab = lambda n: pl.BlockSpec((n, rows, LANES), lambda i: (0, i, 0))
    row = lambda c: pl.BlockSpec((rows, c), lambda i: (i, 0))
    far = _pair_mask(nb, GLA_BAND)
    near = _pair_mask(nb, 1)
    return pl.pallas_call(
        functools.partial(_gla_prompt_kernel, nb=nb),
        grid=(t // rows,),
        in_specs=[slab(4), slab(4), slab(4), slab(8), row(1024), _const_spec(far.shape), _const_spec(near.shape),
                  _const_spec(nw.shape)],
        out_specs=(row(1024), _const_spec((nb, GLA_HEADS, GLA_DK, GLA_DV))),
        out_shape=(jax.ShapeDtypeStruct((t, 1024), BF16),
                   jax.ShapeDtypeStruct((nb, GLA_HEADS, GLA_DK, GLA_DV), F32)),
        scratch_shapes=[
            pltpu.VMEM((GLA_HEADS, pad + rows, LANES), F32),
            pltpu.VMEM((GLA_HEADS, pad + rows, LANES), F32),
            pltpu.VMEM((2 * GLA_HEADS, pad + rows, LANES), F32),
            pltpu.VMEM((nb, GLA_HEADS, GLA_DV, GLA_DK), F32),
            pltpu.VMEM((2 * GLA_HEADS, rows, LANES), F32),
            pltpu.VMEM((2 * GLA_HEADS, rows, LANES), F32),
        ],
        compiler_params=_params(),
        name="gla_prompt",
    )(q, k, g, v, r, far, near, nw)


def _gla_step_kernel(q_ref, k_ref, g_ref, v_ref, r_ref, s_ref, nw_ref, *rest, layer, first):
    og_ref, sout_ref = rest[-2:]
    sout_ref = _own_state_block(sout_ref, layer, first)
    nw = nw_ref[...]
    zeros = jnp.zeros((LANES - 8, LANES), F32)
    prow = lax.broadcasted_iota(jnp.int32, (16, 2 * GLA_DV), 0)
    plane = lax.broadcasted_iota(jnp.int32, (16, 2 * GLA_DV), 1)
    pick = [jnp.where(((prow == n) & (plane < GLA_DV)) | ((prow == 8 + n) & (plane >= GLA_DV)), 1.0, 0.0).astype(BF16)
            for n in range(8)]
    for h in range(GLA_HEADS):
        decay = jnp.concatenate([jnp.exp(g_ref[h]), zeros], axis=0).T
        kq = jnp.concatenate([k_ref[h], q_ref[h]], axis=0).astype(BF16)
        cols = [lax.dot_general(kq, pick[n], (((0,), (0,)), ((), ())), preferred_element_type=F32)
                for n in range(8)]
        outs = []
        for n in range(8):
            s_old = s_ref[n, h]
            vrow = jnp.concatenate([v_ref[2 * h, n:n + 1, :], v_ref[2 * h + 1, n:n + 1, :]], axis=1)
            s_new = s_old * decay[:, n:n + 1] + cols[n][:, :GLA_DV] * vrow
            sout_ref[n, h] = s_new
            outs.append(jnp.sum(cols[n][:, GLA_DV:] * s_new, axis=0, keepdims=True))
        o = jnp.concatenate(outs, axis=0)
        ms = jnp.mean(o * o, axis=-1, keepdims=True)
        c0 = h * GLA_DV
        og_ref[:, c0:c0 + GLA_DV] = o * lax.rsqrt(ms + EPS) * nw * _silu(r_ref[:, c0:c0 + GLA_DV])


def _stacked_state_args(states, layer, prev, n_inputs, out_index):
    tail = states.shape[2:]
    zeros = (0,) * len(tail)
    spec = pl.BlockSpec((None, 8) + tail, lambda i, _l=layer: (_l, i) + zeros)
    if prev is None:
        return spec, pl.BlockSpec((states.shape[0], 8) + tail, lambda i: (0, i) + zeros), [], [], {}
    return spec, spec, [pl.BlockSpec(memory_space=pl.ANY)], [prev], {n_inputs: out_index}


def _own_state_block(sout_ref, layer, first):
    if not first:
        return sout_ref
    for other in range(sout_ref.shape[0]):
        if other != layer:
            sout_ref[other] = jnp.zeros(sout_ref.shape[1:], F32)
    return sout_ref.at[layer]


def _gla_step(q, k, g, v, r, states, layer, prev, nw):
    n = r.shape[0]
    slab = lambda c: pl.BlockSpec((c, 8, LANES), lambda i: (0, i, 0))
    sspec, ospec, extra_specs, extra_args, aliases = _stacked_state_args(states, layer, prev, 7, 1)
    return pl.pallas_call(
        functools.partial(_gla_step_kernel, layer=layer, first=prev is None),
        grid=(n // 8,),
        in_specs=[slab(4), slab(4), slab(4), slab(8), pl.BlockSpec((8, 1024), lambda i: (i, 0)), sspec,
                  _const_spec(nw.shape)] + extra_specs,
        out_specs=(pl.BlockSpec((8, 1024), lambda i: (i, 0)), ospec),
        out_shape=(jax.ShapeDtypeStruct((n, 1024), F32), jax.ShapeDtypeStruct(states.shape, F32)),
        input_output_aliases=aliases,
        compiler_params=_params(),
        name="gla_step",
    )(q, k, g, v, r, states, nw, *extra_args)


def _s5_prep_kernel(lre_ref, lim_ref, ldt_ref, lre_r_ref, lim_r_ref, ldt_r_ref, bre_ref, bim_ref,
                    ar_ref, ai_ref, bbre_ref, bbim_ref):
    def disc(lre, lim, ldt):
        dt = jnp.exp(ldt)
        mag = jnp.exp(lre * dt)
        ar = mag * jnp.cos(lim * dt)
        ai = mag * jnp.sin(lim * dt)
        return ar, ai

    ar, ai = disc(lre_ref[...], lim_ref[...], ldt_ref[...])
    ar_ref[...] = ar
    ai_ref[...] = ai
    lre, lim = lre_r_ref[...], lim_r_ref[...]
    ar, ai = disc(lre, lim, ldt_r_ref[...])
    den = lre * lre + lim * lim
    wr = ((ar - 1.0) * lre + ai * lim) / den
    wi = (ai * lre - (ar - 1.0) * lim) / den
    bre, bim = bre_ref[...], bim_ref[...]
    bbre_ref[...] = wr * bre - wi * bim
    bbim_ref[...] = wr * bim + wi * bre


def _s5_prep(lam_re, lam_im, log_dt, b_re, b_im):
    n = lam_re.shape[0]
    rows = n * S5_GROUPS
    lre = lam_re.reshape(rows, S5_P)
    lim = lam_im.reshape(rows, S5_P)
    ldt = jnp.broadcast_to(log_dt.reshape(rows, 1), (rows, S5_P))
    rep = lambda a: jnp.repeat(a, S5_GROUP_CH, axis=1)
    args = (lre, lim, ldt, rep(lre), rep(lim), rep(ldt),
            b_re.reshape(rows, S5_P * S5_GROUP_CH), b_im.reshape(rows, S5_P * S5_GROUP_CH))
    wide = jax.ShapeDtypeStruct((rows, S5_P * S5_GROUP_CH), F32)
    narrow = jax.ShapeDtypeStruct((rows, S5_P), F32)
    ar, ai, bbre, bbim = pl.pallas_call(
        _s5_prep_kernel, out_shape=(narrow, narrow, wide, wide), name="s5_prep")(*args)
    shp = (n, S5_GROUPS, S5_P, S5_GROUP_CH)
    return ar.reshape(n, S5_GROUPS, S5_P), ai.reshape(n, S5_GROUPS, S5_P), bbre.reshape(shp), bbim.reshape(shp)


def _s5_kernel(u_ref, sg_ref, h0re_ref, h0im_ref, wbre_ref, wbim_ref, wc_ref, are_ref, aim_ref, d_ref,
               wglu_ref, bglu_ref, y_ref, hre_out, him_out, hre, him, cre, cim, ys, *, nb, tokens):
    step = pl.program_id(0)
    rows = nb * tokens

    @pl.when(step == 0)
    def _():
        cre[...] = h0re_ref[...]
        cim[...] = h0im_ref[...]

    for m in range(4):
        um = u_ref[:, m * LANES:(m + 1) * LANES].astype(BF16)
        hre[:, m * 512:(m + 1) * 512] = jnp.dot(um, wbre_ref[m], preferred_element_type=F32)
        him[:, m * 512:(m + 1) * 512] = jnp.dot(um, wbim_ref[m], preferred_element_type=F32)

    for qd in range(2):
        ql = slice(qd * 1024, (qd + 1) * 1024)
        ar = jnp.broadcast_to(are_ref[:, ql], (nb, 1024))
        ai = jnp.broadcast_to(aim_ref[:, ql], (nb, 1024))

        def tok(t, carry, ql=ql, ar=ar, ai=ai):
            hr, hi = carry
            sel = pl.ds(pl.multiple_of(t * nb, nb), nb)
            nr = ar * hr - ai * hi + hre[sel, ql]
            ni = ar * hi + ai * hr + him[sel, ql]
            hre[sel, ql] = nr
            him[sel, ql] = ni
            return nr, ni

        hr, hi = lax.fori_loop(0, tokens, tok, (cre[:, ql], cim[:, ql]))
        cre[:, ql] = hr
        cim[:, ql] = hi

    for m in range(4):
        hc = jnp.concatenate([hre[:, m * 512:(m + 1) * 512], him[:, m * 512:(m + 1) * 512]], axis=1)
        sl = slice(m * LANES, (m + 1) * LANES)
        y = _mm(hc, wc_ref[m]) + d_ref[:, sl] * u_ref[:, sl]
        ys[:, sl] = jax.nn.gelu(y)
    y = ys[...]
    gate = _sigmoid(_mm(y, wglu_ref[...]) + bglu_ref[...])
    y_ref[...] = (y * gate * _silu(sg_ref[...])).astype(BF16)

    @pl.when(step == pl.num_programs(0) - 1)
    def _():
        hre_out[...] = cre[...]
        him_out[...] = cim[...]


def _s5(u, sg, h0re, h0im, w, nb, tokens):
    t = u.shape[0]
    rows = nb * tokens
    row = lambda c: pl.BlockSpec((rows, c), lambda i: (i, 0))
    weights = (w["wbre"], w["wbim"], w["wc"], w["are"], w["aim"], w["d"], w["wglu"], w["bglu"])
    state = jax.ShapeDtypeStruct((nb, S5_STATE), F32)
    return pl.pallas_call(
        functools.partial(_s5_kernel, nb=nb, tokens=tokens),
        grid=(t // rows,),
        in_specs=[row(S5_WIDTH), row(S5_WIDTH), _const_spec(h0re.shape), _const_spec(h0im.shape)]
        + [_const_spec(a.shape) for a in weights],
        out_specs=(row(S5_WIDTH), _const_spec((nb, S5_STATE)), _const_spec((nb, S5_STATE))),
        out_shape=(jax.ShapeDtypeStruct((t, S5_WIDTH), BF16), state, state),
        scratch_shapes=[
            pltpu.VMEM((rows, S5_STATE), F32),
            pltpu.VMEM((rows, S5_STATE), F32),
            pltpu.VMEM((nb, S5_STATE), F32),
            pltpu.VMEM((nb, S5_STATE), F32),
            pltpu.VMEM((rows, S5_WIDTH), F32),
        ],
        compiler_params=_params(),
        name="s5",
    )(u, sg, h0re, h0im, *weights)


def _dn_solve(lms, rhss):
    nblk = CHUNK // SUB
    row = lax.broadcasted_iota(jnp.int32, (SUB, LANES), 0)
    lane = lax.broadcasted_iota(jnp.int32, (SUB, LANES), 1)
    seg = (lane // SUB) * SUB
    in_diag = [(lane >= SUB * blk) & (lane < SUB * (blk + 1)) for blk in range(nblk)]
    eye_pack = jnp.where((lane - seg == row) & (lane < CHUNK), 1.0, 0.0)
    zpad = jnp.zeros((CHUNK, LANES - CHUNK), F32)
    t_invs, lowers = [], []
    for lm in lms:
        wide = jnp.concatenate([lm, zpad], axis=1)
        tiles = [wide[SUB * blk:SUB * (blk + 1)] for blk in range(nblk)]
        dpack = jnp.zeros((SUB, LANES), F32)
        for blk in range(nblk):
            dpack = jnp.where(in_diag[blk], tiles[blk], dpack)
        inv = eye_pack
        for j in range(SUB - 1):
            col_j = jnp.take_along_axis(dpack, seg + j, axis=1)
            inv = inv - col_j * inv[j:j + 1, :]
        t_invs.append(jnp.concatenate(
            [jnp.where(in_diag[blk], inv, 0.0)[:, :CHUNK] for blk in range(nblk)], axis=0).astype(BF16))
        lowers.append(jnp.concatenate(
            [jnp.where(in_diag[blk], 0.0, tiles[blk])[:, :CHUNK] for blk in range(nblk)], axis=0).astype(BF16))
    d = lambda a, b: jnp.dot(a, b, preferred_element_type=F32)
    his = [rhs.astype(BF16) for rhs in rhss]
    los = [(rhs - hi.astype(F32)).astype(BF16) for rhs, hi in zip(rhss, his)]
    width = rhss[0].shape[1]
    applied = [d(t, jnp.concatenate([hi, lo_, lw], axis=1)) for t, hi, lo_, lw in zip(t_invs, his, los, lowers)]
    trs = [ap[:, :width] + ap[:, width:2 * width] for ap in applied]
    tns = [ap[:, 2 * width:] for ap in applied]
    xs = [[tr[0:SUB]] for tr in trs]
    for blk in range(1, nblk):
        lo = SUB * blk
        curs = [tr[lo:lo + SUB] - _mm(tn[lo:lo + SUB, 0:lo], jnp.concatenate(x, axis=0))
                for tr, tn, x in zip(trs, tns, xs)]
        for x, cur in zip(xs, curs):
            x.append(cur)
    return [jnp.concatenate(x, axis=0) for x in xs]


def _dn_prompt_kernel(x_ref, z_ref, beta_ref, gd_ref, cw_ref, nw_ref, c0_ref,
                      og_ref, sout_ref, cout_ref,
                      xs, qs, ks, vs, bcol, gcol, gc, gt, oscr, *, nb):
    rows = CHUNK * nb
    hist = (CONV_W - 1) * nb
    pad = PAD_TOKENS * nb
    step = pl.program_id(0)
    st = sout_ref

    @pl.when(step == 0)
    def _():
        st[...] = jnp.zeros_like(st)
        xs[0:hist, :] = c0_ref[...]
        gc[0, 0:pad, :] = jnp.zeros((pad, LANES), F32)

    xs[hist:hist + 64, :] = x_ref[0:64, :]

    def conv_rows(src, base, r0):
        tap = lambda i: base + r0 + i * nb if isinstance(r0, int) else pl.multiple_of(base + r0 + i * nb, 8)
        for c in range(DN_CONV_DIM // LANES):
            cl = slice(c * LANES, (c + 1) * LANES)
            y = src[pl.ds(tap(0), 64), cl] * cw_ref[0:1, cl]
            for i in range(1, CONV_W):
                y = y + src[pl.ds(tap(i), 64), cl] * cw_ref[i:i + 1, cl]
            y = _silu(y)
            if c < 2 * DN_HEADS:
                y = y * lax.rsqrt(jnp.sum(y * y, axis=-1, keepdims=True) + EPS)
            if c < DN_HEADS:
                qs[c, pl.ds(r0, 64), :] = y * (DN_DK ** -0.5)
            elif c < 2 * DN_HEADS:
                ks[c - DN_HEADS, pl.ds(r0, 64), :] = y
            else:
                vs[c - 2 * DN_HEADS, pl.ds(r0, 64), :] = y

    gc[0, pad:pad + rows, :] = gd_ref[...]
    _cumsum_tokens(gc, 0, pad, rows, nb)

    def bcast_rows(r0):
        gtile = gc[0, pl.ds(pad + r0, 64), :]
        btile = beta_ref[pl.ds(r0, 64), :]
        for h in range(DN_HEADS):
            gcol[h, pl.ds(r0, 64), :] = jnp.broadcast_to(gtile[:, h:h + 1], (64, LANES))
            bcol[h, pl.ds(r0, 64), :] = jnp.broadcast_to(btile[:, h:h + 1], (64, LANES))

    conv_rows(xs, 0, 0)
    bcast_rows(0)

    def conv_tile(ti, carry):
        r0 = pl.multiple_of(ti * 64, 64)
        conv_rows(x_ref, -hist, r0)
        bcast_rows(r0)
        return carry

    lax.fori_loop(1, rows // 64, conv_tile, 0)

    xs[0:hist, :] = x_ref[rows - hist:rows, :]

    ii = lax.broadcasted_iota(jnp.int32, (CHUNK, CHUNK), 0)
    jj = lax.broadcasted_iota(jnp.int32, (CHUNK, CHUNK), 1)
    eye = (ii == jj).astype(BF16)

    parts = [_split3(gc[0, pl.ds(pad + b, CHUNK, stride=nb), :]) for b in range(nb)]
    tdot = lambda x: lax.dot_general(x, eye, (((0,), (0,)), ((), ())), preferred_element_type=F32)
    firsts = [tdot(p[0]) for p in parts]
    seconds = [tdot(p[1]) for p in parts]
    thirds = [tdot(p[2]) for p in parts]
    for b in range(nb):
        gt[b] = (firsts[b] + seconds[b]) + thirds[b]

    def per_batch(bi, carry):
        def head_group(gi, carry2):
            probs = [(bi * DN_BATCH + jb, gi * DN_GROUP + jh) for jb in range(DN_BATCH) for jh in range(DN_GROUP)]
            sels = [pl.ds(b, CHUNK, stride=nb) for b, _ in probs]
            loaded = [(qs[h, sel, :], ks[h, sel, :], vs[h, sel, :], bcol[h, sel, :], gcol[h, sel, :],
                       gt[b, pl.ds(h, 1), :], st[b, h]) for (b, h), sel in zip(probs, sels)]
            qbs, kbs, vbs, bcs, gcls, grows, s_olds = zip(*loaded)
            decs = [jnp.where(jj <= ii, jnp.exp(jnp.where(jj <= ii, gcl[:, :CHUNK] - grow, 0.0)), 0.0)
                    for gcl, grow in zip(gcls, grows)]
            kqs = [_mm_nt(jnp.concatenate([kb, qb], axis=0), kb) for qb, kb in zip(qbs, kbs)]
            kks = [kq[:CHUNK] for kq in kqs]
            qks = [kq[CHUNK:] for kq in kqs]
            lms = [jnp.where(jj < ii, bc[:, :CHUNK] * kk * dec, 0.0) for bc, kk, dec in zip(bcs, kks, decs)]
            egs = [jnp.exp(gcl) for gcl in gcls]
            rhss = [jnp.concatenate([vb * bc, kb * (bc * eg)], axis=1)
                    for vb, kb, bc, eg in zip(vbs, kbs, bcs, egs)]
            sols = _dn_solve(lms, rhss)
            v_news = [sol[:, :DN_DV] - _mm(sol[:, DN_DV:], s_old) for sol, s_old in zip(sols, s_olds)]
            os_ = [_mm(qb * eg, s_old) + _mm(qk * dec, v_new)
                   for qb, eg, s_old, qk, dec, v_new in zip(qbs, egs, s_olds, qks, decs, v_news)]
            glasts = [gcl[CHUNK - 1:CHUNK, :] for gcl in gcls]
            s_news = [s_old * jnp.exp(glast) + _mm_tn(kb * jnp.exp(glast - gcl), v_new)
                      for s_old, glast, kb, gcl, v_new in zip(s_olds, glasts, kbs, gcls, v_news)]
            for (b, h), sel, o, s_new in zip(probs, sels, os_, s_news):
                st[b, h] = s_new
                oscr[h, sel, :] = o
            return carry2

        lax.fori_loop(0, DN_HEADS // DN_GROUP, head_group, 0)
        return carry

    lax.fori_loop(0, nb // DN_BATCH, per_batch, 0)

    nw = nw_ref[...]

    def epi_tile(ti, carry):
        r0 = pl.multiple_of(ti * 64, 64)
        for h in range(DN_HEADS):
            o = oscr[h, pl.ds(r0, 64), :]
            inv = lax.rsqrt(jnp.mean(o * o, axis=-1, keepdims=True) + EPS)
            cl = slice(h * LANES, (h + 1) * LANES)
            og_ref[pl.ds(r0, 64), cl] = (o * inv * nw * _silu(z_ref[pl.ds(r0, 64), cl])).astype(BF16)
        return carry

    lax.fori_loop(0, rows // 64, epi_tile, 0)

    @pl.when(step == pl.num_programs(0) - 1)
    def _():
        cout_ref[...] = xs[0:hist, :]


def _dn_prompt(xqkv, z, beta, gd, cw, nw, c0, nb):
    t = z.shape[0]
    rows = CHUNK * nb
    hist = (CONV_W - 1) * nb
    row = lambda c: pl.BlockSpec((rows, c), lambda i: (i, 0))
    slab = pltpu.VMEM((DN_HEADS, rows, LANES), F32)
    return pl.pallas_call(
        functools.partial(_dn_prompt_kernel, nb=nb),
        grid=(t // rows,),
        in_specs=[row(DN_CONV_DIM), row(1024), row(LANES), row(LANES), _const_spec(cw.shape),
                  _const_spec(nw.shape), _const_spec(c0.shape)],
        out_specs=(row(1024), _const_spec((nb, DN_HEADS, DN_DK, DN_DV)), _const_spec((hist, DN_CONV_DIM))),
        out_shape=(jax.ShapeDtypeStruct((t, 1024), BF16),
                   jax.ShapeDtypeStruct((nb, DN_HEADS, DN_DK, DN_DV), F32),
                   jax.ShapeDtypeStruct((hist, DN_CONV_DIM), F32)),
        scratch_shapes=[
            pltpu.VMEM((hist + 64, DN_CONV_DIM), F32),
            slab, slab, slab, slab, slab,
            pltpu.VMEM((1, PAD_TOKENS * nb + rows, LANES), F32),
            pltpu.VMEM((nb, LANES, CHUNK), F32),
            slab,
        ],
        compiler_params=_params(),
        name="dn_prompt",
    )(xqkv, z, beta, gd, cw, nw, c0)


def _dn_step_kernel(x_ref, cb_ref, z_ref, beta_ref, gd_ref, s_ref, cw_ref, nw_ref, *rest, layer, first):
    og_ref, sout_ref, cn_ref = rest[-3:]
    sout_ref = _own_state_block(sout_ref, layer, first)
    nw = nw_ref[...]
    cn_ref[0] = cb_ref[1]
    cn_ref[1] = cb_ref[2]
    cn_ref[2] = x_ref[...]
    beta = beta_ref[...]
    eg_all = jnp.exp(gd_ref[...])

    def conv(c):
        cl = slice(c * LANES, (c + 1) * LANES)
        y = cb_ref[0, :, cl] * cw_ref[0:1, cl]
        y = y + cb_ref[1, :, cl] * cw_ref[1:2, cl]
        y = y + cb_ref[2, :, cl] * cw_ref[2:3, cl]
        y = y + x_ref[:, cl] * cw_ref[3:4, cl]
        return _silu(y)

    def l2(y):
        return y * lax.rsqrt(jnp.sum(y * y, axis=-1, keepdims=True) + EPS)

    prow = lax.broadcasted_iota(jnp.int32, (16, 2 * LANES), 0)
    plane = lax.broadcasted_iota(jnp.int32, (16, 2 * LANES), 1)
    pick = [jnp.where(((prow == n) & (plane < LANES)) | ((prow == 8 + n) & (plane >= LANES)), 1.0, 0.0).astype(BF16)
            for n in range(8)]

    for h in range(DN_HEADS):
        qh = l2(conv(h)) * (DN_DK ** -0.5)
        kh = l2(conv(DN_HEADS + h))
        vh = conv(2 * DN_HEADS + h)
        kq = jnp.concatenate([kh, qh], axis=0).astype(BF16)
        cols = [lax.dot_general(kq, pick[n], (((0,), (0,)), ((), ())), preferred_element_type=F32)
                for n in range(8)]
        outs = []
        for n in range(8):
            s_old = s_ref[n, h]
            kcol = cols[n][:, :LANES]
            eg = eg_all[n:n + 1, h:h + 1]
            ks_row = jnp.sum(kcol * s_old, axis=0, keepdims=True)
            v_new = beta[n:n + 1, h:h + 1] * (vh[n:n + 1, :] - eg * ks_row)
            s_new = s_old * eg + kcol * v_new
            sout_ref[n, h] = s_new
            outs.append(jnp.sum(cols[n][:, LANES:] * s_new, axis=0, keepdims=True))
        o = jnp.concatenate(outs, axis=0)
        cl = slice(h * LANES, (h + 1) * LANES)
        og_ref[:, cl] = o * lax.rsqrt(jnp.mean(o * o, axis=-1, keepdims=True) + EPS) * nw * _silu(z_ref[:, cl])


def _dn_step(xqkv, cbuf, z, beta, gd, states, layer, prev, cw, nw):
    n = z.shape[0]
    row = lambda c: pl.BlockSpec((8, c), lambda i: (i, 0))
    cspec = pl.BlockSpec((CONV_W - 1, 8, DN_CONV_DIM), lambda i: (0, i, 0))
    sspec, ospec, extra_specs, extra_args, aliases = _stacked_state_args(states, layer, prev, 8, 1)
    return pl.pallas_call(
        functools.partial(_dn_step_kernel, layer=layer, first=prev is None),
        grid=(n // 8,),
        in_specs=[row(DN_CONV_DIM), cspec, row(1024), row(LANES), row(LANES), sspec,
                  _const_spec(cw.shape), _const_spec(nw.shape)] + extra_specs,
        out_specs=(row(1024), ospec, cspec),
        out_shape=(jax.ShapeDtypeStruct((n, 1024), F32), jax.ShapeDtypeStruct(states.shape, F32),
                   jax.ShapeDtypeStruct(cbuf.shape, F32)),
        input_output_aliases=aliases,
        compiler_params=_params(),
        name="dn_step",
    )(xqkv, cbuf, z, beta, gd, states, cw, nw, *extra_args)


def _pad_cols(a, n):
    return jnp.pad(a, ((0, 0), (0, n - a.shape[1])))


def _even_weights(w_in, w_gate_up, b_gate, w_out):
    return {
        "w": _pad_cols(w_in.astype(BF16), W_IN_PAD),
        "wg": jnp.pad(w_gate_up, ((0, LANES - GLA_RANK), (0, 0))).astype(BF16),
        "bg": b_gate.reshape(1, -1),
        "wo": w_out.astype(BF16),
    }


def _s5_weights(ar, ai, bbre, bbim, c_re, c_im, d, w_glu, b_glu):
    eye = jnp.eye(8, dtype=F32)

    def pack_b(bb):
        t = bb.reshape(4, 8, S5_P, S5_GROUP_CH).transpose(0, 1, 3, 2)
        return (t[:, :, :, None, :] * eye[None, :, None, :, None]).reshape(4, 128, 512).astype(BF16)

    def pack_c(c):
        t = c.reshape(4, 8, S5_GROUP_CH, S5_P).transpose(0, 1, 3, 2)
        return (t[:, :, :, None, :] * eye[None, :, None, :, None]).reshape(4, 512, 128)

    return {
        "wbre": pack_b(bbre), "wbim": pack_b(bbim),
        "wc": jnp.concatenate([pack_c(c_re), -pack_c(c_im)], axis=1).astype(BF16),
        "are": ar.reshape(1, S5_STATE), "aim": ai.reshape(1, S5_STATE),
        "d": d.reshape(1, -1), "wglu": w_glu.astype(BF16), "bglu": b_glu.reshape(1, -1),
    }


def _odd_weights(w_in, conv_w, a_log, dt_bias, w_out):
    return {
        "w": _pad_cols(w_in.astype(BF16), W_IN_PAD),
        "alog": _pad_cols(a_log.reshape(1, -1), LANES), "dtb": _pad_cols(dt_bias.reshape(1, -1), LANES),
        "cw": jnp.pad(conv_w, ((0, 8 - CONV_W), (0, 0))),
        "wo": w_out.astype(BF16),
    }


def kernel(x_prompt, x_sample, state_gla, state_s5_re, state_s5_im, state_delta, state_conv, norm_w, final_norm_w, w_in_even, gla_w_gate_up, gla_b_gate, gla_norm_w, s5_lambda_re, s5_lambda_im, s5_log_dt, s5_b_re, s5_b_im, s5_c_re, s5_c_im, s5_d, s5_w_glu, s5_b_glu, w_out_even, w_in_odd, dn_conv_w, dn_a_log, dn_dt_bias, dn_norm_w, w_out_odd):
    nb, seq, _ = x_prompt.shape
    ns = x_sample.shape[0]
    assert seq % CHUNK == 0 and nb % 8 == 0 and ns % 8 == 0 and x_sample.shape[1] == 1
    depth = norm_w.shape[0]

    assert depth % 2 == 0 and state_gla.shape[0] == depth // 2 and state_delta.shape[0] == depth // 2
    xp = None
    xs = x_sample.reshape(ns, D_MODEL)
    fw = final_norm_w.reshape(1, -1)

    ar, ai, bbre, bbim = _s5_prep(s5_lambda_re, s5_lambda_im, s5_log_dt, s5_b_re, s5_b_im)

    gla_p, s5r_p, s5i_p, s5r_s, s5i_s, dn_p, cv_p, cv_s = ([] for _ in range(8))
    gla_s = dn_s = None
    for layer in range(depth):
        i = layer // 2
        nw = norm_w[layer].reshape(1, -1)
        if layer % 2 == 0:
            w = _even_weights(w_in_even[i], gla_w_gate_up[i], gla_b_gate[i], w_out_even[i])
            w5 = _s5_weights(ar[i], ai[i], bbre[i], bbim[i], s5_c_re[i], s5_c_im[i], s5_d[i],
                             s5_w_glu[i], s5_b_glu[i])
            gnw = gla_norm_w[i].reshape(1, -1)
            if layer == 0:
                q, k, g, v, r, u, sg, xp = _proj_even(x_prompt, nw, w, batch_major=True)
            else:
                q, k, g, v, r, u, sg, xp = _proj_even(xp, nw, w, pending=pending)
            og, s_new = _gla_prompt(q, k, g, v, r, gnw, nb)
            zero = jnp.zeros((nb, S5_STATE), F32)
            y5, hre, him = _s5(u, sg, zero, zero, w5, nb, CHUNK)
            pending = (og, y5, w["wo"])
            gla_p.append(s_new)
            s5r_p.append(hre.reshape(nb, S5_GROUPS, S5_P))
            s5i_p.append(him.reshape(nb, S5_GROUPS, S5_P))
            q, k, g, v, r, u, sg = _proj_even(xs, nw, w)
            og, gla_s = _gla_step(q, k, g, v, r, state_gla, i, gla_s, gnw)
            y5, hre, him = _s5(u, sg, state_s5_re[i].reshape(ns, S5_STATE), state_s5_im[i].reshape(ns, S5_STATE),
                               w5, ns, 1)
            xs = _out_even(xs, og, y5, w["wo"])
            s5r_s.append(hre.reshape(ns, S5_GROUPS, S5_P))
            s5i_s.append(him.reshape(ns, S5_GROUPS, S5_P))
        else:
            w = _odd_weights(w_in_odd[i], dn_conv_w[i], dn_a_log[i], dn_dt_bias[i], w_out_odd[i])
            dnw = dn_norm_w[i].reshape(1, -1)
            final = layer == depth - 1
            xqkv, z, beta, gd, xp = _proj_odd(xp, nw, w, pending=pending)
            c0 = jnp.zeros(((CONV_W - 1) * nb, DN_CONV_DIM), F32)
            og, s_new, c_new = _dn_prompt(xqkv, z, beta, gd, w["cw"], dnw, c0, nb)
            if final:
                xp = _out_odd(xp, og, w["wo"], fw, final, nb_out=nb)
            else:
                pending = (og, w["wo"])
            dn_p.append(s_new)
            cv_p.append(c_new.reshape(CONV_W - 1, nb, DN_CONV_DIM).transpose(1, 0, 2))
            xqkv, z, beta, gd = _proj_odd(xs, nw, w)
            og, dn_s, c_new = _dn_step(xqkv, state_conv[i].transpose(1, 0, 2), z, beta, gd, state_delta, i, dn_s,
                                       w["cw"], dnw)
            xs = _out_odd(xs, og, w["wo"], fw, final)
            cv_s.append(c_new.transpose(1, 0, 2))

    y_prompt = xp
    y_sample = xs.reshape(ns, 1, D_MODEL)
    st = jnp.stack
    return (y_prompt, y_sample, st(gla_p), gla_s, st(s5r_p), st(s5i_p), st(s5r_s), st(s5i_s),
            st(dn_p), dn_s, st(cv_p), st(cv_s))
```
